```python
import math
import jax, jax.numpy as jnp
from jax import lax
import numpy as np

D_MODEL = 4096
BATCH = 4
SEQ = 2048
DEPTH = 1

CHUNK = 64
Q_BLOCK = 128
EPS = 1e-6

MLA_HEADS = 16
QK_NOPE = 128
QK_ROPE = 64
V_HEAD = 128
Q_RANK = 768
KV_RANK = 512
ROPE_THETA = 10000.0
MLA_WIDTH = MLA_HEADS * V_HEAD

SSM_WIDTH = D_MODEL - MLA_WIDTH
SSM_GROUP_CH = 16
SSM_GROUPS = SSM_WIDTH // SSM_GROUP_CH
SSM_STATE = 64
DT_MIN = 1e-3
DT_MAX = 1e-1

D_MIX = MLA_WIDTH + SSM_WIDTH
IN_COLS = Q_RANK + KV_RANK + QK_ROPE + SSM_WIDTH

N_EGROUPS = 8
EXPERTS_PER_GROUP = 8
N_EXPERTS = N_EGROUPS * EXPERTS_PER_GROUP
TOP_K = 2
D_EXPERT = 512
MOE_BLOCK = 128

kernel_name = "hymba_mla_s5_hier_moe_adaln"


def rms_norm(x, gain):
    xf = x.astype(jnp.float32)
    y = xf * lax.rsqrt(jnp.mean(xf * xf, axis=-1, keepdims=True) + EPS)
    return (y * gain.astype(jnp.float32)).astype(x.dtype)


def modulate(h, shift, scale):
    return h * (1.0 + scale[:, None, :]) + shift[:, None, :]


def rope(x, cos, sin):
    half = x.shape[-1] // 2
    x1, x2 = x[..., :half], x[..., half:]
    return jnp.concatenate([x1 * cos - x2 * sin, x2 * cos + x1 * sin], axis=-1)


def mla_group(q_lat, kv_lat, k_rope_raw, positions, q_lat_gain, w_uq, kv_lat_gain, w_ukv):
    bsz, seq, _ = q_lat.shape
    q = (rms_norm(q_lat, q_lat_gain) @ w_uq).reshape(bsz, seq, MLA_HEADS, QK_NOPE + QK_ROPE)
    kv = (rms_norm(kv_lat, kv_lat_gain) @ w_ukv).reshape(bsz, seq, MLA_HEADS, QK_NOPE + V_HEAD)
    q_nope, q_rope = q[..., :QK_NOPE], q[..., QK_NOPE:]
    k_nope, v = kv[..., :QK_NOPE], kv[..., QK_NOPE:]

    inv_freq = jnp.exp(-math.log(ROPE_THETA) * jnp.arange(0, QK_ROPE, 2, dtype=jnp.float32) / QK_ROPE)
    ang = positions.astype(jnp.float32)[..., None] * inv_freq
    cos = jnp.cos(ang).astype(q.dtype)
    sin = jnp.sin(ang).astype(q.dtype)
    q_rope = rope(q_rope, cos[:, :, None, :], sin[:, :, None, :])
    k_rope = rope(k_rope_raw, cos, sin)

    scale = (QK_NOPE + QK_ROPE) ** -0.5
    outs = []
    for blk in range(seq // Q_BLOCK):
        q0 = blk * Q_BLOCK
        kend = q0 + Q_BLOCK
        s = (jnp.einsum('bqhd,bkhd->bhqk', q_nope[:, q0:kend], k_nope[:, :kend])
             + jnp.einsum('bqhr,bkr->bhqk', q_rope[:, q0:kend], k_rope[:, :kend]))
        s = s.astype(jnp.float32) * scale
        q_chunk = (q0 + jnp.arange(Q_BLOCK)) // CHUNK
        k_chunk = jnp.arange(kend) // CHUNK
        mask = k_chunk[None, :] <= q_chunk[:, None]
        p = jax.nn.softmax(jnp.where(mask, s, -jnp.inf), axis=-1).astype(v.dtype)
        outs.append(jnp.einsum('bhqk,bkhd->bqhd', p, v[:, :kend]))
    return jnp.concatenate(outs, axis=1).reshape(bsz, seq, MLA_WIDTH)


def _complex_affine_combine(left, right):
    ar1, ai1, br1, bi1 = left
    ar2, ai2, br2, bi2 = right
    ar = ar2 * ar1 - ai2 * ai1
    ai = ar2 * ai1 + ai2 * ar1
    ar2b, ai2b = ar2[:, None], ai2[:, None]
    br = ar2b * br1 - ai2b * bi1 + br2
    bi = ar2b * bi1 + ai2b * br1 + bi2
    return ar, ai, br, bi


def s5_group(u, lam_re, lam_im, log_dt, b_re, b_im, c_re, c_im, d_skip, w_glu, b_glu):
    bsz, seq, _ = u.shape
    ug = u.reshape(bsz, seq, SSM_GROUPS, SSM_GROUP_CH)
    dt = jnp.exp(log_dt)[:, None]
    mag = jnp.exp(lam_re * dt)
    abar_re = mag * jnp.cos(lam_im * dt)
    abar_im = mag * jnp.sin(lam_im * dt)
    nr, ni = abar_re - 1.0, abar_im
    den = lam_re * lam_re + lam_im * lam_im
    f_re = (nr * lam_re + ni * lam_im) / den
    f_im = (ni * lam_re - nr * lam_im) / den
    bbar_re = f_re[..., None] * b_re - f_im[..., None] * b_im
    bbar_im = f_re[..., None] * b_im + f_im[..., None] * b_re
    bu_re = jnp.einsum('blgn,gpn->lbgp', ug, bbar_re)
    bu_im = jnp.einsum('blgn,gpn->lbgp', ug, bbar_im)
    a_re = jnp.broadcast_to(abar_re, (seq,) + abar_re.shape)
    a_im = jnp.broadcast_to(abar_im, (seq,) + abar_im.shape)
    _, _, h_re, h_im = lax.associative_scan(_complex_affine_combine, (a_re, a_im, bu_re, bu_im), axis=0)
    y = (jnp.einsum('lbgp,gnp->blgn', h_re, c_re) - jnp.einsum('lbgp,gnp->blgn', h_im, c_im)
         + d_skip.reshape(SSM_GROUPS, SSM_GROUP_CH) * ug)
    g = jax.nn.gelu(y.reshape(bsz, seq, SSM_WIDTH))
    return g * jax.nn.sigmoid(g @ w_glu + b_glu)


def hierarchical_moe(h, w_grp, b_grp, w_erouter, b_erouter, w1, w3, w2):
    bsz, seq, dm = h.shape
    t = bsz * seq
    hf = h.reshape(t, dm)
    g_logits = (hf @ w_grp).astype(jnp.float32) + b_grp.astype(jnp.float32)
    g_prob = jax.nn.softmax(g_logits, axis=-1)
    _, grp = lax.top_k(g_logits, 1)
    p_grp = jnp.take_along_axis(g_prob, grp, axis=1)[:, 0]
    e_logits = ((hf @ w_erouter).astype(jnp.float32) + b_erouter.astype(jnp.float32)
                ).reshape(t, N_EGROUPS, EXPERTS_PER_GROUP)
    e_in = jnp.take_along_axis(e_logits, grp[:, :, None], axis=1)[:, 0]
    top_v, top_i = lax.top_k(e_in, TOP_K)
    weights = p_grp[:, None] * jax.nn.softmax(top_v, axis=-1)
    expert = grp * EXPERTS_PER_GROUP + top_i

    n_assign = t * TOP_K
    e_flat = expert.reshape(n_assign)
    w_flat = weights.reshape(n_assign)
    tok_flat = jnp.repeat(jnp.arange(t, dtype=jnp.int32), TOP_K)
    order = jnp.argsort(e_flat)
    e_sorted, tok_sorted, w_sorted = e_flat[order], tok_flat[order], w_flat[order]
    counts = jnp.bincount(e_flat, length=N_EXPERTS)
    padded = ((counts + MOE_BLOCK - 1) // MOE_BLOCK) * MOE_BLOCK
    starts = jnp.cumsum(counts) - counts
    pends = jnp.cumsum(padded)
    pstarts = pends - padded
    dest = pstarts[e_sorted] + jnp.arange(n_assign) - starts[e_sorted]
    n_blocks = (n_assign + N_EXPERTS * (MOE_BLOCK - 1)) // MOE_BLOCK
    rows = n_blocks * MOE_BLOCK
    row_tok = jnp.full((rows,), t, jnp.int32).at[dest].set(tok_sorted)
    row_w = jnp.zeros((rows,), jnp.float32).at[dest].set(w_sorted)
    block_expert = jnp.minimum(
        jnp.searchsorted(pends, jnp.arange(n_blocks) * MOE_BLOCK, side='right'), N_EXPERTS - 1)
    h_pad = jnp.concatenate([hf, jnp.zeros((1, dm), hf.dtype)], axis=0)

    def run_block(args):
        tok, e = args
        xb = h_pad[tok]
        return (jax.nn.silu(xb @ w1[e]) * (xb @ w3[e])) @ w2[e]

    ys = lax.map(run_block, (row_tok.reshape(n_blocks, MOE_BLOCK), block_expert))
    ys = ys.reshape(rows, dm) * row_w[:, None].astype(ys.dtype)
    out = jnp.zeros((t + 1, dm), ys.dtype).at[row_tok].add(ys)
    return out[:t].reshape(bsz, seq, dm)


def setup_inputs(seed: int = 0) -> dict:
    key = jax.random.key(seed)
    ks = iter(jax.random.split(key, 40))
    f32 = jnp.float32

    def nrm(shape, scale):
        return jax.random.normal(next(ks), shape, f32) * scale

    def gain(shape):
        return 1.0 + nrm(shape, 0.01)

    L = DEPTH
    offsets = jax.random.randint(next(ks), (BATCH, 1), 0, 8192, dtype=jnp.int32)
    positions = offsets + jnp.arange(SEQ, dtype=jnp.int32)[None, :]
    lam_im = (math.pi * jnp.arange(SSM_STATE, dtype=f32))[None, None, :] + nrm((L, SSM_GROUPS, SSM_STATE), 0.01)
    return {
        "x": nrm((BATCH, SEQ, D_MODEL), 1.0),
        "c": nrm((BATCH, D_MODEL), 1.0),
        "positions": positions,
        "w_ada": nrm((L, D_MODEL, 6 * D_MODEL), 0.5 * D_MODEL ** -0.5),
        "b_ada": nrm((L, 6 * D_MODEL), 0.01),
        "norm_mix_gain": gain((L, D_MODEL)),
        "w_in": nrm((L, D_MODEL, IN_COLS), D_MODEL ** -0.5),
        "q_lat_gain": gain((L, Q_RANK)),
        "w_uq": nrm((L, Q_RANK, MLA_HEADS * (QK_NOPE + QK_ROPE)), Q_RANK ** -0.5),
        "kv_lat_gain": gain((L, KV_RANK)),
        "w_ukv": nrm((L, KV_RANK, MLA_HEADS * (QK_NOPE + V_HEAD)), KV_RANK ** -0.5),
        "ssm_lam_re": -0.5 * (1.0 + nrm((L, SSM_GROUPS, SSM_STATE), 0.01)),
        "ssm_lam_im": lam_im,
        "ssm_log_dt": jax.random.uniform(next(ks), (L, SSM_GROUPS), f32, math.log(DT_MIN), math.log(DT_MAX)),
        "ssm_b_re": nrm((L, SSM_GROUPS, SSM_STATE, SSM_GROUP_CH), (2 * SSM_GROUP_CH) ** -0.5),
        "ssm_b_im": nrm((L, SSM_GROUPS, SSM_STATE, SSM_GROUP_CH), (2 * SSM_GROUP_CH) ** -0.5),
        "ssm_c_re": nrm((L, SSM_GROUPS, SSM_GROUP_CH, SSM_STATE), (2 * SSM_STATE) ** -0.5),
        "ssm_c_im": nrm((L, SSM_GROUPS, SSM_GROUP_CH, SSM_STATE), (2 * SSM_STATE) ** -0.5),
        "ssm_d": nrm((L, SSM_WIDTH), 1.0),
        "w_glu": nrm((L, SSM_WIDTH, SSM_WIDTH), SSM_WIDTH ** -0.5),
        "b_glu": nrm((L, SSM_WIDTH), 0.01),
        "mla_out_gain": gain((L, MLA_WIDTH)),
        "ssm_out_gain": gain((L, SSM_WIDTH)),
        "w_out": nrm((L, D_MIX, D_MODEL), D_MIX ** -0.5),
        "norm_ffn_gain": gain((L, D_MODEL)),
        "w_group_router": nrm((L, D_MODEL, N_EGROUPS), D_MODEL ** -0.5),
        "b_group_router": nrm((L, N_EGROUPS), 0.01),
        "w_expert_router": nrm((L, D_MODEL, N_EXPERTS), D_MODEL ** -0.5),
        "b_expert_router": nrm((L, N_EXPERTS), 0.01),
        "w1_experts": nrm((L, N_EXPERTS, D_MODEL, D_EXPERT), D_MODEL ** -0.5),
        "w3_experts": nrm((L, N_EXPERTS, D_MODEL, D_EXPERT), D_MODEL ** -0.5),
        "w2_experts": nrm((L, N_EXPERTS, D_EXPERT, D_MODEL), D_EXPERT ** -0.5),
        "final_gain": gain((D_MODEL,)),
    }


def reference(x, c, positions, w_ada, b_ada, norm_mix_gain, w_in, q_lat_gain, w_uq, kv_lat_gain, w_ukv,
              ssm_lam_re, ssm_lam_im, ssm_log_dt, ssm_b_re, ssm_b_im, ssm_c_re, ssm_c_im, ssm_d,
              w_glu, b_glu, mla_out_gain, ssm_out_gain, w_out, norm_ffn_gain,
              w_group_router, b_group_router, w_expert_router, b_expert_router,
              w1_experts, w3_experts, w2_experts, final_gain):
    c_act = jax.nn.silu(c)
    s0 = Q_RANK
    s1 = s0 + KV_RANK
    s2 = s1 + QK_ROPE
    for i in range(DEPTH):
        mod = c_act @ w_ada[i] + b_ada[i]
        sh_a, sc_a, g_a, sh_f, sc_f, g_f = jnp.split(mod, 6, axis=-1)

        h = modulate(rms_norm(x, norm_mix_gain[i]), sh_a, sc_a)
        z = h @ w_in[i]
        o_mla = mla_group(z[..., :s0], z[..., s0:s1], z[..., s1:s2], positions,
                          q_lat_gain[i], w_uq[i], kv_lat_gain[i], w_ukv[i])
        o_ssm = s5_group(z[..., s2:], ssm_lam_re[i], ssm_lam_im[i], ssm_log_dt[i],
                         ssm_b_re[i], ssm_b_im[i], ssm_c_re[i], ssm_c_im[i], ssm_d[i],
                         w_glu[i], b_glu[i])
        o = jnp.concatenate([rms_norm(o_mla, mla_out_gain[i]), rms_norm(o_ssm, ssm_out_gain[i])], axis=-1)
        x = x + g_a[:, None, :] * (o @ w_out[i])

        h = modulate(rms_norm(x, norm_ffn_gain[i]), sh_f, sc_f)
        x = x + g_f[:, None, :] * hierarchical_moe(h, w_group_router[i], b_group_router[i],
                                                   w_expert_router[i], b_expert_router[i],
                                                   w1_experts[i], w3_experts[i], w2_experts[i])
    return rms_norm(x, final_gain)
```

```python
import math

import jax
import jax.numpy as jnp
from jax import lax
from jax.experimental import pallas as pl
from jax.experimental.pallas import tpu as pltpu

F32 = jnp.float32
BF16 = jnp.bfloat16

D_MODEL = 4096
BATCH = 4
SEQ = 2048
TOKENS = BATCH * SEQ
CHUNK = 64
EPS = 1e-6

MLA_HEADS = 16
QK_NOPE = 128
QK_ROPE = 64
QK_HEAD = QK_NOPE + QK_ROPE
V_HEAD = 128
Q_RANK = 768
KV_RANK = 512
ROPE_THETA = 10000.0
MLA_WIDTH = MLA_HEADS * V_HEAD

SSM_WIDTH = D_MODEL - MLA_WIDTH
SSM_GROUP_CH = 16
SSM_GROUPS = SSM_WIDTH // SSM_GROUP_CH
SSM_STATE = 64

N_EGROUPS = 8
EXPERTS_PER_GROUP = 8
N_EXPERTS = N_EGROUPS * EXPERTS_PER_GROUP
TOP_K = 2
D_EXPERT = 512

LANES = 128
SUBLANES = 8
VMEM_LIMIT = 56 * 1024 * 1024

Z_Q = 0
Z_SSM = Q_RANK
Z_KV = Z_SSM + SSM_WIDTH
Z_KR = Z_KV + KV_RANK
Z_COLS = Z_KR + 2 * QK_ROPE

ATT_BLK = 256
ATT_NBLK = SEQ // ATT_BLK

SSM_GB = 8
SSM_GB_CH = SSM_GB * SSM_GROUP_CH
SSM_GB_ST = SSM_GB * SSM_STATE
SSM_PAIRS = SSM_GROUPS // (2 * SSM_GB)
SSM_TCHUNK = 256

MOE_BLK = 256
MOE_NBLK = (TOKENS * TOP_K + N_EXPERTS * (MOE_BLK - 1)) // MOE_BLK + 1
MOE_ROWS = MOE_NBLK * MOE_BLK
DMA_UNROLL = 8
CMB_TM = 128


def _cparams(n_axes, vmem=VMEM_LIMIT):
    return pltpu.CompilerParams(dimension_semantics=("arbitrary",) * n_axes, vmem_limit_bytes=vmem)


def _ada_kernel(c_ref, w_ref, b_ref, o_ref):
    c = c_ref[...]
    ca = (c * jax.nn.sigmoid(c)).astype(BF16)
    o_ref[...] = jnp.dot(ca, w_ref[...].astype(BF16), preferred_element_type=F32) + b_ref[...]


def _ada(c8, w, b):
    n = w.shape[1]
    tn = 512
    return pl.pallas_call(
        _ada_kernel,
        grid=(n // tn,),
        in_specs=[pl.BlockSpec((SUBLANES, D_MODEL), lambda j: (0, 0)),
                  pl.BlockSpec((D_MODEL, tn), lambda j: (0, j)),
                  pl.BlockSpec((1, tn), lambda j: (0, j))],
        out_specs=pl.BlockSpec((SUBLANES, tn), lambda j: (0, j)),
        out_shape=jax.ShapeDtypeStruct((SUBLANES, n), F32),
        compiler_params=_cparams(1),
        name="ada_mod",
    )(c8, w, b)


def _rms(x, gain):
    return x * lax.rsqrt(jnp.mean(x * x, axis=-1, keepdims=True) + EPS) * gain


def _norm_mod_kernel(x_ref, g_ref, sh_ref, sc_ref, o_ref):
    y = _rms(x_ref[...], g_ref[...])
    o_ref[...] = (y * (1.0 + sc_ref[...]) + sh_ref[...]).astype(o_ref.dtype)


def _norm_mod(x2, gain, mod3, shift_chunk, scale_chunk):
    tm = 256
    per_b = SEQ // tm
    return pl.pallas_call(
        _norm_mod_kernel,
        grid=(TOKENS // tm,),
        in_specs=[pl.BlockSpec((tm, D_MODEL), lambda i: (i, 0)),
                  pl.BlockSpec((1, D_MODEL), lambda i: (0, 0)),
                  pl.BlockSpec((None, 1, D_MODEL), lambda i: (i // per_b, 0, shift_chunk)),
                  pl.BlockSpec((None, 1, D_MODEL), lambda i: (i // per_b, 0, scale_chunk))],
        out_specs=pl.BlockSpec((tm, D_MODEL), lambda i: (i, 0)),
        out_shape=jax.ShapeDtypeStruct((TOKENS, D_MODEL), BF16),
        compiler_params=_cparams(1),
        name="norm_mod",
    )(x2, gain, mod3, mod3)


def _mm_kernel(a_ref, w_ref, o_ref):
    o_ref[...] = jnp.dot(a_ref[...], w_ref[...], preferred_element_type=F32).astype(o_ref.dtype)


def _matmul(a, w, out_dtype, tm, tn, name):
    m, k = a.shape
    n = w.shape[1]
    return pl.pallas_call(
        _mm_kernel,
        grid=(m // tm, n // tn),
        in_specs=[pl.BlockSpec((tm, k), lambda i, j: (i, 0)),
                  pl.BlockSpec((k, tn), lambda i, j: (0, j))],
        out_specs=pl.BlockSpec((tm, tn), lambda i, j: (i, j)),
        out_shape=jax.ShapeDtypeStruct((m, n), out_dtype),
        compiler_params=_cparams(2),
        name=name,
    )(a, w)


def _rope_table_kernel(pos_ref, o_ref):
    lane = lax.broadcasted_iota(jnp.int32, (1, LANES), 1)
    pair = (lane % (QK_ROPE // 2)).astype(F32)
    inv_freq = jnp.exp(-math.log(ROPE_THETA) * (2.0 * pair) / QK_ROPE)
    ang = pos_ref[...].astype(F32) * inv_freq
    o_ref[...] = jnp.where(lane < QK_ROPE, jnp.cos(ang), jnp.sin(ang))


def _rope_table(pos_col):
    tm = 1024
    return pl.pallas_call(
        _rope_table_kernel,
        grid=(TOKENS // tm,),
        in_specs=[pl.BlockSpec((tm, 1), lambda i: (i, 0))],
        out_specs=pl.BlockSpec((tm, LANES), lambda i: (i, 0)),
        out_shape=jax.ShapeDtypeStruct((TOKENS, LANES), F32),
        compiler_params=_cparams(1),
        name="rope_table",
    )(pos_col)


def _rope_pair(t, cs):
    u = t * cs
    return u + pltpu.roll(u, QK_ROPE, axis=1)


def _q_proj_kernel(ql_ref, g_ref, w_ref, cs_ref, o_ref):
    hn = _rms(ql_ref[...], g_ref[...]).astype(BF16)
    cs = cs_ref[...]
    for h in range(MLA_HEADS):
        r = jnp.dot(hn, w_ref[h], preferred_element_type=F32)
        o_ref[h, :, :QK_NOPE] = r[:, :QK_NOPE].astype(o_ref.dtype)
        o_ref[h, :, QK_NOPE:] = _rope_pair(r[:, QK_NOPE:], cs)[:, :QK_ROPE].astype(o_ref.dtype)


def _q_proj(z, gain, w_heads, cs):
    tm = ATT_BLK
    per_b = SEQ // tm
    return pl.pallas_call(
        _q_proj_kernel,
        grid=(TOKENS // tm,),
        in_specs=[pl.BlockSpec((tm, Q_RANK), lambda i: (i, Z_Q // Q_RANK)),
                  pl.BlockSpec((1, Q_RANK), lambda i: (0, 0)),
                  pl.BlockSpec((MLA_HEADS, Q_RANK, 2 * LANES), lambda i: (0, 0, 0)),
                  pl.BlockSpec((tm, LANES), lambda i: (i, 0))],
        out_specs=pl.BlockSpec((None, MLA_HEADS, tm, QK_HEAD), lambda i: (i // per_b, 0, i % per_b, 0)),
        out_shape=jax.ShapeDtypeStruct((BATCH, MLA_HEADS, SEQ, QK_HEAD), BF16),
        compiler_params=_cparams(1),
        name="mla_q_proj",
    )(z, gain, w_heads, cs)


def _kv_proj_kernel(kva_ref, kvb_ref, kr_ref, g_ref, wk_ref, wvt_ref, cs_ref, k_ref, vt_ref):
    kvl = jnp.concatenate([kva_ref[...], kvb_ref[...]], axis=1)
    hn = _rms(kvl, g_ref[...]).astype(BF16)
    kr = _rope_pair(kr_ref[...], cs_ref[...])[:, :QK_ROPE].astype(k_ref.dtype)
    for h in range(MLA_HEADS):
        k_ref[h, :, :QK_NOPE] = jnp.dot(hn, wk_ref[h], preferred_element_type=F32).astype(k_ref.dtype)
        k_ref[h, :, QK_NOPE:] = kr
        vt_ref[h] = lax.dot_general(wvt_ref[h], hn, (((1,), (1,)), ((), ())),
                                    preferred_element_type=F32).astype(vt_ref.dtype)


def _kv_proj(z, gain, wk_heads, wvt_heads, cs):
    tm = ATT_BLK
    per_b = SEQ // tm
    half = KV_RANK // 2
    return pl.pallas_call(
        _kv_proj_kernel,
        grid=(TOKENS // tm,),
        in_specs=[pl.BlockSpec((tm, half), lambda i: (i, Z_KV // half)),
                  pl.BlockSpec((tm, half), lambda i: (i, Z_KV // half + 1)),
                  pl.BlockSpec((tm, LANES), lambda i: (i, Z_KR // LANES)),
                  pl.BlockSpec((1, KV_RANK), lambda i: (0, 0)),
                  pl.BlockSpec((MLA_HEADS, KV_RANK, QK_NOPE), lambda i: (0, 0, 0)),
                  pl.BlockSpec((MLA_HEADS, V_HEAD, KV_RANK), lambda i: (0, 0, 0)),
                  pl.BlockSpec((tm, LANES), lambda i: (i, 0))],
        out_specs=[pl.BlockSpec((None, MLA_HEADS, tm, QK_HEAD), lambda i: (i // per_b, 0, i % per_b, 0)),
                   pl.BlockSpec((None, MLA_HEADS, None, V_HEAD, tm), lambda i: (i // per_b, 0, i % per_b, 0, 0))],
        out_shape=[jax.ShapeDtypeStruct((BATCH, MLA_HEADS, SEQ, QK_HEAD), BF16),
                   jax.ShapeDtypeStruct((BATCH, MLA_HEADS, ATT_NBLK, V_HEAD, ATT_BLK), BF16)],
        compiler_params=_cparams(1),
        name="mla_kv_proj",
    )(z, z, z, gain, wk_heads, wvt_heads, cs)


ATT_HEADS = 4


def _attn_kernel(q_ref, k_ref, vt_ref, o_ref, acc_ref):
    log2_scale = (QK_HEAD ** -0.5) * math.log2(math.e)
    key_chunk = lax.broadcasted_iota(jnp.int32, (ATT_BLK, ATT_BLK), 0) // CHUNK
    qry_chunk = lax.broadcasted_iota(jnp.int32, (ATT_BLK, ATT_BLK), 1) // CHUNK
    diag_mask = key_chunk <= qry_chunk

    def q_body(qi, carry):
        q0 = pl.multiple_of(qi * ATT_BLK, ATT_BLK)

        def kv_block(j, stats, masked):
            k0 = pl.multiple_of(j * ATT_BLK, ATT_BLK)
            sts = [lax.dot_general(k_ref[h, pl.ds(k0, ATT_BLK), :], q_ref[h, pl.ds(q0, ATT_BLK), :],
                                   (((1,), (1,)), ((), ())), preferred_element_type=F32)
                   for h in range(ATT_HEADS)]
            new_stats, ps, alphas = [], [], []
            for h in range(ATT_HEADS):
                m, l = stats[h]
                st = sts[h] * log2_scale
                if masked:
                    st = jnp.where(diag_mask, st, -jnp.inf)
                m_new = jnp.maximum(m, jnp.max(st, axis=0, keepdims=True))
                alpha = jnp.exp2(m - m_new)
                p = jnp.exp2(st - m_new)
                new_stats.append((m_new, alpha * l + jnp.sum(p, axis=0, keepdims=True)))
                ps.append(p.astype(BF16))
                alphas.append(alpha)
            for h in range(ATT_HEADS):
                acc_ref[h] = alphas[h] * acc_ref[h] + jnp.dot(vt_ref[h, j], ps[h], preferred_element_type=F32)
            return tuple(new_stats)

        acc_ref[...] = jnp.zeros_like(acc_ref)
        stats = tuple((jnp.full((1, ATT_BLK), -jnp.inf, F32), jnp.zeros((1, ATT_BLK), F32))
                      for _ in range(ATT_HEADS))
        stats = lax.fori_loop(0, qi, lambda j, st: kv_block(j, st, False), stats)
        stats = kv_block(qi, stats, True)
        for h in range(ATT_HEADS):
            o_ref[pl.ds(q0, ATT_BLK), h * V_HEAD:(h + 1) * V_HEAD] = (acc_ref[h] / stats[h][1]).T
        return carry

    lax.fori_loop(0, ATT_NBLK, q_body, 0)


def _attention(q, k, vt):
    return pl.pallas_call(
        _attn_kernel,
        grid=(BATCH, MLA_HEADS // ATT_HEADS),
        in_specs=[pl.BlockSpec((None, ATT_HEADS, SEQ, QK_HEAD), lambda b, h: (b, h, 0, 0)),
                  pl.BlockSpec((None, ATT_HEADS, SEQ, QK_HEAD), lambda b, h: (b, h, 0, 0)),
                  pl.BlockSpec((None, ATT_HEADS, ATT_NBLK, V_HEAD, ATT_BLK), lambda b, h: (b, h, 0, 0, 0))],
        out_specs=pl.BlockSpec((SEQ, ATT_HEADS * V_HEAD), lambda b, h: (b, h)),
        out_shape=jax.ShapeDtypeStruct((TOKENS, MLA_WIDTH), F32),
        scratch_shapes=[pltpu.VMEM((ATT_HEADS, V_HEAD, ATT_BLK), F32)],
        compiler_params=_cparams(2),
        name="mla_attention",
    )(q, k, vt)


def _s5_disc_kernel(lre_ref, lim_ref, ldt_ref, bre_ref, bim_ref, are_ref, aim_ref, bbre_ref, bbim_ref):
    lam_re = lre_ref[...]
    lam_im = lim_ref[...]
    dt = jnp.exp(ldt_ref[...])
    mag = jnp.exp(lam_re * dt)
    abar_re = mag * jnp.cos(lam_im * dt)
    abar_im = mag * jnp.sin(lam_im * dt)
    nr = abar_re - 1.0
    ni = abar_im
    den = lam_re * lam_re + lam_im * lam_im
    f_re = (nr * lam_re + ni * lam_im) / den
    f_im = (ni * lam_re - nr * lam_im) / den
    are_ref[...] = abar_re
    aim_ref[...] = abar_im
    b_re = bre_ref[...]
    b_im = bim_ref[...]
    bbre_ref[...] = f_re * b_re - f_im * b_im
    bbim_ref[...] = f_re * b_im + f_im * b_re


def _s5_disc(lam_re, lam_im, log_dt, b_re, b_im):
    n = SSM_GROUPS * SSM_STATE
    tm = 1024
    col = pl.BlockSpec((tm, 1), lambda i: (i, 0))
    mat = pl.BlockSpec((tm, SSM_GROUP_CH), lambda i: (i, 0))
    return pl.pallas_call(
        _s5_disc_kernel,
        grid=(n // tm,),
        in_specs=[col, col, col, mat, mat],
        out_specs=[col, col, mat, mat],
        out_shape=[jax.ShapeDtypeStruct((n, 1), F32), jax.ShapeDtypeStruct((n, 1), F32),
                   jax.ShapeDtypeStruct((n, SSM_GROUP_CH), F32), jax.ShapeDtypeStruct((n, SSM_GROUP_CH), F32)],
        compiler_params=_cparams(1),
        name="s5_discretise",
    )(lam_re, lam_im, log_dt, b_re, b_im)


def _s5_scan_kernel(u_ref, wb_ref, a_ref, wc_ref, d_ref, y_ref, lhs_ref, s_ref, yb_ref, h_ref):
    @pl.when(pl.program_id(1) == 0)
    def _():
        h_ref[...] = jnp.zeros_like(h_ref)
        lhs_ref[...] = jnp.zeros_like(lhs_ref)

    for b in range(BATCH):
        lhs_ref[0, pl.ds(b, SSM_TCHUNK, stride=SUBLANES), :] = u_ref[b, :, :SSM_GB_CH]
        lhs_ref[1, pl.ds(b + BATCH, SSM_TCHUNK, stride=SUBLANES), :] = u_ref[b, :, SSM_GB_CH:]
    um = jnp.concatenate([lhs_ref[0], lhs_ref[1]], axis=1).astype(BF16)
    s_ref[...] = jnp.dot(um, wb_ref[...], preferred_element_type=F32)

    a_re = a_ref[:, :SSM_GB_ST]
    a_im = a_ref[:, SSM_GB_ST:]

    def step(t, carry):
        h_re, h_im = carry
        r0 = pl.multiple_of(t * SUBLANES, SUBLANES)
        n_re = a_re * h_re - a_im * h_im + s_ref[pl.ds(r0, SUBLANES), :SSM_GB_ST]
        n_im = a_re * h_im + a_im * h_re + s_ref[pl.ds(r0, SUBLANES), SSM_GB_ST:]
        s_ref[pl.ds(r0, SUBLANES), :SSM_GB_ST] = n_re
        s_ref[pl.ds(r0, SUBLANES), SSM_GB_ST:] = n_im
        return n_re, n_im

    h_re, h_im = lax.fori_loop(0, SSM_TCHUNK, step, (h_ref[:, :SSM_GB_ST], h_ref[:, SSM_GB_ST:]), unroll=8)
    h_ref[:, :SSM_GB_ST] = h_re
    h_ref[:, SSM_GB_ST:] = h_im

    y8 = jnp.dot(s_ref[...].astype(BF16), wc_ref[...], preferred_element_type=F32)
    yb_ref[0] = y8[:, :SSM_GB_CH]
    yb_ref[1] = y8[:, SSM_GB_CH:]
    d = d_ref[...]
    for b in range(BATCH):
        y_ref[b, :, :SSM_GB_CH] = (yb_ref[0, pl.ds(b, SSM_TCHUNK, stride=SUBLANES), :]
                                   + d[:, :SSM_GB_CH] * u_ref[b, :, :SSM_GB_CH])
        y_ref[b, :, SSM_GB_CH:] = (yb_ref[1, pl.ds(b + BATCH, SSM_TCHUNK, stride=SUBLANES), :]
                                   + d[:, SSM_GB_CH:] * u_ref[b, :, SSM_GB_CH:])


def _s5_scan(z3, wb, a8, wc, d_row):
    rows = SSM_TCHUNK * SUBLANES
    width = 2 * SSM_GB_CH
    states = 2 * SSM_GB_ST
    return pl.pallas_call(
        _s5_scan_kernel,
        grid=(SSM_PAIRS, SEQ // SSM_TCHUNK),
        in_specs=[pl.BlockSpec((BATCH, SSM_TCHUNK, width), lambda g, c: (0, c, Z_SSM // width + g)),
                  pl.BlockSpec((None, width, states), lambda g, c: (g, 0, 0)),
                  pl.BlockSpec((None, SUBLANES, states), lambda g, c: (g, 0, 0)),
                  pl.BlockSpec((None, states, width), lambda g, c: (g, 0, 0)),
                  pl.BlockSpec((1, width), lambda g, c: (0, g))],
        out_specs=pl.BlockSpec((BATCH, SSM_TCHUNK, width), lambda g, c: (0, c, g)),
        out_shape=jax.ShapeDtypeStruct((BATCH, SEQ, SSM_WIDTH), F32),
        scratch_shapes=[pltpu.VMEM((2, rows, SSM_GB_CH), F32), pltpu.VMEM((rows, states), F32),
                        pltpu.VMEM((2, rows, SSM_GB_CH), F32), pltpu.VMEM((SUBLANES, states), F32)],
        compiler_params=_cparams(2),
        name="s5_scan",
    )(z3, wb, a8, wc, d_row)


def _gelu_tanh(x):
    return 0.5 * x * (1.0 + jnp.tanh(math.sqrt(2.0 / math.pi) * (x + 0.044715 * (x * x * x))))


def _glu_kernel(y_ref, w_ref, b_ref, g_ref, o_ref):
    g = _gelu_tanh(y_ref[...])
    gate = jax.nn.sigmoid(jnp.dot(g.astype(BF16), w_ref[...], preferred_element_type=F32) + b_ref[...])
    o_ref[...] = _rms(g * gate, g_ref[...]).astype(o_ref.dtype)


def _glu(y, w, b, gain):
    tm = 256
    return pl.pallas_call(
        _glu_kernel,
        grid=(TOKENS // tm,),
        in_specs=[pl.BlockSpec((tm, SSM_WIDTH), lambda i: (i, 0)),
                  pl.BlockSpec((SSM_WIDTH, SSM_WIDTH), lambda i: (0, 0)),
                  pl.BlockSpec((1, SSM_WIDTH), lambda i: (0, 0)),
                  pl.BlockSpec((1, SSM_WIDTH), lambda i: (0, 0))],
        out_specs=pl.BlockSpec((tm, SSM_WIDTH), lambda i: (i, 0)),
        out_shape=jax.ShapeDtypeStruct((TOKENS, SSM_WIDTH), BF16),
        compiler_params=_cparams(1),
        name="s5_glu",
    )(y, w, b, gain)


def _out_proj_kernel(om_ref, os_ref, g_ref, w_ref, x_ref, ga_ref, o_ref, lhs_ref):
    @pl.when(pl.program_id(1) == 0)
    def _():
        lhs_ref[:, :MLA_WIDTH] = _rms(om_ref[...], g_ref[...]).astype(BF16)
        lhs_ref[:, MLA_WIDTH:] = os_ref[...]

    o_ref[...] = x_ref[...] + ga_ref[...] * jnp.dot(lhs_ref[...], w_ref[...], preferred_element_type=F32)


def _out_proj(o_mla, o_ssm, gain, w, x2, mod3, gate_chunk):
    tm, tn = 512, 1024
    per_b = SEQ // tm
    nj = D_MODEL // tn
    return pl.pallas_call(
        _out_proj_kernel,
        grid=(TOKENS // tm, nj),
        in_specs=[pl.BlockSpec((tm, MLA_WIDTH), lambda i, j: (i, 0)),
                  pl.BlockSpec((tm, SSM_WIDTH), lambda i, j: (i, 0)),
                  pl.BlockSpec((1, MLA_WIDTH), lambda i, j: (0, 0)),
                  pl.BlockSpec((D_MODEL, tn), lambda i, j: (0, j)),
                  pl.BlockSpec((tm, tn), lambda i, j: (i, j)),
                  pl.BlockSpec((None, 1, tn), lambda i, j: (i // per_b, 0, gate_chunk * nj + j))],
        out_specs=pl.BlockSpec((tm, tn), lambda i, j: (i, j)),
        out_shape=jax.ShapeDtypeStruct((TOKENS, D_MODEL), F32),
        scratch_shapes=[pltpu.VMEM((tm, D_MODEL), BF16)],
        compiler_params=_cparams(2),
        name="out_proj",
    )(o_mla, o_ssm, gain, w, x2, mod3)


def _router_kernel(x_ref, g_ref, sh_ref, sc_ref, whi_ref, wlo_ref, br_ref, h_ref, eid_ref, wt_ref):
    h = _rms(x_ref[...], g_ref[...]) * (1.0 + sc_ref[...]) + sh_ref[...]
    h_ref[...] = h
    h_hi = h.astype(BF16)
    h_lo = (h - h_hi.astype(F32)).astype(BF16)

    whi = whi_ref[...]
    logits = (jnp.dot(h_hi, whi, preferred_element_type=F32)
              + jnp.dot(h_lo, whi, preferred_element_type=F32)
              + jnp.dot(h_hi, wlo_ref[...], preferred_element_type=F32)
              + br_ref[...])

    col = lax.broadcasted_iota(jnp.int32, logits.shape, 1)
    neg = -jnp.inf
    is_g = col < N_EGROUPS
    gl = jnp.where(is_g, logits, neg)
    gmax = jnp.max(gl, axis=1, keepdims=True)
    grp = jnp.min(jnp.where(gl == gmax, col, LANES), axis=1, keepdims=True)
    p_grp = 1.0 / jnp.sum(jnp.where(is_g, jnp.exp(logits - gmax), 0.0), axis=1, keepdims=True)

    ecol = col - N_EGROUPS
    in_grp = (ecol >= 0) & (ecol < N_EXPERTS) & ((ecol // EXPERTS_PER_GROUP) == grp)
    el = jnp.where(in_grp, logits, neg)
    v1 = jnp.max(el, axis=1, keepdims=True)
    i1 = jnp.min(jnp.where(el == v1, col, LANES), axis=1, keepdims=True)
    el2 = jnp.where(col == i1, neg, el)
    v2 = jnp.max(el2, axis=1, keepdims=True)
    i2 = jnp.min(jnp.where(el2 == v2, col, LANES), axis=1, keepdims=True)

    s2 = jnp.exp(v2 - v1)
    w1 = p_grp / (1.0 + s2)
    w2 = p_grp * s2 / (1.0 + s2)
    eid_ref[...] = jnp.where(col == 0, i1 - N_EGROUPS, jnp.where(col == 1, i2 - N_EGROUPS, 0))
    wt_ref[...] = jnp.where(col == 0, w1, jnp.where(col == 1, w2, 0.0))


def _router(x1, gain, mod3, shift_chunk, scale_chunk, w_hi, w_lo, b_row):
    tm = 256
    per_b = SEQ // tm
    return pl.pallas_call(
        _router_kernel,
        grid=(TOKENS // tm,),
        in_specs=[pl.BlockSpec((tm, D_MODEL), lambda i: (i, 0)),
                  pl.BlockSpec((1, D_MODEL), lambda i: (0, 0)),
                  pl.BlockSpec((None, 1, D_MODEL), lambda i: (i // per_b, 0, shift_chunk)),
                  pl.BlockSpec((None, 1, D_MODEL), lambda i: (i // per_b, 0, scale_chunk)),
                  pl.BlockSpec((D_MODEL, LANES), lambda i: (0, 0)),
                  pl.BlockSpec((D_MODEL, LANES), lambda i: (0, 0)),
                  pl.BlockSpec((1, LANES), lambda i: (0, 0))],
        out_specs=[pl.BlockSpec((tm, D_MODEL), lambda i: (i, 0)),
                   pl.BlockSpec((tm, LANES), lambda i: (i, 0)),
                   pl.BlockSpec((tm, LANES), lambda i: (i, 0))],
        out_shape=[jax.ShapeDtypeStruct((TOKENS, D_MODEL), F32),
                   jax.ShapeDtypeStruct((TOKENS, LANES), jnp.int32),
                   jax.ShapeDtypeStruct((TOKENS, LANES), F32)],
        compiler_params=_cparams(1),
        name="ffn_router",
    )(x1, gain, mod3, mod3, w_hi, w_lo, b_row)


def _moe_plan(eid):
    n_assign = TOKENS * TOP_K
    e_flat = eid.reshape(n_assign)
    onehot = (e_flat[:, None] == jnp.arange(N_EXPERTS, dtype=jnp.int32)[None, :]).astype(jnp.int32)
    csum = jnp.cumsum(onehot, axis=0)
    counts = csum[-1]
    rank = jnp.sum((csum - onehot) * onehot, axis=1)
    nblk_e = (counts + MOE_BLK - 1) // MOE_BLK
    bend = jnp.cumsum(nblk_e)
    bstart = bend - nblk_e
    dest = jnp.sum(onehot * bstart[None, :], axis=1) * MOE_BLK + rank
    tok = jnp.arange(n_assign, dtype=jnp.int32) // TOP_K
    row_tok = jnp.zeros((MOE_ROWS,), jnp.int32).at[dest].set(tok, unique_indices=True)
    blk = jnp.arange(MOE_NBLK, dtype=jnp.int32)
    n_used = bend[-1]
    blk_e = jnp.minimum(jnp.sum((bend[None, :] <= blk[:, None]).astype(jnp.int32), axis=1), N_EXPERTS - 1)
    blk_nvalid = jnp.clip(counts[blk_e] - (blk - bstart[blk_e]) * MOE_BLK, 0, MOE_BLK)
    blk_nvalid = jnp.where(blk < n_used, blk_nvalid, 0)
    blk_ngroups = ((blk_nvalid + DMA_UNROLL - 1) // DMA_UNROLL).astype(jnp.int32)
    last_e = blk_e[jnp.maximum(n_used - 1, 0)]
    blk_e = jnp.where(blk < n_used, blk_e, last_e).astype(jnp.int32)
    return row_tok, blk_e, blk_ngroups, n_used.reshape(1).astype(jnp.int32), dest.reshape(TOKENS, TOP_K)


def _expert_up_kernel(tok_ref, be_ref, ng_ref, nu_ref, h_hbm, w1_ref, w3_ref, o_ref, xbuf, sem):
    i = pl.program_id(0)
    slot = i % 2

    def row_copy(src_row, dst_slot, dst_row):
        return pltpu.make_async_copy(h_hbm.at[pl.ds(src_row, 1)], xbuf.at[dst_slot, pl.ds(dst_row, 1)],
                                     sem.at[dst_slot])

    def issue(block, dst_slot):
        base = block * MOE_BLK

        def body(g, c):
            for u in range(DMA_UNROLL):
                r = g * DMA_UNROLL + u
                row_copy(tok_ref[base + r], dst_slot, r).start()
            return c

        lax.fori_loop(0, ng_ref[block], body, 0)

    @pl.when(i == 0)
    def _():
        xbuf[...] = jnp.zeros_like(xbuf)
        issue(0, 0)

    @pl.when(i + 1 < pl.num_programs(0))
    def _():
        issue(i + 1, 1 - slot)

    def wait_body(g, c):
        for u in range(DMA_UNROLL):
            row_copy(0, slot, g * DMA_UNROLL + u).wait()
        return c

    lax.fori_loop(0, ng_ref[i], wait_body, 0)

    @pl.when(i < nu_ref[0])
    def _():
        x = xbuf[slot].astype(BF16)
        a = jnp.dot(x, w1_ref[...].astype(BF16), preferred_element_type=F32)
        b = jnp.dot(x, w3_ref[...].astype(BF16), preferred_element_type=F32)
        o_ref[...] = (a * jax.nn.sigmoid(a) * b).astype(o_ref.dtype)

    @pl.when(i >= nu_ref[0])
    def _():
        o_ref[...] = jnp.zeros_like(o_ref)


def _expert_up(row_tok, blk_e, blk_ngroups, n_used, h2, w1, w3):
    wspec = pl.BlockSpec((None, D_MODEL, D_EXPERT), lambda i, tok, be, ng, nu: (be[i], 0, 0))
    return pl.pallas_call(
        _expert_up_kernel,
        grid_spec=pltpu.PrefetchScalarGridSpec(
            num_scalar_prefetch=4,
            grid=(MOE_NBLK,),
            in_specs=[pl.BlockSpec(memory_space=pl.ANY), wspec, wspec],
            out_specs=pl.BlockSpec((MOE_BLK, D_EXPERT), lambda i, tok, be, ng, nu: (i, 0)),
            scratch_shapes=[pltpu.VMEM((2, MOE_BLK, D_MODEL), F32),
                            pltpu.SemaphoreType.DMA((2,))]),
        out_shape=jax.ShapeDtypeStruct((MOE_ROWS, D_EXPERT), BF16),
        compiler_params=_cparams(1, 60 * 1024 * 1024),
        name="moe_expert_up",
    )(row_tok, blk_e, blk_ngroups, n_used, h2, w1, w3)


def _expert_down_kernel(be_ref, nu_ref, h_ref, w_ref, o_ref):
    @pl.when(pl.program_id(0) < nu_ref[0])
    def _():
        o_ref[...] = jnp.dot(h_ref[...], w_ref[...].astype(BF16), preferred_element_type=F32)

    @pl.when(pl.program_id(0) >= nu_ref[0])
    def _():
        o_ref[...] = jnp.zeros_like(o_ref)


def _expert_down(blk_e, n_used, hmid, w2):
    return pl.pallas_call(
        _expert_down_kernel,
        grid_spec=pltpu.PrefetchScalarGridSpec(
            num_scalar_prefetch=2,
            grid=(MOE_NBLK,),
            in_specs=[pl.BlockSpec((MOE_BLK, D_EXPERT), lambda i, be, nu: (i, 0)),
                      pl.BlockSpec((None, D_EXPERT, D_MODEL), lambda i, be, nu: (be[i], 0, 0))],
            out_specs=pl.BlockSpec((MOE_BLK, D_MODEL), lambda i, be, nu: (i, 0))),
        out_shape=jax.ShapeDtypeStruct((MOE_ROWS, D_MODEL), F32),
        compiler_params=_cparams(1),
        name="moe_expert_down",
    )(blk_e, n_used, hmid, w2)


def _combine_kernel(pos_ref, ys_hbm, x_ref, wt_ref, gf_ref, fg_ref, o_ref, ybuf, sem):
    i = pl.program_id(0)
    slot = i % 2

    def row_copy(src_row, dst_slot, k, dst_row):
        return pltpu.make_async_copy(ys_hbm.at[pl.ds(src_row, 1)], ybuf.at[dst_slot, k, pl.ds(dst_row, 1)],
                                     sem.at[dst_slot])

    def issue(block, dst_slot):
        base = block * (CMB_TM * TOP_K)

        def body(r, c):
            for k in range(TOP_K):
                row_copy(pos_ref[base + r * TOP_K + k], dst_slot, k, r).start()
            return c

        lax.fori_loop(0, CMB_TM, body, 0, unroll=DMA_UNROLL // TOP_K)

    @pl.when(i == 0)
    def _():
        issue(0, 0)

    @pl.when(i + 1 < pl.num_programs(0))
    def _():
        issue(i + 1, 1 - slot)

    def wait_body(r, c):
        for k in range(TOP_K):
            row_copy(0, slot, k, r).wait()
        return c

    lax.fori_loop(0, CMB_TM, wait_body, 0, unroll=DMA_UNROLL // TOP_K)

    wt = wt_ref[...]
    moe = wt[:, 0:1] * ybuf[slot, 0] + wt[:, 1:2] * ybuf[slot, 1]
    o_ref[...] = _rms(x_ref[...] + gf_ref[...] * moe, fg_ref[...])


def _combine(pos_flat, ys, x1, wt, mod3, gate_chunk, final_gain):
    per_b = SEQ // CMB_TM
    return pl.pallas_call(
        _combine_kernel,
        grid_spec=pltpu.PrefetchScalarGridSpec(
            num_scalar_prefetch=1,
            grid=(TOKENS // CMB_TM,),
            in_specs=[pl.BlockSpec(memory_space=pl.ANY),
                      pl.BlockSpec((CMB_TM, D_MODEL), lambda i, pos: (i, 0)),
                      pl.BlockSpec((CMB_TM, LANES), lambda i, pos: (i, 0)),
                      pl.BlockSpec((None, 1, D_MODEL), lambda i, pos: (i // per_b, 0, gate_chunk)),
                      pl.BlockSpec((1, D_MODEL), lambda i, pos: (0, 0))],
            out_specs=pl.BlockSpec((CMB_TM, D_MODEL), lambda i, pos: (i, 0)),
            scratch_shapes=[pltpu.VMEM((2, TOP_K, CMB_TM, D_MODEL), F32),
                            pltpu.SemaphoreType.DMA((2,))]),
        out_shape=jax.ShapeDtypeStruct((TOKENS, D_MODEL), F32),
        compiler_params=_cparams(1),
        name="moe_combine_final_norm",
    )(pos_flat, ys, x1, wt, mod3, final_gain)


def _rotate_half_cols(w):
    half = w.shape[-1] // 2
    return jnp.concatenate([-w[..., half:], w[..., :half]], axis=-1)


def _block_diag_in(bbar):
    bb = bbar.reshape(SSM_GROUPS // SSM_GB, SSM_GB, SSM_STATE, SSM_GROUP_CH)
    w = jnp.einsum('bgpn,gh->bgnhp', bb, jnp.eye(SSM_GB, dtype=bbar.dtype))
    return w.reshape(SSM_GROUPS // SSM_GB, SSM_GB_CH, SSM_GB_ST)


def _block_diag_out(cmat):
    cc = cmat.reshape(SSM_GROUPS // SSM_GB, SSM_GB, SSM_GROUP_CH, SSM_STATE)
    w = jnp.einsum('bgnp,gh->bgphn', cc, jnp.eye(SSM_GB, dtype=cmat.dtype))
    return w.reshape(SSM_GROUPS // SSM_GB, SSM_GB_ST, SSM_GB_CH)


def kernel(x, c, positions, w_ada, b_ada, norm_mix_gain, w_in, q_lat_gain, w_uq, kv_lat_gain, w_ukv,
           ssm_lam_re, ssm_lam_im, ssm_log_dt, ssm_b_re, ssm_b_im, ssm_c_re, ssm_c_im, ssm_d,
           w_glu, b_glu, mla_out_gain, ssm_out_gain, w_out, norm_ffn_gain,
           w_group_router, b_group_router, w_expert_router, b_expert_router,
           w1_experts, w3_experts, w2_experts, final_gain):
    x2 = x.reshape(TOKENS, D_MODEL)

    c8 = jnp.zeros((SUBLANES, D_MODEL), F32).at[:BATCH].set(c)
    mod = _ada(c8, w_ada[0], b_ada[0].reshape(1, 6 * D_MODEL))
    mod3 = mod.reshape(SUBLANES, 1, 6 * D_MODEL)

    h = _norm_mod(x2, norm_mix_gain[0].reshape(1, D_MODEL), mod3, 0, 1)
    w_in0 = w_in[0]
    s_kv = Q_RANK
    s_kr = Q_RANK + KV_RANK
    s_ssm = s_kr + QK_ROPE
    w_kr = w_in0[:, s_kr:s_ssm]
    w_z = jnp.concatenate([w_in0[:, :s_kv], w_in0[:, s_ssm:], w_in0[:, s_kv:s_kr], w_kr, _rotate_half_cols(w_kr)],
                          axis=1).astype(BF16)
    z = _matmul(h, w_z, F32, 512, Z_COLS // 3, "in_proj")

    cs = _rope_table(positions.reshape(TOKENS, 1))

    wq = w_uq[0].reshape(Q_RANK, MLA_HEADS, QK_HEAD)
    wq_heads = jnp.concatenate([wq, _rotate_half_cols(wq[:, :, QK_NOPE:])], axis=-1)
    wq_heads = wq_heads.transpose(1, 0, 2).astype(BF16)
    q = _q_proj(z, q_lat_gain[0].reshape(1, Q_RANK), wq_heads, cs)
    wkv = w_ukv[0].reshape(KV_RANK, MLA_HEADS, QK_NOPE + V_HEAD)
    wk_heads = wkv[:, :, :QK_NOPE].transpose(1, 0, 2).astype(BF16)
    wvt_heads = wkv[:, :, QK_NOPE:].transpose(1, 2, 0).astype(BF16)
    k, vt = _kv_proj(z, kv_lat_gain[0].reshape(1, KV_RANK), wk_heads, wvt_heads, cs)
    o_mla = _attention(q, k, vt)

    n_st = SSM_GROUPS * SSM_STATE
    abar_re, abar_im, bbar_re, bbar_im = _s5_disc(
        ssm_lam_re[0].reshape(n_st, 1), ssm_lam_im[0].reshape(n_st, 1),
        jnp.repeat(ssm_log_dt[0], SSM_STATE).reshape(n_st, 1),
        ssm_b_re[0].reshape(n_st, SSM_GROUP_CH), ssm_b_im[0].reshape(n_st, SSM_GROUP_CH))
    n_gb = SSM_GROUPS // SSM_GB
    wb = jnp.concatenate([_block_diag_in(bbar_re), _block_diag_in(bbar_im)], axis=-1)
    wb = wb.reshape(SSM_PAIRS, 2 * SSM_GB_CH, 2 * SSM_GB_ST).astype(BF16)
    wc = jnp.concatenate([_block_diag_out(ssm_c_re[0]), -_block_diag_out(ssm_c_im[0])], axis=1)
    wc = wc.reshape(SSM_PAIRS, 2, 2 * SSM_GB_ST, SSM_GB_CH).transpose(0, 2, 1, 3)
    wc = wc.reshape(SSM_PAIRS, 2 * SSM_GB_ST, 2 * SSM_GB_CH).astype(BF16)
    a16 = jnp.concatenate([abar_re.reshape(n_gb, SSM_GB_ST), abar_im.reshape(n_gb, SSM_GB_ST)], axis=-1)
    a8 = jnp.repeat(a16.reshape(SSM_PAIRS, 2, 2 * SSM_GB_ST), BATCH, axis=1)

    y = _s5_scan(z.reshape(BATCH, SEQ, Z_COLS), wb, a8, wc, ssm_d[0].reshape(1, SSM_WIDTH))
    o_ssm = _glu(y.reshape(TOKENS, SSM_WIDTH), w_glu[0].astype(BF16), b_glu[0].reshape(1, SSM_WIDTH),
                 ssm_out_gain[0].reshape(1, SSM_WIDTH))

    x1 = _out_proj(o_mla, o_ssm, mla_out_gain[0].reshape(1, MLA_WIDTH), w_out[0].astype(BF16), x2, mod3, 2)

    w_r = jnp.concatenate([w_group_router[0], w_expert_router[0],
                           jnp.zeros((D_MODEL, LANES - N_EGROUPS - N_EXPERTS), F32)], axis=1)
    w_r_hi = w_r.astype(BF16)
    w_r_lo = (w_r - w_r_hi.astype(F32)).astype(BF16)
    b_r = jnp.concatenate([b_group_router[0], b_expert_router[0],
                           jnp.zeros((LANES - N_EGROUPS - N_EXPERTS,), F32)]).reshape(1, LANES)
    h2, eid, wt = _router(x1, norm_ffn_gain[0].reshape(1, D_MODEL), mod3, 3, 4, w_r_hi, w_r_lo, b_r)

    row_tok, blk_e, blk_ngroups, n_used, dest = _moe_plan(eid[:, :TOP_K])
    hmid = _expert_up(row_tok, blk_e, blk_ngroups, n_used, h2, w1_experts[0], w3_experts[0])
    ys = _expert_down(blk_e, n_used, hmid, w2_experts[0])
    out = _combine(dest.reshape(TOKENS * TOP_K), ys, x1, wt, mod3, 5, final_gain.reshape(1, D_MODEL))
    return out.reshape(BATCH, SEQ, D_MODEL)
```

```python
import math

import jax
import jax.numpy as jnp
from jax import lax
from jax.experimental import pallas as pl
from jax.experimental.pallas import tpu as pltpu

F32 = jnp.float32
BF16 = jnp.bfloat16

D_MODEL = 4096
BATCH = 4
SEQ = 2048
TOKENS = BATCH * SEQ
CHUNK = 64
EPS = 1e-6

MLA_HEADS = 16
QK_NOPE = 128
QK_ROPE = 64
QK_HEAD = QK_NOPE + QK_ROPE
V_HEAD = 128
Q_RANK = 768
KV_RANK = 512
ROPE_THETA = 10000.0
MLA_WIDTH = MLA_HEADS * V_HEAD

SSM_WIDTH = D_MODEL - MLA_WIDTH
SSM_GROUP_CH = 16
SSM_GROUPS = SSM_WIDTH // SSM_GROUP_CH
SSM_STATE = 64

N_EGROUPS = 8
EXPERTS_PER_GROUP = 8
N_EXPERTS = N_EGROUPS * EXPERTS_PER_GROUP
TOP_K = 2
D_EXPERT = 512

LANES = 128
SUBLANES = 8
VMEM_LIMIT = 56 * 1024 * 1024

Z_Q = 0
Z_SSM = Q_RANK
Z_KV = Z_SSM + SSM_WIDTH
Z_KR = Z_KV + KV_RANK
Z_COLS = Z_KR + 2 * QK_ROPE

ATT_BLK = 256
ATT_NBLK = SEQ // ATT_BLK

SSM_GB = 8
SSM_GB_CH = SSM_GB * SSM_GROUP_CH
SSM_GB_ST = SSM_GB * SSM_STATE
SSM_PAIRS = SSM_GROUPS // (2 * SSM_GB)
SSM_TCHUNK = 256

MOE_BLK = 256
MOE_NBLK = (TOKENS * TOP_K + N_EXPERTS * (MOE_BLK - 1)) // MOE_BLK + 1
MOE_ROWS = MOE_NBLK * MOE_BLK
DMA_UNROLL = 8
CMB_TM = 128


def _cparams(n_axes, vmem=VMEM_LIMIT):
    return pltpu.CompilerParams(dimension_semantics=("arbitrary",) * n_axes, vmem_limit_bytes=vmem)


def _ada_kernel(c_ref, w_ref, b_ref, o_ref):
    c = c_ref[...]
    ca = (c * jax.nn.sigmoid(c)).astype(BF16)
    o_ref[...] = jnp.dot(ca, w_ref[...].astype(BF16), preferred_element_type=F32) + b_ref[...]


def _ada(c8, w, b):
    n = w.shape[1]
    tn = 512
    return pl.pallas_call(
        _ada_kernel,
        grid=(n // tn,),
        in_specs=[pl.BlockSpec((SUBLANES, D_MODEL), lambda j: (0, 0)),
                  pl.BlockSpec((D_MODEL, tn), lambda j: (0, j)),
                  pl.BlockSpec((1, tn), lambda j: (0, j))],
        out_specs=pl.BlockSpec((SUBLANES, tn), lambda j: (0, j)),
        out_shape=jax.ShapeDtypeStruct((SUBLANES, n), F32),
        compiler_params=_cparams(1),
        name="ada_mod",
    )(c8, w, b)


def _rms(x, gain):
    return x * lax.rsqrt(jnp.mean(x * x, axis=-1, keepdims=True) + EPS) * gain


def _norm_mod_kernel(x_ref, g_ref, sh_ref, sc_ref, o_ref):
    y = _rms(x_ref[...], g_ref[...])
    o_ref[...] = (y * (1.0 + sc_ref[...]) + sh_ref[...]).astype(o_ref.dtype)


def _norm_mod(x2, gain, mod3, shift_chunk, scale_chunk):
    tm = 256
    per_b = SEQ // tm
    return pl.pallas_call(
        _norm_mod_kernel,
        grid=(TOKENS // tm,),
        in_specs=[pl.BlockSpec((tm, D_MODEL), lambda i: (i, 0)),
                  pl.BlockSpec((1, D_MODEL), lambda i: (0, 0)),
                  pl.BlockSpec((None, 1, D_MODEL), lambda i: (i // per_b, 0, shift_chunk)),
                  pl.BlockSpec((None, 1, D_MODEL), lambda i: (i // per_b, 0, scale_chunk))],
        out_specs=pl.BlockSpec((tm, D_MODEL), lambda i: (i, 0)),
        out_shape=jax.ShapeDtypeStruct((TOKENS, D_MODEL), BF16),
        compiler_params=_cparams(1),
        name="norm_mod",
    )(x2, gain, mod3, mod3)


def _mm_kernel(a_ref, w_ref, o_ref):
    o_ref[...] = jnp.dot(a_ref[...], w_ref[...], preferred_element_type=F32).astype(o_ref.dtype)


def _matmul(a, w, out_dtype, tm, tn, name):
    m, k = a.shape
    n = w.shape[1]
    return pl.pallas_call(
        _mm_kernel,
        grid=(m // tm, n // tn),
        in_specs=[pl.BlockSpec((tm, k), lambda i, j: (i, 0)),
                  pl.BlockSpec((k, tn), lambda i, j: (0, j))],
        out_specs=pl.BlockSpec((tm, tn), lambda i, j: (i, j)),
        out_shape=jax.ShapeDtypeStruct((m, n), out_dtype),
        compiler_params=_cparams(2),
        name=name,
    )(a, w)


def _rope_table_kernel(pos_ref, o_ref):
    lane = lax.broadcasted_iota(jnp.int32, (1, LANES), 1)
    pair = (lane % (QK_ROPE // 2)).astype(F32)
    inv_freq = jnp.exp(-math.log(ROPE_THETA) * (2.0 * pair) / QK_ROPE)
    ang = pos_ref[...].astype(F32) * inv_freq
    o_ref[...] = jnp.where(lane < QK_ROPE, jnp.cos(ang), jnp.sin(ang))


def _rope_table(pos_col):
    tm = 1024
    return pl.pallas_call(
        _rope_table_kernel,
        grid=(TOKENS // tm,),
        in_specs=[pl.BlockSpec((tm, 1), lambda i: (i, 0))],
        out_specs=pl.BlockSpec((tm, LANES), lambda i: (i, 0)),
        out_shape=jax.ShapeDtypeStruct((TOKENS, LANES), F32),
        compiler_params=_cparams(1),
        name="rope_table",
    )(pos_col)


def _rope_pair(t, cs):
    u = t * cs
    return u + pltpu.roll(u, QK_ROPE, axis=1)


def _q_proj_kernel(ql_ref, g_ref, w_ref, cs_ref, o_ref):
    hn = _rms(ql_ref[...], g_ref[...]).astype(BF16)
    cs = cs_ref[...]
    for h in range(MLA_HEADS):
        r = jnp.dot(hn, w_ref[h], preferred_element_type=F32)
        o_ref[h, :, :QK_NOPE] = r[:, :QK_NOPE].astype(o_ref.dtype)
        o_ref[h, :, QK_NOPE:] = _rope_pair(r[:, QK_NOPE:], cs)[:, :QK_ROPE].astype(o_ref.dtype)


def _q_proj(z, gain, w_heads, cs):
    tm = ATT_BLK
    per_b = SEQ // tm
    return pl.pallas_call(
        _q_proj_kernel,
        grid=(TOKENS // tm,),
        in_specs=[pl.BlockSpec((tm, Q_RANK), lambda i: (i, Z_Q // Q_RANK)),
                  pl.BlockSpec((1, Q_RANK), lambda i: (0, 0)),
                  pl.BlockSpec((MLA_HEADS, Q_RANK, 2 * LANES), lambda i: (0, 0, 0)),
                  pl.BlockSpec((tm, LANES), lambda i: (i, 0))],
        out_specs=pl.BlockSpec((None, MLA_HEADS, tm, QK_HEAD), lambda i: (i // per_b, 0, i % per_b, 0)),
        out_shape=jax.ShapeDtypeStruct((BATCH, MLA_HEADS, SEQ, QK_HEAD), BF16),
        compiler_params=_cparams(1),
        name="mla_q_proj",
    )(z, gain, w_heads, cs)


def _kv_proj_kernel(kva_ref, kvb_ref, kr_ref, g_ref, wk_ref, wvt_ref, cs_ref, k_ref, vt_ref):
    kvl = jnp.concatenate([kva_ref[...], kvb_ref[...]], axis=1)
    hn = _rms(kvl, g_ref[...]).astype(BF16)
    kr = _rope_pair(kr_ref[...], cs_ref[...])[:, :QK_ROPE].astype(k_ref.dtype)
    for h in range(MLA_HEADS):
        k_ref[h, :, :QK_NOPE] = jnp.dot(hn, wk_ref[h], preferred_element_type=F32).astype(k_ref.dtype)
        k_ref[h, :, QK_NOPE:] = kr
        vt_ref[h] = lax.dot_general(wvt_ref[h], hn, (((1,), (1,)), ((), ())),
                                    preferred_element_type=F32).astype(vt_ref.dtype)


def _kv_proj(z, gain, wk_heads, wvt_heads, cs):
    tm = ATT_BLK
    per_b = SEQ // tm
    half = KV_RANK // 2
    return pl.pallas_call(
        _kv_proj_kernel,
        grid=(TOKENS // tm,),
        in_specs=[pl.BlockSpec((tm, half), lambda i: (i, Z_KV // half)),
                  pl.BlockSpec((tm, half), lambda i: (i, Z_KV // half + 1)),
                  pl.BlockSpec((tm, LANES), lambda i: (i, Z_KR // LANES)),
                  pl.BlockSpec((1, KV_RANK), lambda i: (0, 0)),
                  pl.BlockSpec((MLA_HEADS, KV_RANK, QK_NOPE), lambda i: (0, 0, 0)),
                  pl.BlockSpec((MLA_HEADS, V_HEAD, KV_RANK), lambda i: (0, 0, 0)),
                  pl.BlockSpec((tm, LANES), lambda i: (i, 0))],
        out_specs=[pl.BlockSpec((None, MLA_HEADS, tm, QK_HEAD), lambda i: (i // per_b, 0, i % per_b, 0)),
                   pl.BlockSpec((None, MLA_HEADS, None, V_HEAD, tm), lambda i: (i // per_b, 0, i % per_b, 0, 0))],
        out_shape=[jax.ShapeDtypeStruct((BATCH, MLA_HEADS, SEQ, QK_HEAD), BF16),
                   jax.ShapeDtypeStruct((BATCH, MLA_HEADS, ATT_NBLK, V_HEAD, ATT_BLK), BF16)],
        compiler_params=_cparams(1),
        name="mla_kv_proj",
    )(z, z, z, gain, wk_heads, wvt_heads, cs)


ATT_HEADS = 4


def _attn_kernel(q_ref, k_ref, vt_ref, o_ref, acc_ref):
    log2_scale = (QK_HEAD ** -0.5) * math.log2(math.e)
    key_chunk = lax.broadcasted_iota(jnp.int32, (ATT_BLK, ATT_BLK), 0) // CHUNK
    qry_chunk = lax.broadcasted_iota(jnp.int32, (ATT_BLK, ATT_BLK), 1) // CHUNK
    diag_mask = key_chunk <= qry_chunk

    def q_body(qi, carry):
        q0 = pl.multiple_of(qi * ATT_BLK, ATT_BLK)

        def kv_block(j, stats, masked):
            k0 = pl.multiple_of(j * ATT_BLK, ATT_BLK)
            sts = [lax.dot_general(k_ref[h, pl.ds(k0, ATT_BLK), :], q_ref[h, pl.ds(q0, ATT_BLK), :],
                                   (((1,), (1,)), ((), ())), preferred_element_type=F32)
                   for h in range(ATT_HEADS)]
            new_stats, ps, alphas = [], [], []
            for h in range(ATT_HEADS):
                m, l = stats[h]
                st = sts[h] * log2_scale
                if masked:
                    st = jnp.where(diag_mask, st, -jnp.inf)
                m_new = jnp.maximum(m, jnp.max(st, axis=0, keepdims=True))
                alpha = jnp.exp2(m - m_new)
                p = jnp.exp2(st - m_new)
                new_stats.append((m_new, alpha * l + jnp.sum(p, axis=0, keepdims=True)))
                ps.append(p.astype(BF16))
                alphas.append(alpha)
            for h in range(ATT_HEADS):
                acc_ref[h] = alphas[h] * acc_ref[h] + jnp.dot(vt_ref[h, j], ps[h], preferred_element_type=F32)
            return tuple(new_stats)

        acc_ref[...] = jnp.zeros_like(acc_ref)
        stats = tuple((jnp.full((1, ATT_BLK), -jnp.inf, F32), jnp.zeros((1, ATT_BLK), F32))
                      for _ in range(ATT_HEADS))
        stats = lax.fori_loop(0, qi, lambda j, st: kv_block(j, st, False), stats)
        stats = kv_block(qi, stats, True)
        for h in range(ATT_HEADS):
            o_ref[pl.ds(q0, ATT_BLK), h * V_HEAD:(h + 1) * V_HEAD] = (acc_ref[h] / stats[h][1]).T
        return carry

    lax.fori_loop(0, ATT_NBLK, q_body, 0)


def _attention(q, k, vt):
    return pl.pallas_call(
        _attn_kernel,
        grid=(BATCH, MLA_HEADS // ATT_HEADS),
        in_specs=[pl.BlockSpec((None, ATT_HEADS, SEQ, QK_HEAD), lambda b, h: (b, h, 0, 0)),
                  pl.BlockSpec((None, ATT_HEADS, SEQ, QK_HEAD), lambda b, h: (b, h, 0, 0)),
                  pl.BlockSpec((None, ATT_HEADS, ATT_NBLK, V_HEAD, ATT_BLK), lambda b, h: (b, h, 0, 0, 0))],
        out_specs=pl.BlockSpec((SEQ, ATT_HEADS * V_HEAD), lambda b, h: (b, h)),
        out_shape=jax.ShapeDtypeStruct((TOKENS, MLA_WIDTH), F32),
        scratch_shapes=[pltpu.VMEM((ATT_HEADS, V_HEAD, ATT_BLK), F32)],
        compiler_params=_cparams(2),
        name="mla_attention",
    )(q, k, vt)


def _s5_disc_kernel(lre_ref, lim_ref, ldt_ref, bre_ref, bim_ref, are_ref, aim_ref, bbre_ref, bbim_ref):
    lam_re = lre_ref[...]
    lam_im = lim_ref[...]
    dt = jnp.exp(ldt_ref[...])
    mag = jnp.exp(lam_re * dt)
    abar_re = mag * jnp.cos(lam_im * dt)
    abar_im = mag * jnp.sin(lam_im * dt)
    nr = abar_re - 1.0
    ni = abar_im
    den = lam_re * lam_re + lam_im * lam_im
    f_re = (nr * lam_re + ni * lam_im) / den
    f_im = (ni * lam_re - nr * lam_im) / den
    are_ref[...] = abar_re
    aim_ref[...] = abar_im
    b_re = bre_ref[...]
    b_im = bim_ref[...]
    bbre_ref[...] = f_re * b_re - f_im * b_im
    bbim_ref[...] = f_re * b_im + f_im * b_re


def _s5_disc(lam_re, lam_im, log_dt, b_re, b_im):
    n = SSM_GROUPS * SSM_STATE
    tm = 1024
    col = pl.BlockSpec((tm, 1), lambda i: (i, 0))
    mat = pl.BlockSpec((tm, SSM_GROUP_CH), lambda i: (i, 0))
    return pl.pallas_call(
        _s5_disc_kernel,
        grid=(n // tm,),
        in_specs=[col, col, col, mat, mat],
        out_specs=[col, col, mat, mat],
        out_shape=[jax.ShapeDtypeStruct((n, 1), F32), jax.ShapeDtypeStruct((n, 1), F32),
                   jax.ShapeDtypeStruct((n, SSM_GROUP_CH), F32), jax.ShapeDtypeStruct((n, SSM_GROUP_CH), F32)],
        compiler_params=_cparams(1),
        name="s5_discretise",
    )(lam_re, lam_im, log_dt, b_re, b_im)


def _s5_scan_kernel(u_ref, wb_ref, a_ref, wc_ref, d_ref, y_ref, lhs_ref, s_ref, yb_ref, h_ref):
    @pl.when(pl.program_id(1) == 0)
    def _():
        h_ref[...] = jnp.zeros_like(h_ref)
        lhs_ref[...] = jnp.zeros_like(lhs_ref)

    for b in range(BATCH):
        lhs_ref[0, pl.ds(b, SSM_TCHUNK, stride=SUBLANES), :] = u_ref[b, :, :SSM_GB_CH]
        lhs_ref[1, pl.ds(b + BATCH, SSM_TCHUNK, stride=SUBLANES), :] = u_ref[b, :, SSM_GB_CH:]
    half = SSM_TCHUNK * SUBLANES // 2
    wb = wb_ref[...]
    for r0 in (0, half):
        um = jnp.concatenate([lhs_ref[0, r0:r0 + half], lhs_ref[1, r0:r0 + half]], axis=1).astype(BF16)
        s_ref[r0:r0 + half] = jnp.dot(um, wb, preferred_element_type=F32)

    a_re = a_ref[:, :SSM_GB_ST]
    a_im = a_ref[:, SSM_GB_ST:]

    def step(t, carry):
        h_re, h_im = carry
        r0 = pl.multiple_of(t * SUBLANES, SUBLANES)
        n_re = a_re * h_re - a_im * h_im + s_ref[pl.ds(r0, SUBLANES), :SSM_GB_ST]
        n_im = a_re * h_im + a_im * h_re + s_ref[pl.ds(r0, SUBLANES), SSM_GB_ST:]
        s_ref[pl.ds(r0, SUBLANES), :SSM_GB_ST] = n_re
        s_ref[pl.ds(r0, SUBLANES), SSM_GB_ST:] = n_im
        return n_re, n_im

    h_re, h_im = lax.fori_loop(0, SSM_TCHUNK, step, (h_ref[:, :SSM_GB_ST], h_ref[:, SSM_GB_ST:]), unroll=8)
    h_ref[:, :SSM_GB_ST] = h_re
    h_ref[:, SSM_GB_ST:] = h_im

    wc = wc_ref[...]
    for r0 in (0, half):
        y8 = jnp.dot(s_ref[r0:r0 + half].astype(BF16), wc, preferred_element_type=F32)
        yb_ref[0, r0:r0 + half] = y8[:, :SSM_GB_CH]
        yb_ref[1, r0:r0 + half] = y8[:, SSM_GB_CH:]
    d = d_ref[...]
    for b in range(BATCH):
        y_ref[b, :, :SSM_GB_CH] = (yb_ref[0, pl.ds(b, SSM_TCHUNK, stride=SUBLANES), :]
                                   + d[:, :SSM_GB_CH] * u_ref[b, :, :SSM_GB_CH])
        y_ref[b, :, SSM_GB_CH:] = (yb_ref[1, pl.ds(b + BATCH, SSM_TCHUNK, stride=SUBLANES), :]
                                   + d[:, SSM_GB_CH:] * u_ref[b, :, SSM_GB_CH:])


def _s5_scan(z3, wb, a8, wc, d_row):
    rows = SSM_TCHUNK * SUBLANES
    width = 2 * SSM_GB_CH
    states = 2 * SSM_GB_ST
    return pl.pallas_call(
        _s5_scan_kernel,
        grid=(SSM_PAIRS, SEQ // SSM_TCHUNK),
        in_specs=[pl.BlockSpec((BATCH, SSM_TCHUNK, width), lambda g, c: (0, c, Z_SSM // width + g)),
                  pl.BlockSpec((None, width, states), lambda g, c: (g, 0, 0)),
                  pl.BlockSpec((None, SUBLANES, states), lambda g, c: (g, 0, 0)),
                  pl.BlockSpec((None, states, width), lambda g, c: (g, 0, 0)),
                  pl.BlockSpec((1, width), lambda g, c: (0, g))],
        out_specs=pl.BlockSpec((BATCH, SSM_TCHUNK, width), lambda g, c: (0, c, g)),
        out_shape=jax.ShapeDtypeStruct((BATCH, SEQ, SSM_WIDTH), F32),
        scratch_shapes=[pltpu.VMEM((2, rows, SSM_GB_CH), F32), pltpu.VMEM((rows, states), F32),
                        pltpu.VMEM((2, rows, SSM_GB_CH), F32), pltpu.VMEM((SUBLANES, states), F32)],
        compiler_params=_cparams(2),
        name="s5_scan",
    )(z3, wb, a8, wc, d_row)


def _gelu_tanh(x):
    return 0.5 * x * (1.0 + jnp.tanh(math.sqrt(2.0 / math.pi) * (x + 0.044715 * (x * x * x))))


def _glu_kernel(y_ref, w_ref, b_ref, g_ref, o_ref):
    g = _gelu_tanh(y_ref[...])
    gate = jax.nn.sigmoid(jnp.dot(g.astype(BF16), w_ref[...], preferred_element_type=F32) + b_ref[...])
    o_ref[...] = _rms(g * gate, g_ref[...]).astype(o_ref.dtype)


def _glu(y, w, b, gain):
    tm = 256
    return pl.pallas_call(
        _glu_kernel,
        grid=(TOKENS // tm,),
        in_specs=[pl.BlockSpec((tm, SSM_WIDTH), lambda i: (i, 0)),
                  pl.BlockSpec((SSM_WIDTH, SSM_WIDTH), lambda i: (0, 0)),
                  pl.BlockSpec((1, SSM_WIDTH), lambda i: (0, 0)),
                  pl.BlockSpec((1, SSM_WIDTH), lambda i: (0, 0))],
        out_specs=pl.BlockSpec((tm, SSM_WIDTH), lambda i: (i, 0)),
        out_shape=jax.ShapeDtypeStruct((TOKENS, SSM_WIDTH), BF16),
        compiler_params=_cparams(1),
        name="s5_glu",
    )(y, w, b, gain)


def _out_proj_kernel(om_ref, os_ref, g_ref, w_ref, x_ref, ga_ref, o_ref, lhs_ref):
    @pl.when(pl.program_id(1) == 0)
    def _():
        lhs_ref[:, :MLA_WIDTH] = _rms(om_ref[...], g_ref[...]).astype(BF16)
        lhs_ref[:, MLA_WIDTH:] = os_ref[...]

    o_ref[...] = x_ref[...] + ga_ref[...] * jnp.dot(lhs_ref[...], w_ref[...], preferred_element_type=F32)


def _out_proj(o_mla, o_ssm, gain, w, x2, mod3, gate_chunk):
    tm, tn = 512, 1024
    per_b = SEQ // tm
    nj = D_MODEL // tn
    return pl.pallas_call(
        _out_proj_kernel,
        grid=(TOKENS // tm, nj),
        in_specs=[pl.BlockSpec((tm, MLA_WIDTH), lambda i, j: (i, 0)),
                  pl.BlockSpec((tm, SSM_WIDTH), lambda i, j: (i, 0)),
                  pl.BlockSpec((1, MLA_WIDTH), lambda i, j: (0, 0)),
                  pl.BlockSpec((D_MODEL, tn), lambda i, j: (0, j)),
                  pl.BlockSpec((tm, tn), lambda i, j: (i, j)),
                  pl.BlockSpec((None, 1, tn), lambda i, j: (i // per_b, 0, gate_chunk * nj + j))],
        out_specs=pl.BlockSpec((tm, tn), lambda i, j: (i, j)),
        out_shape=jax.ShapeDtypeStruct((TOKENS, D_MODEL), F32),
        scratch_shapes=[pltpu.VMEM((tm, D_MODEL), BF16)],
        compiler_params=_cparams(2),
        name="out_proj",
    )(o_mla, o_ssm, gain, w, x2, mod3)


def _router_kernel(x_ref, g_ref, sh_ref, sc_ref, whi_ref, wlo_ref, br_ref, h_ref, eid_ref, wt_ref):
    h = _rms(x_ref[...], g_ref[...]) * (1.0 + sc_ref[...]) + sh_ref[...]
    h_ref[...] = h
    h_hi = h.astype(BF16)
    h_lo = (h - h_hi.astype(F32)).astype(BF16)

    whi = whi_ref[...]
    logits = (jnp.dot(h_hi, whi, preferred_element_type=F32)
              + jnp.dot(h_lo, whi, preferred_element_type=F32)
              + jnp.dot(h_hi, wlo_ref[...], preferred_element_type=F32)
              + br_ref[...])

    col = lax.broadcasted_iota(jnp.int32, logits.shape, 1)
    neg = -jnp.inf
    is_g = col < N_EGROUPS
    gl = jnp.where(is_g, logits, neg)
    gmax = jnp.max(gl, axis=1, keepdims=True)
    grp = jnp.min(jnp.where(gl == gmax, col, LANES), axis=1, keepdims=True)
    p_grp = 1.0 / jnp.sum(jnp.where(is_g, jnp.exp(logits - gmax), 0.0), axis=1, keepdims=True)

    ecol = col - N_EGROUPS
    in_grp = (ecol >= 0) & (ecol < N_EXPERTS) & ((ecol // EXPERTS_PER_GROUP) == grp)
    el = jnp.where(in_grp, logits, neg)
    v1 = jnp.max(el, axis=1, keepdims=True)
    i1 = jnp.min(jnp.where(el == v1, col, LANES), axis=1, keepdims=True)
    el2 = jnp.where(col == i1, neg, el)
    v2 = jnp.max(el2, axis=1, keepdims=True)
    i2 = jnp.min(jnp.where(el2 == v2, col, LANES), axis=1, keepdims=True)

    s2 = jnp.exp(v2 - v1)
    w1 = p_grp / (1.0 + s2)
    w2 = p_grp * s2 / (1.0 + s2)
    eid_ref[...] = jnp.where(col == 0, i1 - N_EGROUPS, jnp.where(col == 1, i2 - N_EGROUPS, 0))
    wt_ref[...] = jnp.where(col == 0, w1, jnp.where(col == 1, w2, 0.0))


def _router(x1, gain, mod3, shift_chunk, scale_chunk, w_hi, w_lo, b_row):
    tm = 256
    per_b = SEQ // tm
    return pl.pallas_call(
        _router_kernel,
        grid=(TOKENS // tm,),
        in_specs=[pl.BlockSpec((tm, D_MODEL), lambda i: (i, 0)),
                  pl.BlockSpec((1, D_MODEL), lambda i: (0, 0)),
                  pl.BlockSpec((None, 1, D_MODEL), lambda i: (i // per_b, 0, shift_chunk)),
                  pl.BlockSpec((None, 1, D_MODEL), lambda i: (i // per_b, 0, scale_chunk)),
                  pl.BlockSpec((D_MODEL, LANES), lambda i: (0, 0)),
                  pl.BlockSpec((D_MODEL, LANES), lambda i: (0, 0)),
                  pl.BlockSpec((1, LANES), lambda i: (0, 0))],
        out_specs=[pl.BlockSpec((tm, D_MODEL), lambda i: (i, 0)),
                   pl.BlockSpec((tm, LANES), lambda i: (i, 0)),
                   pl.BlockSpec((tm, LANES), lambda i: (i, 0))],
        out_shape=[jax.ShapeDtypeStruct((TOKENS, D_MODEL), F32),
                   jax.ShapeDtypeStruct((TOKENS, LANES), jnp.int32),
                   jax.ShapeDtypeStruct((TOKENS, LANES), F32)],
        compiler_params=_cparams(1),
        name="ffn_router",
    )(x1, gain, mod3, mod3, w_hi, w_lo, b_row)


def _moe_plan(eid):
    n_assign = TOKENS * TOP_K
    e_flat = eid.reshape(n_assign)
    onehot = (e_flat[:, None] == jnp.arange(N_EXPERTS, dtype=jnp.int32)[None, :]).astype(jnp.int32)
    csum = jnp.cumsum(onehot, axis=0)
    counts = csum[-1]
    rank = jnp.sum((csum - onehot) * onehot, axis=1)
    nblk_e = (counts + MOE_BLK - 1) // MOE_BLK
    bend = jnp.cumsum(nblk_e)
    bstart = bend - nblk_e
    dest = jnp.sum(onehot * bstart[None, :], axis=1) * MOE_BLK + rank
    a_id = jnp.arange(n_assign, dtype=jnp.int32)
    row_dst = jnp.zeros((MOE_ROWS,), jnp.int32).at[dest].set((a_id % TOP_K) * TOKENS + a_id // TOP_K,
                                                              unique_indices=True)
    blk = jnp.arange(MOE_NBLK, dtype=jnp.int32)
    n_used = bend[-1]
    blk_e = jnp.minimum(jnp.sum((bend[None, :] <= blk[:, None]).astype(jnp.int32), axis=1), N_EXPERTS - 1)
    blk_nvalid = jnp.clip(counts[blk_e] - (blk - bstart[blk_e]) * MOE_BLK, 0, MOE_BLK)
    blk_nvalid = jnp.where(blk < n_used, blk_nvalid, 0).astype(jnp.int32)
    last_e = blk_e[jnp.maximum(n_used - 1, 0)]
    blk_e = jnp.where(blk < n_used, blk_e, last_e).astype(jnp.int32)
    return row_dst, blk_e, blk_nvalid, n_used.reshape(1).astype(jnp.int32)


def _copy_groups(n_rows):
    return lax.shift_right_logical(n_rows + (DMA_UNROLL - 1), DMA_UNROLL.bit_length() - 1)


def _expert_up_kernel(dst_ref, be_ref, nv_ref, nu_ref, h_hbm, w1_ref, w3_ref, o_ref, xbuf, sem):
    i = pl.program_id(0)
    slot = i % 2

    def row_copy(src_row, dst_slot, dst_row):
        return pltpu.make_async_copy(h_hbm.at[pl.ds(src_row, 1)], xbuf.at[dst_slot, pl.ds(dst_row, 1)],
                                     sem.at[dst_slot])

    def issue(block, dst_slot):
        base = block * MOE_BLK

        def body(g, c):
            for u in range(DMA_UNROLL):
                r = g * DMA_UNROLL + u
                row_copy(dst_ref[base + r] & (TOKENS - 1), dst_slot, r).start()
            return c

        lax.fori_loop(0, _copy_groups(nv_ref[block]), body, 0)

    @pl.when(i == 0)
    def _():
        xbuf[...] = jnp.zeros_like(xbuf)
        issue(0, 0)

    @pl.when(i + 1 < pl.num_programs(0))
    def _():
        issue(i + 1, 1 - slot)

    def wait_body(g, c):
        for u in range(DMA_UNROLL):
            row_copy(0, slot, g * DMA_UNROLL + u).wait()
        return c

    lax.fori_loop(0, _copy_groups(nv_ref[i]), wait_body, 0)

    @pl.when(i < nu_ref[0])
    def _():
        x = xbuf[slot].astype(BF16)
        a = jnp.dot(x, w1_ref[...].astype(BF16), preferred_element_type=F32)
        b = jnp.dot(x, w3_ref[...].astype(BF16), preferred_element_type=F32)
        o_ref[...] = (a * jax.nn.sigmoid(a) * b).astype(o_ref.dtype)

    @pl.when(i >= nu_ref[0])
    def _():
        o_ref[...] = jnp.zeros_like(o_ref)


def _expert_up(row_dst, blk_e, blk_nvalid, n_used, h2, w1, w3):
    wspec = pl.BlockSpec((None, D_MODEL, D_EXPERT), lambda i, tok, be, ng, nu: (be[i], 0, 0))
    return pl.pallas_call(
        _expert_up_kernel,
        grid_spec=pltpu.PrefetchScalarGridSpec(
            num_scalar_prefetch=4,
            grid=(MOE_NBLK,),
            in_specs=[pl.BlockSpec(memory_space=pl.ANY), wspec, wspec],
            out_specs=pl.BlockSpec((MOE_BLK, D_EXPERT), lambda i, tok, be, ng, nu: (i, 0)),
            scratch_shapes=[pltpu.VMEM((2, MOE_BLK, D_MODEL), F32),
                            pltpu.SemaphoreType.DMA((2,))]),
        out_shape=jax.ShapeDtypeStruct((MOE_ROWS, D_EXPERT), BF16),
        compiler_params=_cparams(1, 60 * 1024 * 1024),
        name="moe_expert_up",
    )(row_dst, blk_e, blk_nvalid, n_used, h2, w1, w3)


def _expert_down_kernel(dst_ref, be_ref, nv_ref, nu_ref, h_ref, w_ref, y_hbm, ybuf, sem):
    i = pl.program_id(0)
    slot = i % 2

    def row_copy(src_row, dst_row, src_slot):
        return pltpu.make_async_copy(ybuf.at[src_slot, pl.ds(src_row, 1)], y_hbm.at[pl.ds(dst_row, 1)],
                                     sem.at[src_slot])

    def drain(block, src_slot):
        def body(r, c):
            row_copy(r, 0, src_slot).wait()
            return c

        lax.fori_loop(0, nv_ref[block], body, 0)

    @pl.when(i >= 2)
    def _():
        drain(i - 2, slot)

    @pl.when(i < nu_ref[0])
    def _():
        ybuf[slot] = jnp.dot(h_ref[...], w_ref[...].astype(BF16), preferred_element_type=F32)
        base = i * MOE_BLK
        nv = nv_ref[i]
        full = lax.shift_right_logical(nv, DMA_UNROLL.bit_length() - 1)

        def body(g, c):
            for u in range(DMA_UNROLL):
                r = g * DMA_UNROLL + u
                row_copy(r, dst_ref[base + r], slot).start()
            return c

        lax.fori_loop(0, full, body, 0)

        def tail(r, c):
            row_copy(r, dst_ref[base + r], slot).start()
            return c

        lax.fori_loop(full * DMA_UNROLL, nv, tail, 0)

    @pl.when(i == pl.num_programs(0) - 1)
    def _():
        drain(i - 1, 1 - slot)
        drain(i, slot)


def _expert_down(row_dst, blk_e, blk_nvalid, n_used, hmid, w2):
    return pl.pallas_call(
        _expert_down_kernel,
        grid_spec=pltpu.PrefetchScalarGridSpec(
            num_scalar_prefetch=4,
            grid=(MOE_NBLK,),
            in_specs=[pl.BlockSpec((MOE_BLK, D_EXPERT), lambda i, dst, be, nv, nu: (i, 0)),
                      pl.BlockSpec((None, D_EXPERT, D_MODEL), lambda i, dst, be, nv, nu: (be[i], 0, 0))],
            out_specs=pl.BlockSpec(memory_space=pl.ANY),
            scratch_shapes=[pltpu.VMEM((2, MOE_BLK, D_MODEL), F32),
                            pltpu.SemaphoreType.DMA((2,))]),
        out_shape=jax.ShapeDtypeStruct((TOP_K * TOKENS, D_MODEL), F32),
        compiler_params=_cparams(1),
        name="moe_expert_down",
    )(row_dst, blk_e, blk_nvalid, n_used, hmid, w2)


def _combine_kernel(y0_ref, y1_ref, x_ref, wt_ref, gf_ref, fg_ref, o_ref):
    wt = wt_ref[...]
    moe = wt[:, 0:1] * y0_ref[...] + wt[:, 1:2] * y1_ref[...]
    o_ref[...] = _rms(x_ref[...] + gf_ref[...] * moe, fg_ref[...])


def _combine(ys, x1, wt, mod3, gate_chunk, final_gain):
    per_b = SEQ // CMB_TM
    nblk = TOKENS // CMB_TM
    return pl.pallas_call(
        _combine_kernel,
        grid=(nblk,),
        in_specs=[pl.BlockSpec((CMB_TM, D_MODEL), lambda i: (i, 0)),
                  pl.BlockSpec((CMB_TM, D_MODEL), lambda i: (i + nblk, 0)),
                  pl.BlockSpec((CMB_TM, D_MODEL), lambda i: (i, 0)),
                  pl.BlockSpec((CMB_TM, LANES), lambda i: (i, 0)),
                  pl.BlockSpec((None, 1, D_MODEL), lambda i: (i // per_b, 0, gate_chunk)),
                  pl.BlockSpec((1, D_MODEL), lambda i: (0, 0))],
        out_specs=pl.BlockSpec((CMB_TM, D_MODEL), lambda i: (i, 0)),
        out_shape=jax.ShapeDtypeStruct((TOKENS, D_MODEL), F32),
        compiler_params=_cparams(1),
        name="moe_combine_final_norm",
    )(ys, ys, x1, wt, mod3, final_gain)


def _rotate_half_cols(w):
    half = w.shape[-1] // 2
    return jnp.concatenate([-w[..., half:], w[..., :half]], axis=-1)


def _block_diag_in(bbar):
    bb = bbar.reshape(SSM_GROUPS // SSM_GB, SSM_GB, SSM_STATE, SSM_GROUP_CH)
    w = jnp.einsum('bgpn,gh->bgnhp', bb, jnp.eye(SSM_GB, dtype=bbar.dtype))
    return w.reshape(SSM_GROUPS // SSM_GB, SSM_GB_CH, SSM_GB_ST)


def _block_diag_out(cmat):
    cc = cmat.reshape(SSM_GROUPS // SSM_GB, SSM_GB, SSM_GROUP_CH, SSM_STATE)
    w = jnp.einsum('bgnp,gh->bgphn', cc, jnp.eye(SSM_GB, dtype=cmat.dtype))
    return w.reshape(SSM_GROUPS // SSM_GB, SSM_GB_ST, SSM_GB_CH)


def kernel(x, c, positions, w_ada, b_ada, norm_mix_gain, w_in, q_lat_gain, w_uq, kv_lat_gain, w_ukv,
           ssm_lam_re, ssm_lam_im, ssm_log_dt, ssm_b_re, ssm_b_im, ssm_c_re, ssm_c_im, ssm_d,
           w_glu, b_glu, mla_out_gain, ssm_out_gain, w_out, norm_ffn_gain,
           w_group_router, b_group_router, w_expert_router, b_expert_router,
           w1_experts, w3_experts, w2_experts, final_gain):
    x2 = x.reshape(TOKENS, D_MODEL)

    c8 = jnp.zeros((SUBLANES, D_MODEL), F32).at[:BATCH].set(c)
    mod = _ada(c8, w_ada[0], b_ada[0].reshape(1, 6 * D_MODEL))
    mod3 = mod.reshape(SUBLANES, 1, 6 * D_MODEL)

    h = _norm_mod(x2, norm_mix_gain[0].reshape(1, D_MODEL), mod3, 0, 1)
    w_in0 = w_in[0]
    s_kv = Q_RANK
    s_kr = Q_RANK + KV_RANK
    s_ssm = s_kr + QK_ROPE
    w_kr = w_in0[:, s_kr:s_ssm]
    w_z = jnp.concatenate([w_in0[:, :s_kv], w_in0[:, s_ssm:], w_in0[:, s_kv:s_kr], w_kr, _rotate_half_cols(w_kr)],
                          axis=1).astype(BF16)
    z = _matmul(h, w_z, F32, 512, Z_COLS // 3, "in_proj")

    cs = _rope_table(positions.reshape(TOKENS, 1))

    wq = w_uq[0].reshape(Q_RANK, MLA_HEADS, QK_HEAD)
    wq_heads = jnp.concatenate([wq, _rotate_half_cols(wq[:, :, QK_NOPE:])], axis=-1)
    wq_heads = wq_heads.transpose(1, 0, 2).astype(BF16)
    q = _q_proj(z, q_lat_gain[0].reshape(1, Q_RANK), wq_heads, cs)
    wkv = w_ukv[0].reshape(KV_RANK, MLA_HEADS, QK_NOPE + V_HEAD)
    wk_heads = wkv[:, :, :QK_NOPE].transpose(1, 0, 2).astype(BF16)
    wvt_heads = wkv[:, :, QK_NOPE:].transpose(1, 2, 0).astype(BF16)
    k, vt = _kv_proj(z, kv_lat_gain[0].reshape(1, KV_RANK), wk_heads, wvt_heads, cs)
    o_mla = _attention(q, k, vt)

    n_st = SSM_GROUPS * SSM_STATE
    abar_re, abar_im, bbar_re, bbar_im = _s5_disc(
        ssm_lam_re[0].reshape(n_st, 1), ssm_lam_im[0].reshape(n_st, 1),
        jnp.repeat(ssm_log_dt[0], SSM_STATE).reshape(n_st, 1),
        ssm_b_re[0].reshape(n_st, SSM_GROUP_CH), ssm_b_im[0].reshape(n_st, SSM_GROUP_CH))
    n_gb = SSM_GROUPS // SSM_GB
    wb = jnp.concatenate([_block_diag_in(bbar_re), _block_diag_in(bbar_im)], axis=-1)
    wb = wb.reshape(SSM_PAIRS, 2 * SSM_GB_CH, 2 * SSM_GB_ST).astype(BF16)
    wc = jnp.concatenate([_block_diag_out(ssm_c_re[0]), -_block_diag_out(ssm_c_im[0])], axis=1)
    wc = wc.reshape(SSM_PAIRS, 2, 2 * SSM_GB_ST, SSM_GB_CH).transpose(0, 2, 1, 3)
    wc = wc.reshape(SSM_PAIRS, 2 * SSM_GB_ST, 2 * SSM_GB_CH).astype(BF16)
    a16 = jnp.concatenate([abar_re.reshape(n_gb, SSM_GB_ST), abar_im.reshape(n_gb, SSM_GB_ST)], axis=-1)
    a8 = jnp.repeat(a16.reshape(SSM_PAIRS, 2, 2 * SSM_GB_ST), BATCH, axis=1)

    y = _s5_scan(z.reshape(BATCH, SEQ, Z_COLS), wb, a8, wc, ssm_d[0].reshape(1, SSM_WIDTH))
    o_ssm = _glu(y.reshape(TOKENS, SSM_WIDTH), w_glu[0].astype(BF16), b_glu[0].reshape(1, SSM_WIDTH),
                 ssm_out_gain[0].reshape(1, SSM_WIDTH))

    x1 = _out_proj(o_mla, o_ssm, mla_out_gain[0].reshape(1, MLA_WIDTH), w_out[0].astype(BF16), x2, mod3, 2)

    w_r = jnp.concatenate([w_group_router[0], w_expert_router[0],
                           jnp.zeros((D_MODEL, LANES - N_EGROUPS - N_EXPERTS), F32)], axis=1)
    w_r_hi = w_r.astype(BF16)
    w_r_lo = (w_r - w_r_hi.astype(F32)).astype(BF16)
    b_r = jnp.concatenate([b_group_router[0], b_expert_router[0],
                           jnp.zeros((LANES - N_EGROUPS - N_EXPERTS,), F32)]).reshape(1, LANES)
    h2, eid, wt = _router(x1, norm_ffn_gain[0].reshape(1, D_MODEL), mod3, 3, 4, w_r_hi, w_r_lo, b_r)

    row_dst, blk_e, blk_nvalid, n_used = _moe_plan(eid[:, :TOP_K])
    hmid = _expert_up(row_dst, blk_e, blk_nvalid, n_used, h2, w1_experts[0], w3_experts[0])
    ys = _expert_down(row_dst, blk_e, blk_nvalid, n_used, hmid, w2_experts[0])
    out = _combine(ys, x1, wt, mod3, 5, final_gain.reshape(1, D_MODEL))
    return out.reshape(BATCH, SEQ, D_MODEL)
```

```python
import math

import jax
import jax.numpy as jnp
from jax import lax
from jax.experimental import pallas as pl
from jax.experimental.pallas import tpu as pltpu

F32 = jnp.float32
BF16 = jnp.bfloat16

D_MODEL = 4096
BATCH = 4
SEQ = 2048
TOKENS = BATCH * SEQ
CHUNK = 64
EPS = 1e-6

MLA_HEADS = 16
QK_NOPE = 128
QK_ROPE = 64
QK_HEAD = QK_NOPE + QK_ROPE
V_HEAD = 128
Q_RANK = 768
KV_RANK = 512
ROPE_THETA = 10000.0
MLA_WIDTH = MLA_HEADS * V_HEAD

SSM_WIDTH = D_MODEL - MLA_WIDTH
SSM_GROUP_CH = 16
SSM_GROUPS = SSM_WIDTH // SSM_GROUP_CH
SSM_STATE = 64

N_EGROUPS = 8
EXPERTS_PER_GROUP = 8
N_EXPERTS = N_EGROUPS * EXPERTS_PER_GROUP
TOP_K = 2
D_EXPERT = 512

LANES = 128
SUBLANES = 8
VMEM_LIMIT = 56 * 1024 * 1024

Z_Q = 0
Z_SSM = Q_RANK
Z_KV = Z_SSM + SSM_WIDTH
Z_KR = Z_KV + KV_RANK
Z_COLS = Z_KR + 2 * QK_ROPE

ATT_BLK = 256
ATT_NBLK = SEQ // ATT_BLK

SSM_GB = 8
SSM_GB_CH = SSM_GB * SSM_GROUP_CH
SSM_GB_ST = SSM_GB * SSM_STATE
SSM_PAIRS = SSM_GROUPS // (2 * SSM_GB)
SSM_TCHUNK = 256

MOE_BLK = 256
MOE_NBLK = (TOKENS * TOP_K + N_EXPERTS * (MOE_BLK - 1)) // MOE_BLK + 1
MOE_ROWS = MOE_NBLK * MOE_BLK
DMA_UNROLL = 8
CMB_TM = 128


def _cparams(n_axes, vmem=VMEM_LIMIT):
    return pltpu.CompilerParams(dimension_semantics=("arbitrary",) * n_axes, vmem_limit_bytes=vmem)


def _ada_kernel(c_ref, w_ref, b_ref, o_ref):
    c = c_ref[...]
    ca = (c * jax.nn.sigmoid(c)).astype(BF16)
    o_ref[...] = jnp.dot(ca, w_ref[...].astype(BF16), preferred_element_type=F32) + b_ref[...]


def _ada(c8, w, b):
    n = w.shape[1]
    tn = 512
    return pl.pallas_call(
        _ada_kernel,
        grid=(n // tn,),
        in_specs=[pl.BlockSpec((SUBLANES, D_MODEL), lambda j: (0, 0)),
                  pl.BlockSpec((D_MODEL, tn), lambda j: (0, j)),
                  pl.BlockSpec((1, tn), lambda j: (0, j))],
        out_specs=pl.BlockSpec((SUBLANES, tn), lambda j: (0, j)),
        out_shape=jax.ShapeDtypeStruct((SUBLANES, n), F32),
        compiler_params=_cparams(1),
        name="ada_mod",
    )(c8, w, b)


def _rms(x, gain):
    return x * lax.rsqrt(jnp.mean(x * x, axis=-1, keepdims=True) + EPS) * gain


def _in_proj_kernel(x_ref, g_ref, sh_ref, sc_ref, w_ref, o_ref, h_ref):
    @pl.when(pl.program_id(1) == 0)
    def _():
        y = _rms(x_ref[...], g_ref[...])
        h_ref[...] = (y * (1.0 + sc_ref[...]) + sh_ref[...]).astype(h_ref.dtype)

    o_ref[...] = jnp.dot(h_ref[...], w_ref[...], preferred_element_type=F32)


def _in_proj(x2, gain, mod3, shift_chunk, scale_chunk, w):
    tm = 512
    n = w.shape[1]
    tn = n // 3
    per_b = SEQ // tm
    return pl.pallas_call(
        _in_proj_kernel,
        grid=(TOKENS // tm, n // tn),
        in_specs=[pl.BlockSpec((tm, D_MODEL), lambda i, j: (i, 0)),
                  pl.BlockSpec((1, D_MODEL), lambda i, j: (0, 0)),
                  pl.BlockSpec((None, 1, D_MODEL), lambda i, j: (i // per_b, 0, shift_chunk)),
                  pl.BlockSpec((None, 1, D_MODEL), lambda i, j: (i // per_b, 0, scale_chunk)),
                  pl.BlockSpec((D_MODEL, tn), lambda i, j: (0, j))],
        out_specs=pl.BlockSpec((tm, tn), lambda i, j: (i, j)),
        out_shape=jax.ShapeDtypeStruct((TOKENS, n), F32),
        scratch_shapes=[pltpu.VMEM((tm, D_MODEL), BF16)],
        compiler_params=_cparams(2),
        name="in_proj",
    )(x2, gain, mod3, mod3, w)


W_IN_KV = Q_RANK
W_IN_KR = Q_RANK + KV_RANK
W_IN_SSM = W_IN_KR + QK_ROPE
W_IN_COLS = W_IN_SSM + SSM_WIDTH


def _wz_kernel(w_ref, o_ref):
    o_ref[:, Z_Q:Z_Q + Q_RANK] = w_ref[:, :W_IN_KV].astype(o_ref.dtype)
    o_ref[:, Z_SSM:Z_SSM + SSM_WIDTH] = w_ref[:, W_IN_SSM:].astype(o_ref.dtype)
    o_ref[:, Z_KV:Z_KV + KV_RANK] = w_ref[:, W_IN_KV:W_IN_KR].astype(o_ref.dtype)
    kr = w_ref[:, W_IN_KR:W_IN_SSM]
    half = QK_ROPE // 2
    o_ref[:, Z_KR:] = jnp.concatenate([kr, -kr[:, half:], kr[:, :half]], axis=1).astype(o_ref.dtype)


def _wz_prep(w_in0):
    tk = 512
    return pl.pallas_call(
        _wz_kernel,
        grid=(D_MODEL // tk,),
        in_specs=[pl.BlockSpec((tk, W_IN_COLS), lambda i: (i, 0))],
        out_specs=pl.BlockSpec((tk, Z_COLS), lambda i: (i, 0)),
        out_shape=jax.ShapeDtypeStruct((D_MODEL, Z_COLS), BF16),
        compiler_params=_cparams(1),
        name="in_proj_weight_prep",
    )(w_in0)


def _rope_table_kernel(pos_ref, o_ref):
    lane = lax.broadcasted_iota(jnp.int32, (1, LANES), 1)
    pair = (lane % (QK_ROPE // 2)).astype(F32)
    inv_freq = jnp.exp(-math.log(ROPE_THETA) * (2.0 * pair) / QK_ROPE)
    ang = pos_ref[...].astype(F32) * inv_freq
    o_ref[...] = jnp.where(lane < QK_ROPE, jnp.cos(ang), jnp.sin(ang))


def _rope_table(pos_col):
    tm = 1024
    return pl.pallas_call(
        _rope_table_kernel,
        grid=(TOKENS // tm,),
        in_specs=[pl.BlockSpec((tm, 1), lambda i: (i, 0))],
        out_specs=pl.BlockSpec((tm, LANES), lambda i: (i, 0)),
        out_shape=jax.ShapeDtypeStruct((TOKENS, LANES), F32),
        compiler_params=_cparams(1),
        name="rope_table",
    )(pos_col)


def _rope_pair(t, cs):
    u = t * cs
    return u + pltpu.roll(u, QK_ROPE, axis=1)


def _q_proj_kernel(ql_ref, g_ref, w_ref, cs_ref, o_ref):
    hn = _rms(ql_ref[...], g_ref[...]).astype(BF16)
    cs = cs_ref[...]
    for h in range(MLA_HEADS):
        r = jnp.dot(hn, w_ref[:, h * 2 * LANES:(h + 1) * 2 * LANES], preferred_element_type=F32)
        o_ref[h, :, :QK_NOPE] = r[:, :QK_NOPE].astype(o_ref.dtype)
        o_ref[h, :, QK_NOPE:] = _rope_pair(r[:, QK_NOPE:], cs)[:, :QK_ROPE].astype(o_ref.dtype)


def _q_proj(z, gain, w_heads, cs):
    tm = ATT_BLK
    per_b = SEQ // tm
    return pl.pallas_call(
        _q_proj_kernel,
        grid=(TOKENS // tm,),
        in_specs=[pl.BlockSpec((tm, Q_RANK), lambda i: (i, Z_Q // Q_RANK)),
                  pl.BlockSpec((1, Q_RANK), lambda i: (0, 0)),
                  pl.BlockSpec((Q_RANK, MLA_HEADS * 2 * LANES), lambda i: (0, 0)),
                  pl.BlockSpec((tm, LANES), lambda i: (i, 0))],
        out_specs=pl.BlockSpec((None, MLA_HEADS, tm, QK_HEAD), lambda i: (i // per_b, 0, i % per_b, 0)),
        out_shape=jax.ShapeDtypeStruct((BATCH, MLA_HEADS, SEQ, QK_HEAD), BF16),
        compiler_params=_cparams(1),
        name="mla_q_proj",
    )(z, gain, w_heads, cs)


def _kv_proj_kernel(kva_ref, kvb_ref, kr_ref, g_ref, wk_ref, wvt_ref, cs_ref, k_ref, vt_ref):
    kvl = jnp.concatenate([kva_ref[...], kvb_ref[...]], axis=1)
    hn = _rms(kvl, g_ref[...]).astype(BF16)
    kr = _rope_pair(kr_ref[...], cs_ref[...])[:, :QK_ROPE].astype(k_ref.dtype)
    for h in range(MLA_HEADS):
        k_ref[h, :, :QK_NOPE] = jnp.dot(hn, wk_ref[:, h * QK_NOPE:(h + 1) * QK_NOPE],
                                        preferred_element_type=F32).astype(k_ref.dtype)
        k_ref[h, :, QK_NOPE:] = kr
        vt_ref[h] = lax.dot_general(wvt_ref[h], hn, (((1,), (1,)), ((), ())),
                                    preferred_element_type=F32).astype(vt_ref.dtype)


def _kv_proj(z, gain, wk_heads, wvt_heads, cs):
    tm = ATT_BLK
    per_b = SEQ // tm
    half = KV_RANK // 2
    return pl.pallas_call(
        _kv_proj_kernel,
        grid=(TOKENS // tm,),
        in_specs=[pl.BlockSpec((tm, half), lambda i: (i, Z_KV // half)),
                  pl.BlockSpec((tm, half), lambda i: (i, Z_KV // half + 1)),
                  pl.BlockSpec((tm, LANES), lambda i: (i, Z_KR // LANES)),
                  pl.BlockSpec((1, KV_RANK), lambda i: (0, 0)),
                  pl.BlockSpec((KV_RANK, MLA_HEADS * QK_NOPE), lambda i: (0, 0)),
                  pl.BlockSpec((MLA_HEADS, V_HEAD, KV_RANK), lambda i: (0, 0, 0)),
                  pl.BlockSpec((tm, LANES), lambda i: (i, 0))],
        out_specs=[pl.BlockSpec((None, MLA_HEADS, tm, QK_HEAD), lambda i: (i // per_b, 0, i % per_b, 0)),
                   pl.BlockSpec((None, MLA_HEADS, None, V_HEAD, tm), lambda i: (i // per_b, 0, i % per_b, 0, 0))],
        out_shape=[jax.ShapeDtypeStruct((BATCH, MLA_HEADS, SEQ, QK_HEAD), BF16),
                   jax.ShapeDtypeStruct((BATCH, MLA_HEADS, ATT_NBLK, V_HEAD, ATT_BLK), BF16)],
        compiler_params=_cparams(1),
        name="mla_kv_proj",
    )(z, z, z, gain, wk_heads, wvt_heads, cs)


ATT_HEADS = 4


def _attn_kernel(q_ref, k_ref, vt_ref, o_ref, acc_ref):
    log2_scale = (QK_HEAD ** -0.5) * math.log2(math.e)
    key_chunk = lax.broadcasted_iota(jnp.int32, (ATT_BLK, ATT_BLK), 0) // CHUNK
    qry_chunk = lax.broadcasted_iota(jnp.int32, (ATT_BLK, ATT_BLK), 1) // CHUNK
    diag_mask = key_chunk <= qry_chunk

    def q_body(qi, carry):
        q0 = pl.multiple_of(qi * ATT_BLK, ATT_BLK)

        def kv_block(j, stats, masked):
            k0 = pl.multiple_of(j * ATT_BLK, ATT_BLK)
            sts = [lax.dot_general(k_ref[h, pl.ds(k0, ATT_BLK), :], q_ref[h, pl.ds(q0, ATT_BLK), :],
                                   (((1,), (1,)), ((), ())), preferred_element_type=F32)
                   for h in range(ATT_HEADS)]
            new_stats, ps, alphas = [], [], []
            for h in range(ATT_HEADS):
                m, l = stats[h]
                st = sts[h] * log2_scale
                if masked:
                    st = jnp.where(diag_mask, st, -jnp.inf)
                m_new = jnp.maximum(m, jnp.max(st, axis=0, keepdims=True))
                alpha = jnp.exp2(m - m_new)
                p = jnp.exp2(st - m_new)
                new_stats.append((m_new, alpha * l + jnp.sum(p, axis=0, keepdims=True)))
                ps.append(p.astype(BF16))
                alphas.append(alpha)
            for h in range(ATT_HEADS):
                acc_ref[h] = alphas[h] * acc_ref[h] + jnp.dot(vt_ref[h, j], ps[h], preferred_element_type=F32)
            return tuple(new_stats)

        acc_ref[...] = jnp.zeros_like(acc_ref)
        stats = tuple((jnp.full((1, ATT_BLK), -jnp.inf, F32), jnp.zeros((1, ATT_BLK), F32))
                      for _ in range(ATT_HEADS))
        stats = lax.fori_loop(0, qi, lambda j, st: kv_block(j, st, False), stats)
        stats = kv_block(qi, stats, True)
        for h in range(ATT_HEADS):
            o_ref[pl.ds(q0, ATT_BLK), h * V_HEAD:(h + 1) * V_HEAD] = (acc_ref[h] / stats[h][1]).T
        return carry

    lax.fori_loop(0, ATT_NBLK, q_body, 0)


def _attention(q, k, vt):
    return pl.pallas_call(
        _attn_kernel,
        grid=(BATCH, MLA_HEADS // ATT_HEADS),
        in_specs=[pl.BlockSpec((None, ATT_HEADS, SEQ, QK_HEAD), lambda b, h: (b, h, 0, 0)),
                  pl.BlockSpec((None, ATT_HEADS, SEQ, QK_HEAD), lambda b, h: (b, h, 0, 0)),
                  pl.BlockSpec((None, ATT_HEADS, ATT_NBLK, V_HEAD, ATT_BLK), lambda b, h: (b, h, 0, 0, 0))],
        out_specs=pl.BlockSpec((SEQ, ATT_HEADS * V_HEAD), lambda b, h: (b, h)),
        out_shape=jax.ShapeDtypeStruct((TOKENS, MLA_WIDTH), F32),
        scratch_shapes=[pltpu.VMEM((ATT_HEADS, V_HEAD, ATT_BLK), F32)],
        compiler_params=_cparams(2),
        name="mla_attention",
    )(q, k, vt)


def _s5_disc_kernel(lre_ref, lim_ref, ldt_ref, bre_ref, bim_ref, are_ref, aim_ref, bbre_ref, bbim_ref):
    lam_re = lre_ref[...]
    lam_im = lim_ref[...]
    dt = jnp.exp(ldt_ref[...])
    mag = jnp.exp(lam_re * dt)
    abar_re = mag * jnp.cos(lam_im * dt)
    abar_im = mag * jnp.sin(lam_im * dt)
    nr = abar_re - 1.0
    ni = abar_im
    den = lam_re * lam_re + lam_im * lam_im
    f_re = (nr * lam_re + ni * lam_im) / den
    f_im = (ni * lam_re - nr * lam_im) / den
    are_ref[...] = abar_re
    aim_ref[...] = abar_im
    b_re = bre_ref[...]
    b_im = bim_ref[...]
    bbre_ref[...] = f_re * b_re - f_im * b_im
    bbim_ref[...] = f_re * b_im + f_im * b_re


def _s5_disc(lam_re, lam_im, log_dt, b_re, b_im):
    n = SSM_GROUPS * SSM_STATE
    tm = 1024
    col = pl.BlockSpec((tm, 1), lambda i: (i, 0))
    mat = pl.BlockSpec((tm, SSM_GROUP_CH), lambda i: (i, 0))
    return pl.pallas_call(
        _s5_disc_kernel,
        grid=(n // tm,),
        in_specs=[col, col, col, mat, mat],
        out_specs=[col, col, mat, mat],
        out_shape=[jax.ShapeDtypeStruct((n, 1), F32), jax.ShapeDtypeStruct((n, 1), F32),
                   jax.ShapeDtypeStruct((n, SSM_GROUP_CH), F32), jax.ShapeDtypeStruct((n, SSM_GROUP_CH), F32)],
        compiler_params=_cparams(1),
        name="s5_discretise",
    )(lam_re, lam_im, log_dt, b_re, b_im)


def _s5_scan_kernel(u_ref, wb_ref, a_ref, wc_ref, d_ref, y_ref, lhs_ref, s_ref, yb_ref, h_ref):
    @pl.when(pl.program_id(1) == 0)
    def _():
        h_ref[...] = jnp.zeros_like(h_ref)
        lhs_ref[...] = jnp.zeros_like(lhs_ref)

    for b in range(BATCH):
        lhs_ref[0, pl.ds(b, SSM_TCHUNK, stride=SUBLANES), :] = u_ref[b, :, :SSM_GB_CH]
        lhs_ref[1, pl.ds(b + BATCH, SSM_TCHUNK, stride=SUBLANES), :] = u_ref[b, :, SSM_GB_CH:]
    half = SSM_TCHUNK * SUBLANES // 2
    wb = wb_ref[...]
    for r0 in (0, half):
        um = jnp.concatenate([lhs_ref[0, r0:r0 + half], lhs_ref[1, r0:r0 + half]], axis=1).astype(BF16)
        s_ref[r0:r0 + half] = jnp.dot(um, wb, preferred_element_type=F32)

    a_re = a_ref[:, :SSM_GB_ST]
    a_im = a_ref[:, SSM_GB_ST:]

    def step(t, carry):
        h_re, h_im = carry
        r0 = pl.multiple_of(t * SUBLANES, SUBLANES)
        n_re = a_re * h_re - a_im * h_im + s_ref[pl.ds(r0, SUBLANES), :SSM_GB_ST]
        n_im = a_re * h_im + a_im * h_re + s_ref[pl.ds(r0, SUBLANES), SSM_GB_ST:]
        s_ref[pl.ds(r0, SUBLANES), :SSM_GB_ST] = n_re
        s_ref[pl.ds(r0, SUBLANES), SSM_GB_ST:] = n_im
        return n_re, n_im

    h_re, h_im = lax.fori_loop(0, SSM_TCHUNK, step, (h_ref[:, :SSM_GB_ST], h_ref[:, SSM_GB_ST:]), unroll=8)
    h_ref[:, :SSM_GB_ST] = h_re
    h_ref[:, SSM_GB_ST:] = h_im

    wc = wc_ref[...]
    for r0 in (0, half):
        y8 = jnp.dot(s_ref[r0:r0 + half].astype(BF16), wc, preferred_element_type=F32)
        yb_ref[0, r0:r0 + half] = y8[:, :SSM_GB_CH]
        yb_ref[1, r0:r0 + half] = y8[:, SSM_GB_CH:]
    d = d_ref[...]
    for b in range(BATCH):
        y_ref[b, :, :SSM_GB_CH] = (yb_ref[0, pl.ds(b, SSM_TCHUNK, stride=SUBLANES), :]
                                   + d[:, :SSM_GB_CH] * u_ref[b, :, :SSM_GB_CH])
        y_ref[b, :, SSM_GB_CH:] = (yb_ref[1, pl.ds(b + BATCH, SSM_TCHUNK, stride=SUBLANES), :]
                                   + d[:, SSM_GB_CH:] * u_ref[b, :, SSM_GB_CH:])


def _s5_scan(z3, wb, a8, wc, d_row):
    rows = SSM_TCHUNK * SUBLANES
    width = 2 * SSM_GB_CH
    states = 2 * SSM_GB_ST
    return pl.pallas_call(
        _s5_scan_kernel,
        grid=(SSM_PAIRS, SEQ // SSM_TCHUNK),
        in_specs=[pl.BlockSpec((BATCH, SSM_TCHUNK, width), lambda g, c: (0, c, Z_SSM // width + g)),
                  pl.BlockSpec((None, width, states), lambda g, c: (g, 0, 0)),
                  pl.BlockSpec((None, SUBLANES, states), lambda g, c: (g, 0, 0)),
                  pl.BlockSpec((None, states, width), lambda g, c: (g, 0, 0)),
                  pl.BlockSpec((1, width), lambda g, c: (0, g))],
        out_specs=pl.BlockSpec((BATCH, SSM_TCHUNK, width), lambda g, c: (0, c, g)),
        out_shape=jax.ShapeDtypeStruct((BATCH, SEQ, SSM_WIDTH), F32),
        scratch_shapes=[pltpu.VMEM((2, rows, SSM_GB_CH), F32), pltpu.VMEM((rows, states), F32),
                        pltpu.VMEM((2, rows, SSM_GB_CH), F32), pltpu.VMEM((SUBLANES, states), F32)],
        compiler_params=_cparams(2),
        name="s5_scan",
    )(z3, wb, a8, wc, d_row)


def _gelu_tanh(x):
    return 0.5 * x * (1.0 + jnp.tanh(math.sqrt(2.0 / math.pi) * (x + 0.044715 * (x * x * x))))


def _glu_kernel(y_ref, w_ref, b_ref, g_ref, o_ref):
    g = _gelu_tanh(y_ref[...])
    gate = jax.nn.sigmoid(jnp.dot(g.astype(BF16), w_ref[...], preferred_element_type=F32) + b_ref[...])
    o_ref[...] = _rms(g * gate, g_ref[...]).astype(o_ref.dtype)


def _glu(y, w, b, gain):
    tm = 256
    return pl.pallas_call(
        _glu_kernel,
        grid=(TOKENS // tm,),
        in_specs=[pl.BlockSpec((tm, SSM_WIDTH), lambda i: (i, 0)),
                  pl.BlockSpec((SSM_WIDTH, SSM_WIDTH), lambda i: (0, 0)),
                  pl.BlockSpec((1, SSM_WIDTH), lambda i: (0, 0)),
                  pl.BlockSpec((1, SSM_WIDTH), lambda i: (0, 0))],
        out_specs=pl.BlockSpec((tm, SSM_WIDTH), lambda i: (i, 0)),
        out_shape=jax.ShapeDtypeStruct((TOKENS, SSM_WIDTH), BF16),
        compiler_params=_cparams(1),
        name="s5_glu",
    )(y, w, b, gain)


def _out_proj_kernel(om_ref, os_ref, g_ref, w_ref, x_ref, ga_ref, o_ref, lhs_ref):
    @pl.when(pl.program_id(1) == 0)
    def _():
        lhs_ref[:, :MLA_WIDTH] = _rms(om_ref[...], g_ref[...]).astype(BF16)
        lhs_ref[:, MLA_WIDTH:] = os_ref[...]

    o_ref[...] = x_ref[...] + ga_ref[...] * jnp.dot(lhs_ref[...], w_ref[...], preferred_element_type=F32)


def _out_proj(o_mla, o_ssm, gain, w, x2, mod3, gate_chunk):
    tm, tn = 512, 1024
    per_b = SEQ // tm
    nj = D_MODEL // tn
    return pl.pallas_call(
        _out_proj_kernel,
        grid=(TOKENS // tm, nj),
        in_specs=[pl.BlockSpec((tm, MLA_WIDTH), lambda i, j: (i, 0)),
                  pl.BlockSpec((tm, SSM_WIDTH), lambda i, j: (i, 0)),
                  pl.BlockSpec((1, MLA_WIDTH), lambda i, j: (0, 0)),
                  pl.BlockSpec((D_MODEL, tn), lambda i, j: (0, j)),
                  pl.BlockSpec((tm, tn), lambda i, j: (i, j)),
                  pl.BlockSpec((None, 1, tn), lambda i, j: (i // per_b, 0, gate_chunk * nj + j))],
        out_specs=pl.BlockSpec((tm, tn), lambda i, j: (i, j)),
        out_shape=jax.ShapeDtypeStruct((TOKENS, D_MODEL), F32),
        scratch_shapes=[pltpu.VMEM((tm, D_MODEL), BF16)],
        compiler_params=_cparams(2),
        name="out_proj",
    )(o_mla, o_ssm, gain, w, x2, mod3)


def _router_kernel(x_ref, g_ref, sh_ref, sc_ref, whi_ref, wlo_ref, br_ref, h_ref, eid_ref, wt_ref):
    h = _rms(x_ref[...], g_ref[...]) * (1.0 + sc_ref[...]) + sh_ref[...]
    h_ref[...] = h
    h_hi = h.astype(BF16)
    h_lo = (h - h_hi.astype(F32)).astype(BF16)

    whi = whi_ref[...]
    logits = (jnp.dot(h_hi, whi, preferred_element_type=F32)
              + jnp.dot(h_lo, whi, preferred_element_type=F32)
              + jnp.dot(h_hi, wlo_ref[...], preferred_element_type=F32)
              + br_ref[...])

    col = lax.broadcasted_iota(jnp.int32, logits.shape, 1)
    neg = -jnp.inf
    is_g = col < N_EGROUPS
    gl = jnp.where(is_g, logits, neg)
    gmax = jnp.max(gl, axis=1, keepdims=True)
    grp = jnp.min(jnp.where(gl == gmax, col, LANES), axis=1, keepdims=True)
    p_grp = 1.0 / jnp.sum(jnp.where(is_g, jnp.exp(logits - gmax), 0.0), axis=1, keepdims=True)

    ecol = col - N_EGROUPS
    in_grp = (ecol >= 0) & (ecol < N_EXPERTS) & ((ecol // EXPERTS_PER_GROUP) == grp)
    el = jnp.where(in_grp, logits, neg)
    v1 = jnp.max(el, axis=1, keepdims=True)
    i1 = jnp.min(jnp.where(el == v1, col, LANES), axis=1, keepdims=True)
    el2 = jnp.where(col == i1, neg, el)
    v2 = jnp.max(el2, axis=1, keepdims=True)
    i2 = jnp.min(jnp.where(el2 == v2, col, LANES), axis=1, keepdims=True)

    s2 = jnp.exp(v2 - v1)
    w1 = p_grp / (1.0 + s2)
    w2 = p_grp * s2 / (1.0 + s2)
    eid_ref[...] = jnp.where(col == 0, i1 - N_EGROUPS, jnp.where(col == 1, i2 - N_EGROUPS, 0))
    wt_ref[...] = jnp.where(col == 0, w1, jnp.where(col == 1, w2, 0.0))


def _router(x1, gain, mod3, shift_chunk, scale_chunk, w_hi, w_lo, b_row):
    tm = 256
    per_b = SEQ // tm
    return pl.pallas_call(
        _router_kernel,
        grid=(TOKENS // tm,),
        in_specs=[pl.BlockSpec((tm, D_MODEL), lambda i: (i, 0)),
                  pl.BlockSpec((1, D_MODEL), lambda i: (0, 0)),
                  pl.BlockSpec((None, 1, D_MODEL), lambda i: (i // per_b, 0, shift_chunk)),
                  pl.BlockSpec((None, 1, D_MODEL), lambda i: (i // per_b, 0, scale_chunk)),
                  pl.BlockSpec((D_MODEL, LANES), lambda i: (0, 0)),
                  pl.BlockSpec((D_MODEL, LANES), lambda i: (0, 0)),
                  pl.BlockSpec((1, LANES), lambda i: (0, 0))],
        out_specs=[pl.BlockSpec((tm, D_MODEL), lambda i: (i, 0)),
                   pl.BlockSpec((tm, LANES), lambda i: (i, 0)),
                   pl.BlockSpec((tm, LANES), lambda i: (i, 0))],
        out_shape=[jax.ShapeDtypeStruct((TOKENS, D_MODEL), F32),
                   jax.ShapeDtypeStruct((TOKENS, LANES), jnp.int32),
                   jax.ShapeDtypeStruct((TOKENS, LANES), F32)],
        compiler_params=_cparams(1),
        name="ffn_router",
    )(x1, gain, mod3, mod3, w_hi, w_lo, b_row)


def _moe_plan(eid):
    n_assign = TOKENS * TOP_K
    e_flat = eid.reshape(n_assign)
    onehot = (e_flat[:, None] == jnp.arange(N_EXPERTS, dtype=jnp.int32)[None, :]).astype(jnp.int32)
    csum = jnp.cumsum(onehot, axis=0)
    counts = csum[-1]
    rank = jnp.sum((csum - onehot) * onehot, axis=1)
    nblk_e = (counts + MOE_BLK - 1) // MOE_BLK
    bend = jnp.cumsum(nblk_e)
    bstart = bend - nblk_e
    dest = jnp.sum(onehot * bstart[None, :], axis=1) * MOE_BLK + rank
    a_id = jnp.arange(n_assign, dtype=jnp.int32)
    row_dst = jnp.zeros((MOE_ROWS,), jnp.int32).at[dest].set((a_id % TOP_K) * TOKENS + a_id // TOP_K,
                                                              unique_indices=True)
    blk = jnp.arange(MOE_NBLK, dtype=jnp.int32)
    n_used = bend[-1]
    blk_e = jnp.minimum(jnp.sum((bend[None, :] <= blk[:, None]).astype(jnp.int32), axis=1), N_EXPERTS - 1)
    blk_nvalid = jnp.clip(counts[blk_e] - (blk - bstart[blk_e]) * MOE_BLK, 0, MOE_BLK)
    blk_nvalid = jnp.where(blk < n_used, blk_nvalid, 0).astype(jnp.int32)
    last_e = blk_e[jnp.maximum(n_used - 1, 0)]
    blk_e = jnp.where(blk < n_used, blk_e, last_e).astype(jnp.int32)
    return row_dst, blk_e, blk_nvalid, n_used.reshape(1).astype(jnp.int32)


def _copy_groups(n_rows):
    return lax.shift_right_logical(n_rows + (DMA_UNROLL - 1), DMA_UNROLL.bit_length() - 1)


def _expert_up_kernel(dst_ref, be_ref, nv_ref, nu_ref, h_hbm, w1_ref, w3_ref, o_ref, xbuf, sem):
    i = pl.program_id(0)
    slot = i % 2

    def row_copy(src_row, dst_slot, dst_row):
        return pltpu.make_async_copy(h_hbm.at[pl.ds(src_row, 1)], xbuf.at[dst_slot, pl.ds(dst_row, 1)],
                                     sem.at[dst_slot])

    def issue(block, dst_slot):
        base = block * MOE_BLK

        def body(g, c):
            for u in range(DMA_UNROLL):
                r = g * DMA_UNROLL + u
                row_copy(dst_ref[base + r] & (TOKENS - 1), dst_slot, r).start()
            return c

        lax.fori_loop(0, _copy_groups(nv_ref[block]), body, 0)

    @pl.when(i == 0)
    def _():
        xbuf[...] = jnp.zeros_like(xbuf)
        issue(0, 0)

    @pl.when(i + 1 < pl.num_programs(0))
    def _():
        issue(i + 1, 1 - slot)

    def wait_body(g, c):
        for u in range(DMA_UNROLL):
            row_copy(0, slot, g * DMA_UNROLL + u).wait()
        return c

    lax.fori_loop(0, _copy_groups(nv_ref[i]), wait_body, 0)

    @pl.when(i < nu_ref[0])
    def _():
        x = xbuf[slot].astype(BF16)
        a = jnp.dot(x, w1_ref[...].astype(BF16), preferred_element_type=F32)
        b = jnp.dot(x, w3_ref[...].astype(BF16), preferred_element_type=F32)
        o_ref[...] = (a * jax.nn.sigmoid(a) * b).astype(o_ref.dtype)

    @pl.when(i >= nu_ref[0])
    def _():
        o_ref[...] = jnp.zeros_like(o_ref)


def _expert_up(row_dst, blk_e, blk_nvalid, n_used, h2, w1, w3):
    wspec = pl.BlockSpec((None, D_MODEL, D_EXPERT), lambda i, tok, be, ng, nu: (be[i], 0, 0))
    return pl.pallas_call(
        _expert_up_kernel,
        grid_spec=pltpu.PrefetchScalarGridSpec(
            num_scalar_prefetch=4,
            grid=(MOE_NBLK,),
            in_specs=[pl.BlockSpec(memory_space=pl.ANY), wspec, wspec],
            out_specs=pl.BlockSpec((MOE_BLK, D_EXPERT), lambda i, tok, be, ng, nu: (i, 0)),
            scratch_shapes=[pltpu.VMEM((2, MOE_BLK, D_MODEL), F32),
                            pltpu.SemaphoreType.DMA((2,))]),
        out_shape=jax.ShapeDtypeStruct((MOE_ROWS, D_EXPERT), BF16),
        compiler_params=_cparams(1, 60 * 1024 * 1024),
        name="moe_expert_up",
    )(row_dst, blk_e, blk_nvalid, n_used, h2, w1, w3)


def _expert_down_kernel(dst_ref, be_ref, nv_ref, nu_ref, h_ref, w_ref, y_hbm, ybuf, sem):
    i = pl.program_id(0)
    slot = i % 2

    def row_copy(src_row, dst_row, src_slot):
        return pltpu.make_async_copy(ybuf.at[src_slot, pl.ds(src_row, 1)], y_hbm.at[pl.ds(dst_row, 1)],
                                     sem.at[src_slot])

    def drain(block, src_slot):
        def body(r, c):
            row_copy(r, 0, src_slot).wait()
            return c

        lax.fori_loop(0, nv_ref[block], body, 0)

    @pl.when(i >= 2)
    def _():
        drain(i - 2, slot)

    @pl.when(i < nu_ref[0])
    def _():
        ybuf[slot] = jnp.dot(h_ref[...], w_ref[...].astype(BF16), preferred_element_type=F32)
        base = i * MOE_BLK
        nv = nv_ref[i]
        full = lax.shift_right_logical(nv, DMA_UNROLL.bit_length() - 1)

        def body(g, c):
            for u in range(DMA_UNROLL):
                r = g * DMA_UNROLL + u
                row_copy(r, dst_ref[base + r], slot).start()
            return c

        lax.fori_loop(0, full, body, 0)

        def tail(r, c):
            row_copy(r, dst_ref[base + r], slot).start()
            return c

        lax.fori_loop(full * DMA_UNROLL, nv, tail, 0)

    @pl.when(i == pl.num_programs(0) - 1)
    def _():
        drain(i - 1, 1 - slot)
        drain(i, slot)


def _expert_down(row_dst, blk_e, blk_nvalid, n_used, hmid, w2):
    return pl.pallas_call(
        _expert_down_kernel,
        grid_spec=pltpu.PrefetchScalarGridSpec(
            num_scalar_prefetch=4,
            grid=(MOE_NBLK,),
            in_specs=[pl.BlockSpec((MOE_BLK, D_EXPERT), lambda i, dst, be, nv, nu: (i, 0)),
                      pl.BlockSpec((None, D_EXPERT, D_MODEL), lambda i, dst, be, nv, nu: (be[i], 0, 0))],
            out_specs=pl.BlockSpec(memory_space=pl.ANY),
            scratch_shapes=[pltpu.VMEM((2, MOE_BLK, D_MODEL), F32),
                            pltpu.SemaphoreType.DMA((2,))]),
        out_shape=jax.ShapeDtypeStruct((TOP_K * TOKENS, D_MODEL), F32),
        compiler_params=_cparams(1),
        name="moe_expert_down",
    )(row_dst, blk_e, blk_nvalid, n_used, hmid, w2)


def _combine_kernel(y0_ref, y1_ref, x_ref, wt_ref, gf_ref, fg_ref, o_ref):
    wt = wt_ref[...]
    moe = wt[:, 0:1] * y0_ref[...] + wt[:, 1:2] * y1_ref[...]
    o_ref[...] = _rms(x_ref[...] + gf_ref[...] * moe, fg_ref[...])


def _combine(ys, x1, wt, mod3, gate_chunk, final_gain):
    per_b = SEQ // CMB_TM
    nblk = TOKENS // CMB_TM
    return pl.pallas_call(
        _combine_kernel,
        grid=(nblk,),
        in_specs=[pl.BlockSpec((CMB_TM, D_MODEL), lambda i: (i, 0)),
                  pl.BlockSpec((CMB_TM, D_MODEL), lambda i: (i + nblk, 0)),
                  pl.BlockSpec((CMB_TM, D_MODEL), lambda i: (i, 0)),
                  pl.BlockSpec((CMB_TM, LANES), lambda i: (i, 0)),
                  pl.BlockSpec((None, 1, D_MODEL), lambda i: (i // per_b, 0, gate_chunk)),
                  pl.BlockSpec((1, D_MODEL), lambda i: (0, 0))],
        out_specs=pl.BlockSpec((CMB_TM, D_MODEL), lambda i: (i, 0)),
        out_shape=jax.ShapeDtypeStruct((TOKENS, D_MODEL), F32),
        compiler_params=_cparams(1),
        name="moe_combine_final_norm",
    )(ys, ys, x1, wt, mod3, final_gain)


def _rotate_half_cols(w):
    half = w.shape[-1] // 2
    return jnp.concatenate([-w[..., half:], w[..., :half]], axis=-1)


def _block_diag_in(bbar):
    bb = bbar.reshape(SSM_GROUPS // SSM_GB, SSM_GB, SSM_STATE, SSM_GROUP_CH)
    w = jnp.einsum('bgpn,gh->bgnhp', bb, jnp.eye(SSM_GB, dtype=bbar.dtype))
    return w.reshape(SSM_GROUPS // SSM_GB, SSM_GB_CH, SSM_GB_ST)


def _block_diag_out(cmat):
    cc = cmat.reshape(SSM_GROUPS // SSM_GB, SSM_GB, SSM_GROUP_CH, SSM_STATE)
    w = jnp.einsum('bgnp,gh->bgphn', cc, jnp.eye(SSM_GB, dtype=cmat.dtype))
    return w.reshape(SSM_GROUPS // SSM_GB, SSM_GB_ST, SSM_GB_CH)


def kernel(x, c, positions, w_ada, b_ada, norm_mix_gain, w_in, q_lat_gain, w_uq, kv_lat_gain, w_ukv,
           ssm_lam_re, ssm_lam_im, ssm_log_dt, ssm_b_re, ssm_b_im, ssm_c_re, ssm_c_im, ssm_d,
           w_glu, b_glu, mla_out_gain, ssm_out_gain, w_out, norm_ffn_gain,
           w_group_router, b_group_router, w_expert_router, b_expert_router,
           w1_experts, w3_experts, w2_experts, final_gain):
    x2 = x.reshape(TOKENS, D_MODEL)

    c8 = jnp.zeros((SUBLANES, D_MODEL), F32).at[:BATCH].set(c)
    mod = _ada(c8, w_ada[0], b_ada[0].reshape(1, 6 * D_MODEL))
    mod3 = mod.reshape(SUBLANES, 1, 6 * D_MODEL)

    z = _in_proj(x2, norm_mix_gain[0].reshape(1, D_MODEL), mod3, 0, 1, _wz_prep(w_in[0]))

    cs = _rope_table(positions.reshape(TOKENS, 1))

    wq = w_uq[0].reshape(Q_RANK, MLA_HEADS, QK_HEAD)
    wq_heads = jnp.concatenate([wq, _rotate_half_cols(wq[:, :, QK_NOPE:])], axis=-1)
    wq_heads = wq_heads.astype(BF16).reshape(Q_RANK, MLA_HEADS * 2 * LANES)
    q = _q_proj(z, q_lat_gain[0].reshape(1, Q_RANK), wq_heads, cs)
    wkv = w_ukv[0].reshape(KV_RANK, MLA_HEADS, QK_NOPE + V_HEAD)
    wk_heads = wkv[:, :, :QK_NOPE].astype(BF16).reshape(KV_RANK, MLA_HEADS * QK_NOPE)
    wvt_heads = wkv[:, :, QK_NOPE:].transpose(1, 2, 0).astype(BF16)
    k, vt = _kv_proj(z, kv_lat_gain[0].reshape(1, KV_RANK), wk_heads, wvt_heads, cs)
    o_mla = _attention(q, k, vt)

    n_st = SSM_GROUPS * SSM_STATE
    abar_re, abar_im, bbar_re, bbar_im = _s5_disc(
        ssm_lam_re[0].reshape(n_st, 1), ssm_lam_im[0].reshape(n_st, 1),
        jnp.repeat(ssm_log_dt[0], SSM_STATE).reshape(n_st, 1),
        ssm_b_re[0].reshape(n_st, SSM_GROUP_CH), ssm_b_im[0].reshape(n_st, SSM_GROUP_CH))
    n_gb = SSM_GROUPS // SSM_GB
    wb = jnp.concatenate([_block_diag_in(bbar_re), _block_diag_in(bbar_im)], axis=-1)
    wb = wb.reshape(SSM_PAIRS, 2 * SSM_GB_CH, 2 * SSM_GB_ST).astype(BF16)
    wc = jnp.concatenate([_block_diag_out(ssm_c_re[0]), -_block_diag_out(ssm_c_im[0])], axis=1)
    wc = wc.reshape(SSM_PAIRS, 2, 2 * SSM_GB_ST, SSM_GB_CH).transpose(0, 2, 1, 3)
    wc = wc.reshape(SSM_PAIRS, 2 * SSM_GB_ST, 2 * SSM_GB_CH).astype(BF16)
    a16 = jnp.concatenate([abar_re.reshape(n_gb, SSM_GB_ST), abar_im.reshape(n_gb, SSM_GB_ST)], axis=-1)
    a8 = jnp.repeat(a16.reshape(SSM_PAIRS, 2, 2 * SSM_GB_ST), BATCH, axis=1)

    y = _s5_scan(z.reshape(BATCH, SEQ, Z_COLS), wb, a8, wc, ssm_d[0].reshape(1, SSM_WIDTH))
    o_ssm = _glu(y.reshape(TOKENS, SSM_WIDTH), w_glu[0].astype(BF16), b_glu[0].reshape(1, SSM_WIDTH),
                 ssm_out_gain[0].reshape(1, SSM_WIDTH))

    x1 = _out_proj(o_mla, o_ssm, mla_out_gain[0].reshape(1, MLA_WIDTH), w_out[0].astype(BF16), x2, mod3, 2)

    w_r = jnp.concatenate([w_group_router[0], w_expert_router[0],
                           jnp.zeros((D_MODEL, LANES - N_EGROUPS - N_EXPERTS), F32)], axis=1)
    w_r_hi = w_r.astype(BF16)
    w_r_lo = (w_r - w_r_hi.astype(F32)).astype(BF16)
    b_r = jnp.concatenate([b_group_router[0], b_expert_router[0],
                           jnp.zeros((LANES - N_EGROUPS - N_EXPERTS,), F32)]).reshape(1, LANES)
    h2, eid, wt = _router(x1, norm_ffn_gain[0].reshape(1, D_MODEL), mod3, 3, 4, w_r_hi, w_r_lo, b_r)

    row_dst, blk_e, blk_nvalid, n_used = _moe_plan(eid[:, :TOP_K])
    hmid = _expert_up(row_dst, blk_e, blk_nvalid, n_used, h2, w1_experts[0], w3_experts[0])
    ys = _expert_down(row_dst, blk_e, blk_nvalid, n_used, hmid, w2_experts[0])
    out = _combine(ys, x1, wt, mod3, 5, final_gain.reshape(1, D_MODEL))
    return out.reshape(BATCH, SEQ, D_MODEL)
```

```python
import math

import jax
import jax.numpy as jnp
from jax import lax
from jax.experimental import pallas as pl
from jax.experimental.pallas import tpu as pltpu

F32 = jnp.float32
BF16 = jnp.bfloat16

D_MODEL = 4096
BATCH = 4
SEQ = 2048
TOKENS = BATCH * SEQ
CHUNK = 64
EPS = 1e-6

MLA_HEADS = 16
QK_NOPE = 128
QK_ROPE = 64
QK_HEAD = QK_NOPE + QK_ROPE
V_HEAD = 128
Q_RANK = 768
KV_RANK = 512
ROPE_THETA = 10000.0
MLA_WIDTH = MLA_HEADS * V_HEAD

SSM_WIDTH = D_MODEL - MLA_WIDTH
SSM_GROUP_CH = 16
SSM_GROUPS = SSM_WIDTH // SSM_GROUP_CH
SSM_STATE = 64

N_EGROUPS = 8
EXPERTS_PER_GROUP = 8
N_EXPERTS = N_EGROUPS * EXPERTS_PER_GROUP
TOP_K = 2
D_EXPERT = 512

LANES = 128
SUBLANES = 8
VMEM_LIMIT = 56 * 1024 * 1024

Z_Q = 0
Z_SSM = Q_RANK
Z_KV = Z_SSM + SSM_WIDTH
Z_KR = Z_KV + KV_RANK
Z_COLS = Z_KR + 2 * QK_ROPE

ATT_BLK = 256
ATT_NBLK = SEQ // ATT_BLK

SSM_GB = 8
SSM_GB_CH = SSM_GB * SSM_GROUP_CH
SSM_GB_ST = SSM_GB * SSM_STATE
SSM_PAIRS = SSM_GROUPS // (2 * SSM_GB)
SSM_TCHUNK = 256

MOE_BLK = 256
MOE_NBLK = (TOKENS * TOP_K + N_EXPERTS * (MOE_BLK - 1)) // MOE_BLK + 1
MOE_ROWS = MOE_NBLK * MOE_BLK
DMA_UNROLL = 8
CMB_TM = 128


def _cparams(n_axes, vmem=VMEM_LIMIT):
    return pltpu.CompilerParams(dimension_semantics=("arbitrary",) * n_axes, vmem_limit_bytes=vmem)


def _ada_kernel(c_ref, w_ref, b_ref, o_ref):
    c = c_ref[...]
    ca = (c * jax.nn.sigmoid(c)).astype(BF16)
    o_ref[...] = jnp.dot(ca, w_ref[...].astype(BF16), preferred_element_type=F32) + b_ref[...]


def _ada(c8, w, b):
    n = w.shape[1]
    tn = 512
    return pl.pallas_call(
        _ada_kernel,
        grid=(n // tn,),
        in_specs=[pl.BlockSpec((SUBLANES, D_MODEL), lambda j: (0, 0)),
                  pl.BlockSpec((D_MODEL, tn), lambda j: (0, j)),
                  pl.BlockSpec((1, tn), lambda j: (0, j))],
        out_specs=pl.BlockSpec((SUBLANES, tn), lambda j: (0, j)),
        out_shape=jax.ShapeDtypeStruct((SUBLANES, n), F32),
        compiler_params=_cparams(1),
        name="ada_mod",
    )(c8, w, b)


def _rms(x, gain):
    return x * lax.rsqrt(jnp.mean(x * x, axis=-1, keepdims=True) + EPS) * gain


def _in_proj_kernel(x_ref, g_ref, sh_ref, sc_ref, w_ref, o_ref, h_ref):
    @pl.when(pl.program_id(1) == 0)
    def _():
        y = _rms(x_ref[...], g_ref[...])
        h_ref[...] = (y * (1.0 + sc_ref[...]) + sh_ref[...]).astype(h_ref.dtype)

    o_ref[...] = lax.dot_general(h_ref[...], w_ref[...], (((1,), (1,)), ((), ())), preferred_element_type=F32)


def _in_proj(x2, gain, mod3, shift_chunk, scale_chunk, w_t):
    tm = 512
    n = w_t.shape[0]
    tn = n // 3
    per_b = SEQ // tm
    return pl.pallas_call(
        _in_proj_kernel,
        grid=(TOKENS // tm, n // tn),
        in_specs=[pl.BlockSpec((tm, D_MODEL), lambda i, j: (i, 0)),
                  pl.BlockSpec((1, D_MODEL), lambda i, j: (0, 0)),
                  pl.BlockSpec((None, 1, D_MODEL), lambda i, j: (i // per_b, 0, shift_chunk)),
                  pl.BlockSpec((None, 1, D_MODEL), lambda i, j: (i // per_b, 0, scale_chunk)),
                  pl.BlockSpec((tn, D_MODEL), lambda i, j: (j, 0))],
        out_specs=pl.BlockSpec((tm, tn), lambda i, j: (i, j)),
        out_shape=jax.ShapeDtypeStruct((TOKENS, n), F32),
        scratch_shapes=[pltpu.VMEM((tm, D_MODEL), BF16)],
        compiler_params=_cparams(2),
        name="in_proj",
    )(x2, gain, mod3, mod3, w_t)


W_IN_KV = Q_RANK
W_IN_KR = Q_RANK + KV_RANK
W_IN_SSM = W_IN_KR + QK_ROPE
W_IN_COLS = W_IN_SSM + SSM_WIDTH


def _wz_kernel(w_ref, o_ref):
    o_ref[Z_Q:Z_Q + Q_RANK] = w_ref[:W_IN_KV].astype(o_ref.dtype)
    o_ref[Z_SSM:Z_SSM + SSM_WIDTH] = w_ref[W_IN_SSM:].astype(o_ref.dtype)
    o_ref[Z_KV:Z_KV + KV_RANK] = w_ref[W_IN_KV:W_IN_KR].astype(o_ref.dtype)
    half = QK_ROPE // 2
    o_ref[Z_KR:Z_KR + QK_ROPE] = w_ref[W_IN_KR:W_IN_SSM].astype(o_ref.dtype)
    o_ref[Z_KR + QK_ROPE:Z_KR + QK_ROPE + half] = (-w_ref[W_IN_KR + half:W_IN_SSM]).astype(o_ref.dtype)
    o_ref[Z_KR + QK_ROPE + half:] = w_ref[W_IN_KR:W_IN_KR + half].astype(o_ref.dtype)


def _wz_prep(w_in_t):
    tk = 512
    return pl.pallas_call(
        _wz_kernel,
        grid=(D_MODEL // tk,),
        in_specs=[pl.BlockSpec((W_IN_COLS, tk), lambda i: (0, i))],
        out_specs=pl.BlockSpec((Z_COLS, tk), lambda i: (0, i)),
        out_shape=jax.ShapeDtypeStruct((Z_COLS, D_MODEL), BF16),
        compiler_params=_cparams(1),
        name="in_proj_weight_prep",
    )(w_in_t)


def _rope_table_kernel(pos_ref, o_ref):
    lane = lax.broadcasted_iota(jnp.int32, (1, LANES), 1)
    pair = (lane % (QK_ROPE // 2)).astype(F32)
    inv_freq = jnp.exp(-math.log(ROPE_THETA) * (2.0 * pair) / QK_ROPE)
    ang = pos_ref[...].astype(F32) * inv_freq
    o_ref[...] = jnp.where(lane < QK_ROPE, jnp.cos(ang), jnp.sin(ang))


def _rope_table(pos_col):
    tm = 1024
    return pl.pallas_call(
        _rope_table_kernel,
        grid=(TOKENS // tm,),
        in_specs=[pl.BlockSpec((tm, 1), lambda i: (i, 0))],
        out_specs=pl.BlockSpec((tm, LANES), lambda i: (i, 0)),
        out_shape=jax.ShapeDtypeStruct((TOKENS, LANES), F32),
        compiler_params=_cparams(1),
        name="rope_table",
    )(pos_col)


def _rope_pair(t, cs):
    u = t * cs
    return u + pltpu.roll(u, QK_ROPE, axis=1)


def _q_proj_kernel(ql_ref, g_ref, w_ref, cs_ref, o_ref):
    hn = _rms(ql_ref[...], g_ref[...]).astype(BF16)
    cs = cs_ref[...]
    for h in range(MLA_HEADS):
        r = jnp.dot(hn, w_ref[:, h * 2 * LANES:(h + 1) * 2 * LANES], preferred_element_type=F32)
        o_ref[h, :, :QK_NOPE] = r[:, :QK_NOPE].astype(o_ref.dtype)
        o_ref[h, :, QK_NOPE:] = _rope_pair(r[:, QK_NOPE:], cs)[:, :QK_ROPE].astype(o_ref.dtype)


def _q_proj(z, gain, w_heads, cs):
    tm = ATT_BLK
    per_b = SEQ // tm
    return pl.pallas_call(
        _q_proj_kernel,
        grid=(TOKENS // tm,),
        in_specs=[pl.BlockSpec((tm, Q_RANK), lambda i: (i, Z_Q // Q_RANK)),
                  pl.BlockSpec((1, Q_RANK), lambda i: (0, 0)),
                  pl.BlockSpec((Q_RANK, MLA_HEADS * 2 * LANES), lambda i: (0, 0)),
                  pl.BlockSpec((tm, LANES), lambda i: (i, 0))],
        out_specs=pl.BlockSpec((None, MLA_HEADS, tm, QK_HEAD), lambda i: (i // per_b, 0, i % per_b, 0)),
        out_shape=jax.ShapeDtypeStruct((BATCH, MLA_HEADS, SEQ, QK_HEAD), BF16),
        compiler_params=_cparams(1),
        name="mla_q_proj",
    )(z, gain, w_heads, cs)


def _kv_proj_kernel(kva_ref, kvb_ref, kr_ref, g_ref, wk_ref, wvt_ref, cs_ref, k_ref, vt_ref):
    kvl = jnp.concatenate([kva_ref[...], kvb_ref[...]], axis=1)
    hn = _rms(kvl, g_ref[...]).astype(BF16)
    kr = _rope_pair(kr_ref[...], cs_ref[...])[:, :QK_ROPE].astype(k_ref.dtype)
    for h in range(MLA_HEADS):
        k_ref[h, :, :QK_NOPE] = jnp.dot(hn, wk_ref[:, h * QK_NOPE:(h + 1) * QK_NOPE],
                                        preferred_element_type=F32).astype(k_ref.dtype)
        k_ref[h, :, QK_NOPE:] = kr
        vt_ref[h] = lax.dot_general(wvt_ref[h], hn, (((1,), (1,)), ((), ())),
                                    preferred_element_type=F32).astype(vt_ref.dtype)


def _kv_proj(z, gain, wk_heads, wvt_heads, cs):
    tm = ATT_BLK
    per_b = SEQ // tm
    half = KV_RANK // 2
    return pl.pallas_call(
        _kv_proj_kernel,
        grid=(TOKENS // tm,),
        in_specs=[pl.BlockSpec((tm, half), lambda i: (i, Z_KV // half)),
                  pl.BlockSpec((tm, half), lambda i: (i, Z_KV // half + 1)),
                  pl.BlockSpec((tm, LANES), lambda i: (i, Z_KR // LANES)),
                  pl.BlockSpec((1, KV_RANK), lambda i: (0, 0)),
                  pl.BlockSpec((KV_RANK, MLA_HEADS * QK_NOPE), lambda i: (0, 0)),
                  pl.BlockSpec((MLA_HEADS, V_HEAD, KV_RANK), lambda i: (0, 0, 0)),
                  pl.BlockSpec((tm, LANES), lambda i: (i, 0))],
        out_specs=[pl.BlockSpec((None, MLA_HEADS, tm, QK_HEAD), lambda i: (i // per_b, 0, i % per_b, 0)),
                   pl.BlockSpec((None, MLA_HEADS, None, V_HEAD, tm), lambda i: (i // per_b, 0, i % per_b, 0, 0))],
        out_shape=[jax.ShapeDtypeStruct((BATCH, MLA_HEADS, SEQ, QK_HEAD), BF16),
                   jax.ShapeDtypeStruct((BATCH, MLA_HEADS, ATT_NBLK, V_HEAD, ATT_BLK), BF16)],
        compiler_params=_cparams(1),
        name="mla_kv_proj",
    )(z, z, z, gain, wk_heads, wvt_heads, cs)


ATT_HEADS = 4


def _attn_kernel(q_ref, k_ref, vt_ref, o_ref, acc_ref):
    log2_scale = (QK_HEAD ** -0.5) * math.log2(math.e)
    key_chunk = lax.broadcasted_iota(jnp.int32, (ATT_BLK, ATT_BLK), 0) // CHUNK
    qry_chunk = lax.broadcasted_iota(jnp.int32, (ATT_BLK, ATT_BLK), 1) // CHUNK
    diag_mask = key_chunk <= qry_chunk

    def q_body(qi, carry):
        q0 = pl.multiple_of(qi * ATT_BLK, ATT_BLK)

        def kv_block(j, stats, n_blk, masked):
            keys = n_blk * ATT_BLK
            k0 = pl.multiple_of(j * ATT_BLK, ATT_BLK)
            sts = [lax.dot_general(k_ref[h, pl.ds(k0, keys), :], q_ref[h, pl.ds(q0, ATT_BLK), :],
                                   (((1,), (1,)), ((), ())), preferred_element_type=F32)
                   for h in range(ATT_HEADS)]
            new_stats, ps, alphas = [], [], []
            for h in range(ATT_HEADS):
                m, l = stats[h]
                st = sts[h] * log2_scale
                if masked:
                    mask = diag_mask if n_blk == 1 else jnp.concatenate(
                        [jnp.ones(((n_blk - 1) * ATT_BLK, ATT_BLK), jnp.bool_), diag_mask], axis=0)
                    st = jnp.where(mask, st, -jnp.inf)
                m_new = jnp.maximum(m, jnp.max(st, axis=0, keepdims=True))
                alpha = jnp.exp2(m - m_new)
                p = jnp.exp2(st - m_new)
                new_stats.append((m_new, alpha * l + jnp.sum(p, axis=0, keepdims=True)))
                ps.append(p.astype(BF16))
                alphas.append(alpha)
            for h in range(ATT_HEADS):
                pv = jnp.dot(vt_ref[h, j], ps[h][:ATT_BLK], preferred_element_type=F32)
                for b in range(1, n_blk):
                    pv = pv + jnp.dot(vt_ref[h, j + b], ps[h][b * ATT_BLK:(b + 1) * ATT_BLK],
                                      preferred_element_type=F32)
                acc_ref[h] = alphas[h] * acc_ref[h] + pv
            return tuple(new_stats)

        acc_ref[...] = jnp.zeros_like(acc_ref)
        stats = tuple((jnp.full((1, ATT_BLK), -jnp.inf, F32), jnp.zeros((1, ATT_BLK), F32))
                      for _ in range(ATT_HEADS))
        n_pairs = lax.shift_right_logical(qi, 1)
        stats = lax.fori_loop(0, n_pairs, lambda jj, st: kv_block(2 * jj, st, 2, False), stats)
        stats = lax.cond(qi % 2 == 1, lambda st: kv_block(qi - 1, st, 2, True),
                         lambda st: kv_block(qi, st, 1, True), stats)
        for h in range(ATT_HEADS):
            o_ref[pl.ds(q0, ATT_BLK), h * V_HEAD:(h + 1) * V_HEAD] = (acc_ref[h] / stats[h][1]).T
        return carry

    lax.fori_loop(0, ATT_NBLK, q_body, 0)


def _attention(q, k, vt):
    return pl.pallas_call(
        _attn_kernel,
        grid=(BATCH, MLA_HEADS // ATT_HEADS),
        in_specs=[pl.BlockSpec((None, ATT_HEADS, SEQ, QK_HEAD), lambda b, h: (b, h, 0, 0)),
                  pl.BlockSpec((None, ATT_HEADS, SEQ, QK_HEAD), lambda b, h: (b, h, 0, 0)),
                  pl.BlockSpec((None, ATT_HEADS, ATT_NBLK, V_HEAD, ATT_BLK), lambda b, h: (b, h, 0, 0, 0))],
        out_specs=pl.BlockSpec((SEQ, ATT_HEADS * V_HEAD), lambda b, h: (b, h)),
        out_shape=jax.ShapeDtypeStruct((TOKENS, MLA_WIDTH), F32),
        scratch_shapes=[pltpu.VMEM((ATT_HEADS, V_HEAD, ATT_BLK), F32)],
        compiler_params=_cparams(2),
        name="mla_attention",
    )(q, k, vt)


def _s5_disc_kernel(lre_ref, lim_ref, ldt_ref, bre_ref, bim_ref, are_ref, aim_ref, bbre_ref, bbim_ref):
    lam_re = lre_ref[...]
    lam_im = lim_ref[...]
    dt = jnp.exp(ldt_ref[...])
    mag = jnp.exp(lam_re * dt)
    abar_re = mag * jnp.cos(lam_im * dt)
    abar_im = mag * jnp.sin(lam_im * dt)
    nr = abar_re - 1.0
    ni = abar_im
    den = lam_re * lam_re + lam_im * lam_im
    f_re = (nr * lam_re + ni * lam_im) / den
    f_im = (ni * lam_re - nr * lam_im) / den
    are_ref[...] = abar_re
    aim_ref[...] = abar_im
    b_re = bre_ref[...]
    b_im = bim_ref[...]
    bbre_ref[...] = f_re * b_re - f_im * b_im
    bbim_ref[...] = f_re * b_im + f_im * b_re


def _s5_disc(lam_re, lam_im, log_dt, b_re, b_im):
    n = SSM_GROUPS * SSM_STATE
    tm = 1024
    col = pl.BlockSpec((tm, 1), lambda i: (i, 0))
    mat = pl.BlockSpec((tm, SSM_GROUP_CH), lambda i: (i, 0))
    return pl.pallas_call(
        _s5_disc_kernel,
        grid=(n // tm,),
        in_specs=[col, col, col, mat, mat],
        out_specs=[col, col, mat, mat],
        out_shape=[jax.ShapeDtypeStruct((n, 1), F32), jax.ShapeDtypeStruct((n, 1), F32),
                   jax.ShapeDtypeStruct((n, SSM_GROUP_CH), F32), jax.ShapeDtypeStruct((n, SSM_GROUP_CH), F32)],
        compiler_params=_cparams(1),
        name="s5_discretise",
    )(lam_re, lam_im, log_dt, b_re, b_im)


def _s5_scan_kernel(u_ref, wb_ref, a_ref, wc_ref, d_ref, y_ref, lhs_ref, s_ref, yb_ref, h_ref):
    @pl.when(pl.program_id(1) == 0)
    def _():
        h_ref[...] = jnp.zeros_like(h_ref)
        lhs_ref[...] = jnp.zeros_like(lhs_ref)

    for b in range(BATCH):
        lhs_ref[0, pl.ds(b, SSM_TCHUNK, stride=SUBLANES), :] = u_ref[b, :, :SSM_GB_CH]
        lhs_ref[1, pl.ds(b + BATCH, SSM_TCHUNK, stride=SUBLANES), :] = u_ref[b, :, SSM_GB_CH:]
    half = SSM_TCHUNK * SUBLANES // 2
    wb = wb_ref[...]
    for r0 in (0, half):
        um = jnp.concatenate([lhs_ref[0, r0:r0 + half], lhs_ref[1, r0:r0 + half]], axis=1).astype(BF16)
        s_ref[r0:r0 + half] = jnp.dot(um, wb, preferred_element_type=F32)

    a_re = a_ref[:, :SSM_GB_ST]
    a_im = a_ref[:, SSM_GB_ST:]

    def step(t, carry):
        h_re, h_im = carry
        r0 = pl.multiple_of(t * SUBLANES, SUBLANES)
        n_re = a_re * h_re - a_im * h_im + s_ref[pl.ds(r0, SUBLANES), :SSM_GB_ST]
        n_im = a_re * h_im + a_im * h_re + s_ref[pl.ds(r0, SUBLANES), SSM_GB_ST:]
        s_ref[pl.ds(r0, SUBLANES), :SSM_GB_ST] = n_re
        s_ref[pl.ds(r0, SUBLANES), SSM_GB_ST:] = n_im
        return n_re, n_im

    h_re, h_im = lax.fori_loop(0, SSM_TCHUNK, step, (h_ref[:, :SSM_GB_ST], h_ref[:, SSM_GB_ST:]), unroll=8)
    h_ref[:, :SSM_GB_ST] = h_re
    h_ref[:, SSM_GB_ST:] = h_im

    wc = wc_ref[...]
    for r0 in (0, half):
        y8 = jnp.dot(s_ref[r0:r0 + half].astype(BF16), wc, preferred_element_type=F32)
        yb_ref[0, r0:r0 + half] = y8[:, :SSM_GB_CH]
        yb_ref[1, r0:r0 + half] = y8[:, SSM_GB_CH:]
    d = d_ref[...]
    for b in range(BATCH):
        y_ref[b, :, :SSM_GB_CH] = (yb_ref[0, pl.ds(b, SSM_TCHUNK, stride=SUBLANES), :]
                                   + d[:, :SSM_GB_CH] * u_ref[b, :, :SSM_GB_CH])
        y_ref[b, :, SSM_GB_CH:] = (yb_ref[1, pl.ds(b + BATCH, SSM_TCHUNK, stride=SUBLANES), :]
                                   + d[:, SSM_GB_CH:] * u_ref[b, :, SSM_GB_CH:])


def _s5_scan(z3, wb, a8, wc, d_row):
    rows = SSM_TCHUNK * SUBLANES
    width = 2 * SSM_GB_CH
    states = 2 * SSM_GB_ST
    return pl.pallas_call(
        _s5_scan_kernel,
        grid=(SSM_PAIRS, SEQ // SSM_TCHUNK),
        in_specs=[pl.BlockSpec((BATCH, SSM_TCHUNK, width), lambda g, c: (0, c, Z_SSM // width + g)),
                  pl.BlockSpec((None, width, states), lambda g, c: (g, 0, 0)),
                  pl.BlockSpec((None, SUBLANES, states), lambda g, c: (g, 0, 0)),
                  pl.BlockSpec((None, states, width), lambda g, c: (g, 0, 0)),
                  pl.BlockSpec((1, width), lambda g, c: (0, g))],
        out_specs=pl.BlockSpec((BATCH, SSM_TCHUNK, width), lambda g, c: (0, c, g)),
        out_shape=jax.ShapeDtypeStruct((BATCH, SEQ, SSM_WIDTH), F32),
        scratch_shapes=[pltpu.VMEM((2, rows, SSM_GB_CH), F32), pltpu.VMEM((rows, states), F32),
                        pltpu.VMEM((2, rows, SSM_GB_CH), F32), pltpu.VMEM((SUBLANES, states), F32)],
        compiler_params=_cparams(2),
        name="s5_scan",
    )(z3, wb, a8, wc, d_row)


def _gelu_tanh(x):
    return 0.5 * x * (1.0 + jnp.tanh(math.sqrt(2.0 / math.pi) * (x + 0.044715 * (x * x * x))))


def _glu_kernel(y_ref, w_ref, b_ref, g_ref, o_ref):
    g = _gelu_tanh(y_ref[...])
    gate = jax.nn.sigmoid(jnp.dot(g.astype(BF16), w_ref[...], preferred_element_type=F32) + b_ref[...])
    o_ref[...] = _rms(g * gate, g_ref[...]).astype(o_ref.dtype)


def _glu(y, w, b, gain):
    tm = 256
    return pl.pallas_call(
        _glu_kernel,
        grid=(TOKENS // tm,),
        in_specs=[pl.BlockSpec((tm, SSM_WIDTH), lambda i: (i, 0)),
                  pl.BlockSpec((SSM_WIDTH, SSM_WIDTH), lambda i: (0, 0)),
                  pl.BlockSpec((1, SSM_WIDTH), lambda i: (0, 0)),
                  pl.BlockSpec((1, SSM_WIDTH), lambda i: (0, 0))],
        out_specs=pl.BlockSpec((tm, SSM_WIDTH), lambda i: (i, 0)),
        out_shape=jax.ShapeDtypeStruct((TOKENS, SSM_WIDTH), BF16),
        compiler_params=_cparams(1),
        name="s5_glu",
    )(y, w, b, gain)


def _out_proj_kernel(om_ref, os_ref, g_ref, w_ref, x_ref, ga_ref, o_ref, lhs_ref):
    @pl.when(pl.program_id(1) == 0)
    def _():
        lhs_ref[:, :MLA_WIDTH] = _rms(om_ref[...], g_ref[...]).astype(BF16)
        lhs_ref[:, MLA_WIDTH:] = os_ref[...]

    o_ref[...] = x_ref[...] + ga_ref[...] * jnp.dot(lhs_ref[...], w_ref[...], preferred_element_type=F32)


def _out_proj(o_mla, o_ssm, gain, w, x2, mod3, gate_chunk):
    tm, tn = 512, 1024
    per_b = SEQ // tm
    nj = D_MODEL // tn
    return pl.pallas_call(
        _out_proj_kernel,
        grid=(TOKENS // tm, nj),
        in_specs=[pl.BlockSpec((tm, MLA_WIDTH), lambda i, j: (i, 0)),
                  pl.BlockSpec((tm, SSM_WIDTH), lambda i, j: (i, 0)),
                  pl.BlockSpec((1, MLA_WIDTH), lambda i, j: (0, 0)),
                  pl.BlockSpec((D_MODEL, tn), lambda i, j: (0, j)),
                  pl.BlockSpec((tm, tn), lambda i, j: (i, j)),
                  pl.BlockSpec((None, 1, tn), lambda i, j: (i // per_b, 0, gate_chunk * nj + j))],
        out_specs=pl.BlockSpec((tm, tn), lambda i, j: (i, j)),
        out_shape=jax.ShapeDtypeStruct((TOKENS, D_MODEL), F32),
        scratch_shapes=[pltpu.VMEM((tm, D_MODEL), BF16)],
        compiler_params=_cparams(2),
        name="out_proj",
    )(o_mla, o_ssm, gain, w, x2, mod3)


def _router_kernel(x_ref, g_ref, sh_ref, sc_ref, whi_ref, wlo_ref, br_ref, h_ref, eid_ref, wt_ref):
    h = _rms(x_ref[...], g_ref[...]) * (1.0 + sc_ref[...]) + sh_ref[...]
    h_ref[...] = h
    h_hi = h.astype(BF16)
    h_lo = (h - h_hi.astype(F32)).astype(BF16)

    whi = whi_ref[...]
    logits = (jnp.dot(h_hi, whi, preferred_element_type=F32)
              + jnp.dot(h_lo, whi, preferred_element_type=F32)
              + jnp.dot(h_hi, wlo_ref[...], preferred_element_type=F32)
              + br_ref[...])

    col = lax.broadcasted_iota(jnp.int32, logits.shape, 1)
    neg = -jnp.inf
    is_g = col < N_EGROUPS
    gl = jnp.where(is_g, logits, neg)
    gmax = jnp.max(gl, axis=1, keepdims=True)
    grp = jnp.min(jnp.where(gl == gmax, col, LANES), axis=1, keepdims=True)
    p_grp = 1.0 / jnp.sum(jnp.where(is_g, jnp.exp(logits - gmax), 0.0), axis=1, keepdims=True)

    ecol = col - N_EGROUPS
    in_grp = (ecol >= 0) & (ecol < N_EXPERTS) & ((ecol // EXPERTS_PER_GROUP) == grp)
    el = jnp.where(in_grp, logits, neg)
    v1 = jnp.max(el, axis=1, keepdims=True)
    i1 = jnp.min(jnp.where(el == v1, col, LANES), axis=1, keepdims=True)
    el2 = jnp.where(col == i1, neg, el)
    v2 = jnp.max(el2, axis=1, keepdims=True)
    i2 = jnp.min(jnp.where(el2 == v2, col, LANES), axis=1, keepdims=True)

    s2 = jnp.exp(v2 - v1)
    w1 = p_grp / (1.0 + s2)
    w2 = p_grp * s2 / (1.0 + s2)
    eid_ref[...] = jnp.where(col == 0, i1 - N_EGROUPS, jnp.where(col == 1, i2 - N_EGROUPS, 0))
    wt_ref[...] = jnp.where(col == 0, w1, jnp.where(col == 1, w2, 0.0))


def _router(x1, gain, mod3, shift_chunk, scale_chunk, w_hi, w_lo, b_row):
    tm = 256
    per_b = SEQ // tm
    return pl.pallas_call(
        _router_kernel,
        grid=(TOKENS // tm,),
        in_specs=[pl.BlockSpec((tm, D_MODEL), lambda i: (i, 0)),
                  pl.BlockSpec((1, D_MODEL), lambda i: (0, 0)),
                  pl.BlockSpec((None, 1, D_MODEL), lambda i: (i // per_b, 0, shift_chunk)),
                  pl.BlockSpec((None, 1, D_MODEL), lambda i: (i // per_b, 0, scale_chunk)),
                  pl.BlockSpec((D_MODEL, LANES), lambda i: (0, 0)),
                  pl.BlockSpec((D_MODEL, LANES), lambda i: (0, 0)),
                  pl.BlockSpec((1, LANES), lambda i: (0, 0))],
        out_specs=[pl.BlockSpec((tm, D_MODEL), lambda i: (i, 0)),
                   pl.BlockSpec((tm, LANES), lambda i: (i, 0)),
                   pl.BlockSpec((tm, LANES), lambda i: (i, 0))],
        out_shape=[jax.ShapeDtypeStruct((TOKENS, D_MODEL), F32),
                   jax.ShapeDtypeStruct((TOKENS, LANES), jnp.int32),
                   jax.ShapeDtypeStruct((TOKENS, LANES), F32)],
        compiler_params=_cparams(1),
        name="ffn_router",
    )(x1, gain, mod3, mod3, w_hi, w_lo, b_row)


def _moe_plan(eid):
    n_assign = TOKENS * TOP_K
    e_flat = eid.reshape(n_assign)
    onehot = (e_flat[:, None] == jnp.arange(N_EXPERTS, dtype=jnp.int32)[None, :]).astype(jnp.int32)
    csum = jnp.cumsum(onehot, axis=0)
    counts = csum[-1]
    rank = jnp.sum((csum - onehot) * onehot, axis=1)
    nblk_e = (counts + MOE_BLK - 1) // MOE_BLK
    bend = jnp.cumsum(nblk_e)
    bstart = bend - nblk_e
    dest = jnp.sum(onehot * bstart[None, :], axis=1) * MOE_BLK + rank
    a_id = jnp.arange(n_assign, dtype=jnp.int32)
    row_dst = jnp.zeros((MOE_ROWS,), jnp.int32).at[dest].set((a_id % TOP_K) * TOKENS + a_id // TOP_K,
                                                              unique_indices=True)
    blk = jnp.arange(MOE_NBLK, dtype=jnp.int32)
    n_used = bend[-1]
    blk_e = jnp.minimum(jnp.sum((bend[None, :] <= blk[:, None]).astype(jnp.int32), axis=1), N_EXPERTS - 1)
    blk_nvalid = jnp.clip(counts[blk_e] - (blk - bstart[blk_e]) * MOE_BLK, 0, MOE_BLK)
    blk_nvalid = jnp.where(blk < n_used, blk_nvalid, 0).astype(jnp.int32)
    last_e = blk_e[jnp.maximum(n_used - 1, 0)]
    blk_e = jnp.where(blk < n_used, blk_e, last_e).astype(jnp.int32)
    return row_dst, blk_e, blk_nvalid, n_used.reshape(1).astype(jnp.int32)


def _copy_groups(n_rows):
    return lax.shift_right_logical(n_rows + (DMA_UNROLL - 1), DMA_UNROLL.bit_length() - 1)


def _expert_up_kernel(dst_ref, be_ref, nv_ref, nu_ref, h_hbm, w1_ref, w3_ref, o_ref, xbuf, sem):
    i = pl.program_id(0)
    slot = i % 2

    def row_copy(src_row, dst_slot, dst_row):
        return pltpu.make_async_copy(h_hbm.at[pl.ds(src_row, 1)], xbuf.at[dst_slot, pl.ds(dst_row, 1)],
                                     sem.at[dst_slot])

    def issue(block, dst_slot):
        base = block * MOE_BLK

        def body(g, c):
            for u in range(DMA_UNROLL):
                r = g * DMA_UNROLL + u
                row_copy(dst_ref[base + r] & (TOKENS - 1), dst_slot, r).start()
            return c

        lax.fori_loop(0, _copy_groups(nv_ref[block]), body, 0)

    @pl.when(i == 0)
    def _():
        xbuf[...] = jnp.zeros_like(xbuf)
        issue(0, 0)

    @pl.when(i + 1 < pl.num_programs(0))
    def _():
        issue(i + 1, 1 - slot)

    def wait_body(g, c):
        for u in range(DMA_UNROLL):
            row_copy(0, slot, g * DMA_UNROLL + u).wait()
        return c

    lax.fori_loop(0, _copy_groups(nv_ref[i]), wait_body, 0)

    @pl.when(i < nu_ref[0])
    def _():
        x = xbuf[slot].astype(BF16)
        a = jnp.dot(x, w1_ref[...].astype(BF16), preferred_element_type=F32)
        b = jnp.dot(x, w3_ref[...].astype(BF16), preferred_element_type=F32)
        o_ref[...] = (a * jax.nn.sigmoid(a) * b).astype(o_ref.dtype)

    @pl.when(i >= nu_ref[0])
    def _():
        o_ref[...] = jnp.zeros_like(o_ref)


def _expert_up(row_dst, blk_e, blk_nvalid, n_used, h2, w1, w3):
    wspec = pl.BlockSpec((None, D_MODEL, D_EXPERT), lambda i, tok, be, ng, nu: (be[i], 0, 0))
    return pl.pallas_call(
        _expert_up_kernel,
        grid_spec=pltpu.PrefetchScalarGridSpec(
            num_scalar_prefetch=4,
            grid=(MOE_NBLK,),
            in_specs=[pl.BlockSpec(memory_space=pl.ANY), wspec, wspec],
            out_specs=pl.BlockSpec((MOE_BLK, D_EXPERT), lambda i, tok, be, ng, nu: (i, 0)),
            scratch_shapes=[pltpu.VMEM((2, MOE_BLK, D_MODEL), F32),
                            pltpu.SemaphoreType.DMA((2,))]),
        out_shape=jax.ShapeDtypeStruct((MOE_ROWS, D_EXPERT), BF16),
        compiler_params=_cparams(1, 60 * 1024 * 1024),
        name="moe_expert_up",
    )(row_dst, blk_e, blk_nvalid, n_used, h2, w1, w3)


def _expert_down_kernel(dst_ref, be_ref, nv_ref, nu_ref, h_ref, w_ref, y_hbm, ybuf, sem):
    i = pl.program_id(0)
    slot = i % 2

    def row_copy(src_row, dst_row, src_slot):
        return pltpu.make_async_copy(ybuf.at[src_slot, pl.ds(src_row, 1)], y_hbm.at[pl.ds(dst_row, 1)],
                                     sem.at[src_slot])

    def drain(block, src_slot):
        def body(r, c):
            row_copy(r, 0, src_slot).wait()
            return c

        lax.fori_loop(0, nv_ref[block], body, 0)

    @pl.when(i >= 2)
    def _():
        drain(i - 2, slot)

    @pl.when(i < nu_ref[0])
    def _():
        ybuf[slot] = jnp.dot(h_ref[...], w_ref[...].astype(BF16), preferred_element_type=F32)
        base = i * MOE_BLK
        nv = nv_ref[i]
        full = lax.shift_right_logical(nv, DMA_UNROLL.bit_length() - 1)

        def body(g, c):
            for u in range(DMA_UNROLL):
                r = g * DMA_UNROLL + u
                row_copy(r, dst_ref[base + r], slot).start()
            return c

        lax.fori_loop(0, full, body, 0)

        def tail(r, c):
            row_copy(r, dst_ref[base + r], slot).start()
            return c

        lax.fori_loop(full * DMA_UNROLL, nv, tail, 0)

    @pl.when(i == pl.num_programs(0) - 1)
    def _():
        drain(i - 1, 1 - slot)
        drain(i, slot)


def _expert_down(row_dst, blk_e, blk_nvalid, n_used, hmid, w2):
    return pl.pallas_call(
        _expert_down_kernel,
        grid_spec=pltpu.PrefetchScalarGridSpec(
            num_scalar_prefetch=4,
            grid=(MOE_NBLK,),
            in_specs=[pl.BlockSpec((MOE_BLK, D_EXPERT), lambda i, dst, be, nv, nu: (i, 0)),
                      pl.BlockSpec((None, D_EXPERT, D_MODEL), lambda i, dst, be, nv, nu: (be[i], 0, 0))],
            out_specs=pl.BlockSpec(memory_space=pl.ANY),
            scratch_shapes=[pltpu.VMEM((2, MOE_BLK, D_MODEL), F32),
                            pltpu.SemaphoreType.DMA((2,))]),
        out_shape=jax.ShapeDtypeStruct((TOP_K * TOKENS, D_MODEL), F32),
        compiler_params=_cparams(1),
        name="moe_expert_down",
    )(row_dst, blk_e, blk_nvalid, n_used, hmid, w2)


def _combine_kernel(y0_ref, y1_ref, x_ref, wt_ref, gf_ref, fg_ref, o_ref):
    wt = wt_ref[...]
    moe = wt[:, 0:1] * y0_ref[...] + wt[:, 1:2] * y1_ref[...]
    o_ref[...] = _rms(x_ref[...] + gf_ref[...] * moe, fg_ref[...])


def _combine(ys, x1, wt, mod3, gate_chunk, final_gain):
    per_b = SEQ // CMB_TM
    nblk = TOKENS // CMB_TM
    return pl.pallas_call(
        _combine_kernel,
        grid=(nblk,),
        in_specs=[pl.BlockSpec((CMB_TM, D_MODEL), lambda i: (i, 0)),
                  pl.BlockSpec((CMB_TM, D_MODEL), lambda i: (i + nblk, 0)),
                  pl.BlockSpec((CMB_TM, D_MODEL), lambda i: (i, 0)),
                  pl.BlockSpec((CMB_TM, LANES), lambda i: (i, 0)),
                  pl.BlockSpec((None, 1, D_MODEL), lambda i: (i // per_b, 0, gate_chunk)),
                  pl.BlockSpec((1, D_MODEL), lambda i: (0, 0))],
        out_specs=pl.BlockSpec((CMB_TM, D_MODEL), lambda i: (i, 0)),
        out_shape=jax.ShapeDtypeStruct((TOKENS, D_MODEL), F32),
        compiler_params=_cparams(1),
        name="moe_combine_final_norm",
    )(ys, ys, x1, wt, mod3, final_gain)


def _rotate_half_cols(w):
    half = w.shape[-1] // 2
    return jnp.concatenate([-w[..., half:], w[..., :half]], axis=-1)


def _block_diag_in(bbar):
    bb = bbar.reshape(SSM_GROUPS // SSM_GB, SSM_GB, SSM_STATE, SSM_GROUP_CH)
    w = jnp.einsum('bgpn,gh->bgnhp', bb, jnp.eye(SSM_GB, dtype=bbar.dtype))
    return w.reshape(SSM_GROUPS // SSM_GB, SSM_GB_CH, SSM_GB_ST)


def _block_diag_out(cmat):
    cc = cmat.reshape(SSM_GROUPS // SSM_GB, SSM_GB, SSM_GROUP_CH, SSM_STATE)
    w = jnp.einsum('bgnp,gh->bgphn', cc, jnp.eye(SSM_GB, dtype=cmat.dtype))
    return w.reshape(SSM_GROUPS // SSM_GB, SSM_GB_ST, SSM_GB_CH)


def kernel(x, c, positions, w_ada, b_ada, norm_mix_gain, w_in, q_lat_gain, w_uq, kv_lat_gain, w_ukv,
           ssm_lam_re, ssm_lam_im, ssm_log_dt, ssm_b_re, ssm_b_im, ssm_c_re, ssm_c_im, ssm_d,
           w_glu, b_glu, mla_out_gain, ssm_out_gain, w_out, norm_ffn_gain,
           w_group_router, b_group_router, w_expert_router, b_expert_router,
           w1_experts, w3_experts, w2_experts, final_gain):
    x2 = x.reshape(TOKENS, D_MODEL)

    c8 = jnp.zeros((SUBLANES, D_MODEL), F32).at[:BATCH].set(c)
    mod = _ada(c8, w_ada[0], b_ada[0].reshape(1, 6 * D_MODEL))
    mod3 = mod.reshape(SUBLANES, 1, 6 * D_MODEL)

    z = _in_proj(x2, norm_mix_gain[0].reshape(1, D_MODEL), mod3, 0, 1, _wz_prep(w_in[0].T))

    cs = _rope_table(positions.reshape(TOKENS, 1))

    wq = w_uq[0].reshape(Q_RANK, MLA_HEADS, QK_HEAD)
    wq_heads = jnp.concatenate([wq, _rotate_half_cols(wq[:, :, QK_NOPE:])], axis=-1)
    wq_heads = wq_heads.astype(BF16).reshape(Q_RANK, MLA_HEADS * 2 * LANES)
    q = _q_proj(z, q_lat_gain[0].reshape(1, Q_RANK), wq_heads, cs)
    wkv = w_ukv[0].reshape(KV_RANK, MLA_HEADS, QK_NOPE + V_HEAD)
    wk_heads = wkv[:, :, :QK_NOPE].astype(BF16).reshape(KV_RANK, MLA_HEADS * QK_NOPE)
    wvt_heads = wkv[:, :, QK_NOPE:].transpose(1, 2, 0).astype(BF16)
    k, vt = _kv_proj(z, kv_lat_gain[0].reshape(1, KV_RANK), wk_heads, wvt_heads, cs)
    o_mla = _attention(q, k, vt)

    n_st = SSM_GROUPS * SSM_STATE
    abar_re, abar_im, bbar_re, bbar_im = _s5_disc(
        ssm_lam_re[0].reshape(n_st, 1), ssm_lam_im[0].reshape(n_st, 1),
        jnp.repeat(ssm_log_dt[0], SSM_STATE).reshape(n_st, 1),
        ssm_b_re[0].reshape(n_st, SSM_GROUP_CH), ssm_b_im[0].reshape(n_st, SSM_GROUP_CH))
    n_gb = SSM_GROUPS // SSM_GB
    wb = jnp.concatenate([_block_diag_in(bbar_re), _block_diag_in(bbar_im)], axis=-1)
    wb = wb.reshape(SSM_PAIRS, 2 * SSM_GB_CH, 2 * SSM_GB_ST).astype(BF16)
    wc = jnp.concatenate([_block_diag_out(ssm_c_re[0]), -_block_diag_out(ssm_c_im[0])], axis=1)
    wc = wc.reshape(SSM_PAIRS, 2, 2 * SSM_GB_ST, SSM_GB_CH).transpose(0, 2, 1, 3)
    wc = wc.reshape(SSM_PAIRS, 2 * SSM_GB_ST, 2 * SSM_GB_CH).astype(BF16)
    a16 = jnp.concatenate([abar_re.reshape(n_gb, SSM_GB_ST), abar_im.reshape(n_gb, SSM_GB_ST)], axis=-1)
    a8 = jnp.repeat(a16.reshape(SSM_PAIRS, 2, 2 * SSM_GB_ST), BATCH, axis=1)

    y = _s5_scan(z.reshape(BATCH, SEQ, Z_COLS), wb, a8, wc, ssm_d[0].reshape(1, SSM_WIDTH))
    o_ssm = _glu(y.reshape(TOKENS, SSM_WIDTH), w_glu[0].astype(BF16), b_glu[0].reshape(1, SSM_WIDTH),
                 ssm_out_gain[0].reshape(1, SSM_WIDTH))

    x1 = _out_proj(o_mla, o_ssm, mla_out_gain[0].reshape(1, MLA_WIDTH), w_out[0].astype(BF16), x2, mod3, 2)

    w_r = jnp.concatenate([w_group_router[0], w_expert_router[0],
                           jnp.zeros((D_MODEL, LANES - N_EGROUPS - N_EXPERTS), F32)], axis=1)
    w_r_hi = w_r.astype(BF16)
    w_r_lo = (w_r - w_r_hi.astype(F32)).astype(BF16)
    b_r = jnp.concatenate([b_group_router[0], b_expert_router[0],
                           jnp.zeros((LANES - N_EGROUPS - N_EXPERTS,), F32)]).reshape(1, LANES)
    h2, eid, wt = _router(x1, norm_ffn_gain[0].reshape(1, D_MODEL), mod3, 3, 4, w_r_hi, w_r_lo, b_r)

    row_dst, blk_e, blk_nvalid, n_used = _moe_plan(eid[:, :TOP_K])
    hmid = _expert_up(row_dst, blk_e, blk_nvalid, n_used, h2, w1_experts[0], w3_experts[0])
    ys = _expert_down(row_dst, blk_e, blk_nvalid, n_used, hmid, w2_experts[0])
    out = _combine(ys, x1, wt, mod3, 5, final_gain.reshape(1, D_MODEL))
    return out.reshape(BATCH, SEQ, D_MODEL)
```

```python
import math

import jax
import jax.numpy as jnp
from jax import lax
from jax.experimental import pallas as pl
from jax.experimental.pallas import tpu as pltpu

F32 = jnp.float32
BF16 = jnp.bfloat16

D_MODEL = 4096
BATCH = 4
SEQ = 2048
TOKENS = BATCH * SEQ
CHUNK = 64
EPS = 1e-6

MLA_HEADS = 16
QK_NOPE = 128
QK_ROPE = 64
QK_HEAD = QK_NOPE + QK_ROPE
V_HEAD = 128
Q_RANK = 768
KV_RANK = 512
ROPE_THETA = 10000.0
MLA_WIDTH = MLA_HEADS * V_HEAD

SSM_WIDTH = D_MODEL - MLA_WIDTH
SSM_GROUP_CH = 16
SSM_GROUPS = SSM_WIDTH // SSM_GROUP_CH
SSM_STATE = 64

N_EGROUPS = 8
EXPERTS_PER_GROUP = 8
N_EXPERTS = N_EGROUPS * EXPERTS_PER_GROUP
TOP_K = 2
D_EXPERT = 512

LANES = 128
SUBLANES = 8
VMEM_LIMIT = 56 * 1024 * 1024

Z_Q = 0
Z_SSM = Q_RANK
Z_KV = Z_SSM + SSM_WIDTH
Z_KR = Z_KV + KV_RANK
Z_COLS = Z_KR + 2 * QK_ROPE

ATT_BLK = 256
ATT_NBLK = SEQ // ATT_BLK

SSM_GB = 8
SSM_GB_CH = SSM_GB * SSM_GROUP_CH
SSM_GB_ST = SSM_GB * SSM_STATE
SSM_PAIRS = SSM_GROUPS // (2 * SSM_GB)
SSM_TCHUNK = 256

MOE_BLK = 256
MOE_NBLK = (TOKENS * TOP_K + N_EXPERTS * (MOE_BLK - 1)) // MOE_BLK + 1
MOE_ROWS = MOE_NBLK * MOE_BLK
DMA_UNROLL = 8
CMB_TM = 128


def _cparams(n_axes, vmem=VMEM_LIMIT):
    return pltpu.CompilerParams(dimension_semantics=("arbitrary",) * n_axes, vmem_limit_bytes=vmem)


def _ada_kernel(c_ref, w_ref, b_ref, o_ref):
    c = c_ref[...]
    ca = (c * jax.nn.sigmoid(c)).astype(BF16)
    o_ref[...] = jnp.dot(ca, w_ref[...].astype(BF16), preferred_element_type=F32) + b_ref[...]


def _ada(c8, w, b):
    n = w.shape[1]
    tn = 512
    return pl.pallas_call(
        _ada_kernel,
        grid=(n // tn,),
        in_specs=[pl.BlockSpec((SUBLANES, D_MODEL), lambda j: (0, 0)),
                  pl.BlockSpec((D_MODEL, tn), lambda j: (0, j)),
                  pl.BlockSpec((1, tn), lambda j: (0, j))],
        out_specs=pl.BlockSpec((SUBLANES, tn), lambda j: (0, j)),
        out_shape=jax.ShapeDtypeStruct((SUBLANES, n), F32),
        compiler_params=_cparams(1),
        name="ada_mod",
    )(c8, w, b)


def _rms(x, gain):
    return x * lax.rsqrt(jnp.mean(x * x, axis=-1, keepdims=True) + EPS) * gain


def _in_proj_kernel(x_ref, g_ref, sh_ref, sc_ref, w_ref, o_ref, h_ref):
    @pl.when(pl.program_id(1) == 0)
    def _():
        y = _rms(x_ref[...], g_ref[...])
        h_ref[...] = (y * (1.0 + sc_ref[...]) + sh_ref[...]).astype(h_ref.dtype)

    o_ref[...] = lax.dot_general(h_ref[...], w_ref[...], (((1,), (1,)), ((), ())), preferred_element_type=F32)


def _in_proj(x2, gain, mod3, shift_chunk, scale_chunk, w_t):
    tm = 512
    n = w_t.shape[0]
    tn = n // 3
    per_b = SEQ // tm
    return pl.pallas_call(
        _in_proj_kernel,
        grid=(TOKENS // tm, n // tn),
        in_specs=[pl.BlockSpec((tm, D_MODEL), lambda i, j: (i, 0)),
                  pl.BlockSpec((1, D_MODEL), lambda i, j: (0, 0)),
                  pl.BlockSpec((None, 1, D_MODEL), lambda i, j: (i // per_b, 0, shift_chunk)),
                  pl.BlockSpec((None, 1, D_MODEL), lambda i, j: (i // per_b, 0, scale_chunk)),
                  pl.BlockSpec((tn, D_MODEL), lambda i, j: (j, 0))],
        out_specs=pl.BlockSpec((tm, tn), lambda i, j: (i, j)),
        out_shape=jax.ShapeDtypeStruct((TOKENS, n), F32),
        scratch_shapes=[pltpu.VMEM((tm, D_MODEL), BF16)],
        compiler_params=_cparams(2),
        name="in_proj",
    )(x2, gain, mod3, mod3, w_t)


W_IN_KV = Q_RANK
W_IN_KR = Q_RANK + KV_RANK
W_IN_SSM = W_IN_KR + QK_ROPE
W_IN_COLS = W_IN_SSM + SSM_WIDTH


def _wz_kernel(w_ref, o_ref):
    o_ref[Z_Q:Z_Q + Q_RANK] = w_ref[:W_IN_KV].astype(o_ref.dtype)
    o_ref[Z_SSM:Z_SSM + SSM_WIDTH] = w_ref[W_IN_SSM:].astype(o_ref.dtype)
    o_ref[Z_KV:Z_KV + KV_RANK] = w_ref[W_IN_KV:W_IN_KR].astype(o_ref.dtype)
    half = QK_ROPE // 2
    o_ref[Z_KR:Z_KR + QK_ROPE] = w_ref[W_IN_KR:W_IN_SSM].astype(o_ref.dtype)
    o_ref[Z_KR + QK_ROPE:Z_KR + QK_ROPE + half] = (-w_ref[W_IN_KR + half:W_IN_SSM]).astype(o_ref.dtype)
    o_ref[Z_KR + QK_ROPE + half:] = w_ref[W_IN_KR:W_IN_KR + half].astype(o_ref.dtype)


def _wz_prep(w_in_t):
    tk = 512
    return pl.pallas_call(
        _wz_kernel,
        grid=(D_MODEL // tk,),
        in_specs=[pl.BlockSpec((W_IN_COLS, tk), lambda i: (0, i))],
        out_specs=pl.BlockSpec((Z_COLS, tk), lambda i: (0, i)),
        out_shape=jax.ShapeDtypeStruct((Z_COLS, D_MODEL), BF16),
        compiler_params=_cparams(1),
        name="in_proj_weight_prep",
    )(w_in_t)


def _rope_table_kernel(pos_ref, o_ref):
    lane = lax.broadcasted_iota(jnp.int32, (1, LANES), 1)
    pair = (lane % (QK_ROPE // 2)).astype(F32)
    inv_freq = jnp.exp(-math.log(ROPE_THETA) * (2.0 * pair) / QK_ROPE)
    ang = pos_ref[...].astype(F32) * inv_freq
    o_ref[...] = jnp.where(lane < QK_ROPE, jnp.cos(ang), jnp.sin(ang))


def _rope_table(pos_col):
    tm = 1024
    return pl.pallas_call(
        _rope_table_kernel,
        grid=(TOKENS // tm,),
        in_specs=[pl.BlockSpec((tm, 1), lambda i: (i, 0))],
        out_specs=pl.BlockSpec((tm, LANES), lambda i: (i, 0)),
        out_shape=jax.ShapeDtypeStruct((TOKENS, LANES), F32),
        compiler_params=_cparams(1),
        name="rope_table",
    )(pos_col)


def _rope_pair(t, cs):
    u = t * cs
    return u + pltpu.roll(u, QK_ROPE, axis=1)


def _q_proj_kernel(ql_ref, g_ref, w_ref, cs_ref, o_ref):
    hn = _rms(ql_ref[...], g_ref[...]).astype(BF16)
    cs = cs_ref[...]
    for h in range(MLA_HEADS):
        r = jnp.dot(hn, w_ref[:, h * 2 * LANES:(h + 1) * 2 * LANES], preferred_element_type=F32)
        o_ref[h, :, :QK_NOPE] = r[:, :QK_NOPE].astype(o_ref.dtype)
        o_ref[h, :, QK_NOPE:] = _rope_pair(r[:, QK_NOPE:], cs)[:, :QK_ROPE].astype(o_ref.dtype)


def _q_proj(z, gain, w_heads, cs):
    tm = ATT_BLK
    per_b = SEQ // tm
    return pl.pallas_call(
        _q_proj_kernel,
        grid=(TOKENS // tm,),
        in_specs=[pl.BlockSpec((tm, Q_RANK), lambda i: (i, Z_Q // Q_RANK)),
                  pl.BlockSpec((1, Q_RANK), lambda i: (0, 0)),
                  pl.BlockSpec((Q_RANK, MLA_HEADS * 2 * LANES), lambda i: (0, 0)),
                  pl.BlockSpec((tm, LANES), lambda i: (i, 0))],
        out_specs=pl.BlockSpec((None, MLA_HEADS, tm, QK_HEAD), lambda i: (i // per_b, 0, i % per_b, 0)),
        out_shape=jax.ShapeDtypeStruct((BATCH, MLA_HEADS, SEQ, QK_HEAD), BF16),
        compiler_params=_cparams(1),
        name="mla_q_proj",
    )(z, gain, w_heads, cs)


def _kv_proj_kernel(kva_ref, kvb_ref, kr_ref, g_ref, wk_ref, wvt_ref, cs_ref, k_ref, vt_ref):
    kvl = jnp.concatenate([kva_ref[...], kvb_ref[...]], axis=1)
    hn = _rms(kvl, g_ref[...]).astype(BF16)
    kr = _rope_pair(kr_ref[...], cs_ref[...])[:, :QK_ROPE].astype(k_ref.dtype)
    for h in range(MLA_HEADS):
        k_ref[h, :, :QK_NOPE] = jnp.dot(hn, wk_ref[:, h * QK_NOPE:(h + 1) * QK_NOPE],
                                        preferred_element_type=F32).astype(k_ref.dtype)
        k_ref[h, :, QK_NOPE:] = kr
        vt_ref[h] = lax.dot_general(wvt_ref[h], hn, (((1,), (1,)), ((), ())),
                                    preferred_element_type=F32).astype(vt_ref.dtype)


def _kv_proj(z, gain, wk_heads, wvt_heads, cs):
    tm = ATT_BLK
    per_b = SEQ // tm
    half = KV_RANK // 2
    return pl.pallas_call(
        _kv_proj_kernel,
        grid=(TOKENS // tm,),
        in_specs=[pl.BlockSpec((tm, half), lambda i: (i, Z_KV // half)),
                  pl.BlockSpec((tm, half), lambda i: (i, Z_KV // half + 1)),
                  pl.BlockSpec((tm, LANES), lambda i: (i, Z_KR // LANES)),
                  pl.BlockSpec((1, KV_RANK), lambda i: (0, 0)),
                  pl.BlockSpec((KV_RANK, MLA_HEADS * QK_NOPE), lambda i: (0, 0)),
                  pl.BlockSpec((MLA_HEADS, V_HEAD, KV_RANK), lambda i: (0, 0, 0)),
                  pl.BlockSpec((tm, LANES), lambda i: (i, 0))],
        out_specs=[pl.BlockSpec((None, MLA_HEADS, tm, QK_HEAD), lambda i: (i // per_b, 0, i % per_b, 0)),
                   pl.BlockSpec((None, MLA_HEADS, None, V_HEAD, tm), lambda i: (i // per_b, 0, i % per_b, 0, 0))],
        out_shape=[jax.ShapeDtypeStruct((BATCH, MLA_HEADS, SEQ, QK_HEAD), BF16),
                   jax.ShapeDtypeStruct((BATCH, MLA_HEADS, ATT_NBLK, V_HEAD, ATT_BLK), BF16)],
        compiler_params=_cparams(1),
        name="mla_kv_proj",
    )(z, z, z, gain, wk_heads, wvt_heads, cs)


ATT_HEADS = 4


def _attn_kernel(q_ref, k_ref, vt_ref, o_ref, acc_ref):
    log2_scale = (QK_HEAD ** -0.5) * math.log2(math.e)
    key_chunk = lax.broadcasted_iota(jnp.int32, (ATT_BLK, ATT_BLK), 0) // CHUNK
    qry_chunk = lax.broadcasted_iota(jnp.int32, (ATT_BLK, ATT_BLK), 1) // CHUNK
    diag_mask = key_chunk <= qry_chunk

    def q_body(qi, carry):
        q0 = pl.multiple_of(qi * ATT_BLK, ATT_BLK)

        def kv_block(j, stats, n_blk, masked):
            keys = n_blk * ATT_BLK
            k0 = pl.multiple_of(j * ATT_BLK, ATT_BLK)
            sts = [lax.dot_general(k_ref[h, pl.ds(k0, keys), :], q_ref[h, pl.ds(q0, ATT_BLK), :],
                                   (((1,), (1,)), ((), ())), preferred_element_type=F32)
                   for h in range(ATT_HEADS)]
            new_stats, ps, alphas = [], [], []
            for h in range(ATT_HEADS):
                m, l = stats[h]
                st = sts[h] * log2_scale
                if masked:
                    mask = diag_mask if n_blk == 1 else jnp.concatenate(
                        [jnp.ones(((n_blk - 1) * ATT_BLK, ATT_BLK), jnp.bool_), diag_mask], axis=0)
                    st = jnp.where(mask, st, -jnp.inf)
                m_new = jnp.maximum(m, jnp.max(st, axis=0, keepdims=True))
                alpha = jnp.exp2(m - m_new)
                p = jnp.exp2(st - m_new)
                new_stats.append((m_new, alpha * l + jnp.sum(p, axis=0, keepdims=True)))
                ps.append(p.astype(BF16))
                alphas.append(alpha)
            for h in range(ATT_HEADS):
                pv = jnp.dot(vt_ref[h, j], ps[h][:ATT_BLK], preferred_element_type=F32)
                for b in range(1, n_blk):
                    pv = pv + jnp.dot(vt_ref[h, j + b], ps[h][b * ATT_BLK:(b + 1) * ATT_BLK],
                                      preferred_element_type=F32)
                acc_ref[h] = alphas[h] * acc_ref[h] + pv
            return tuple(new_stats)

        acc_ref[...] = jnp.zeros_like(acc_ref)
        stats = tuple((jnp.full((1, ATT_BLK), -jnp.inf, F32), jnp.zeros((1, ATT_BLK), F32))
                      for _ in range(ATT_HEADS))
        n_pairs = lax.shift_right_logical(qi, 1)
        stats = lax.fori_loop(0, n_pairs, lambda jj, st: kv_block(2 * jj, st, 2, False), stats)
        stats = lax.cond(qi % 2 == 1, lambda st: kv_block(qi - 1, st, 2, True),
                         lambda st: kv_block(qi, st, 1, True), stats)
        for h in range(ATT_HEADS):
            o_ref[pl.ds(q0, ATT_BLK), h * V_HEAD:(h + 1) * V_HEAD] = (acc_ref[h] / stats[h][1]).T
        return carry

    lax.fori_loop(0, ATT_NBLK, q_body, 0)


def _attention(q, k, vt):
    return pl.pallas_call(
        _attn_kernel,
        grid=(BATCH, MLA_HEADS // ATT_HEADS),
        in_specs=[pl.BlockSpec((None, ATT_HEADS, SEQ, QK_HEAD), lambda b, h: (b, h, 0, 0)),
                  pl.BlockSpec((None, ATT_HEADS, SEQ, QK_HEAD), lambda b, h: (b, h, 0, 0)),
                  pl.BlockSpec((None, ATT_HEADS, ATT_NBLK, V_HEAD, ATT_BLK), lambda b, h: (b, h, 0, 0, 0))],
        out_specs=pl.BlockSpec((SEQ, ATT_HEADS * V_HEAD), lambda b, h: (b, h)),
        out_shape=jax.ShapeDtypeStruct((TOKENS, MLA_WIDTH), F32),
        scratch_shapes=[pltpu.VMEM((ATT_HEADS, V_HEAD, ATT_BLK), F32)],
        compiler_params=_cparams(2),
        name="mla_attention",
    )(q, k, vt)


def _s5_disc_kernel(lre_ref, lim_ref, ldt_ref, bre_ref, bim_ref, are_ref, aim_ref, bbre_ref, bbim_ref):
    lam_re = lre_ref[...]
    lam_im = lim_ref[...]
    dt = jnp.exp(ldt_ref[...])
    mag = jnp.exp(lam_re * dt)
    abar_re = mag * jnp.cos(lam_im * dt)
    abar_im = mag * jnp.sin(lam_im * dt)
    nr = abar_re - 1.0
    ni = abar_im
    den = lam_re * lam_re + lam_im * lam_im
    f_re = (nr * lam_re + ni * lam_im) / den
    f_im = (ni * lam_re - nr * lam_im) / den
    are_ref[...] = abar_re
    aim_ref[...] = abar_im
    b_re = bre_ref[...]
    b_im = bim_ref[...]
    bbre_ref[...] = f_re * b_re - f_im * b_im
    bbim_ref[...] = f_re * b_im + f_im * b_re


def _s5_disc(lam_re, lam_im, log_dt, b_re, b_im):
    n = SSM_GROUPS * SSM_STATE
    tm = 1024
    col = pl.BlockSpec((tm, 1), lambda i: (i, 0))
    mat = pl.BlockSpec((tm, SSM_GROUP_CH), lambda i: (i, 0))
    return pl.pallas_call(
        _s5_disc_kernel,
        grid=(n // tm,),
        in_specs=[col, col, col, mat, mat],
        out_specs=[col, col, mat, mat],
        out_shape=[jax.ShapeDtypeStruct((n, 1), F32), jax.ShapeDtypeStruct((n, 1), F32),
                   jax.ShapeDtypeStruct((n, SSM_GROUP_CH), F32), jax.ShapeDtypeStruct((n, SSM_GROUP_CH), F32)],
        compiler_params=_cparams(1),
        name="s5_discretise",
    )(lam_re, lam_im, log_dt, b_re, b_im)


def _s5_scan_kernel(u_ref, wb_ref, a_ref, wc_ref, d_ref, y_ref, lhs_ref, s_ref, yb_ref, h_ref):
    @pl.when(pl.program_id(1) == 0)
    def _():
        h_ref[...] = jnp.zeros_like(h_ref)
        lhs_ref[...] = jnp.zeros_like(lhs_ref)

    for b in range(BATCH):
        lhs_ref[0, pl.ds(b, SSM_TCHUNK, stride=SUBLANES), :] = u_ref[b, :, :SSM_GB_CH]
        lhs_ref[1, pl.ds(b + BATCH, SSM_TCHUNK, stride=SUBLANES), :] = u_ref[b, :, SSM_GB_CH:]
    half = SSM_TCHUNK * SUBLANES // 2
    wb = wb_ref[...]
    for r0 in (0, half):
        um = jnp.concatenate([lhs_ref[0, r0:r0 + half], lhs_ref[1, r0:r0 + half]], axis=1).astype(BF16)
        s_ref[r0:r0 + half] = jnp.dot(um, wb, preferred_element_type=F32)

    a_re = a_ref[:, :SSM_GB_ST]
    a_im = a_ref[:, SSM_GB_ST:]

    def step(t, carry):
        h_re, h_im = carry
        r0 = pl.multiple_of(t * SUBLANES, SUBLANES)
        n_re = a_re * h_re - a_im * h_im + s_ref[pl.ds(r0, SUBLANES), :SSM_GB_ST]
        n_im = a_re * h_im + a_im * h_re + s_ref[pl.ds(r0, SUBLANES), SSM_GB_ST:]
        s_ref[pl.ds(r0, SUBLANES), :SSM_GB_ST] = n_re
        s_ref[pl.ds(r0, SUBLANES), SSM_GB_ST:] = n_im
        return n_re, n_im

    h_re, h_im = lax.fori_loop(0, SSM_TCHUNK, step, (h_ref[:, :SSM_GB_ST], h_ref[:, SSM_GB_ST:]), unroll=8)
    h_ref[:, :SSM_GB_ST] = h_re
    h_ref[:, SSM_GB_ST:] = h_im

    wc = wc_ref[...]
    for r0 in (0, half):
        y8 = jnp.dot(s_ref[r0:r0 + half].astype(BF16), wc, preferred_element_type=F32)
        yb_ref[0, r0:r0 + half] = y8[:, :SSM_GB_CH]
        yb_ref[1, r0:r0 + half] = y8[:, SSM_GB_CH:]
    d = d_ref[...]
    for b in range(BATCH):
        y_ref[b, :, :SSM_GB_CH] = (yb_ref[0, pl.ds(b, SSM_TCHUNK, stride=SUBLANES), :]
                                   + d[:, :SSM_GB_CH] * u_ref[b, :, :SSM_GB_CH])
        y_ref[b, :, SSM_GB_CH:] = (yb_ref[1, pl.ds(b + BATCH, SSM_TCHUNK, stride=SUBLANES), :]
                                   + d[:, SSM_GB_CH:] * u_ref[b, :, SSM_GB_CH:])


def _s5_scan(z3, wb, a8, wc, d_row):
    rows = SSM_TCHUNK * SUBLANES
    width = 2 * SSM_GB_CH
    states = 2 * SSM_GB_ST
    return pl.pallas_call(
        _s5_scan_kernel,
        grid=(SSM_PAIRS, SEQ // SSM_TCHUNK),
        in_specs=[pl.BlockSpec((BATCH, SSM_TCHUNK, width), lambda g, c: (0, c, Z_SSM // width + g)),
                  pl.BlockSpec((None, width, states), lambda g, c: (g, 0, 0)),
                  pl.BlockSpec((None, SUBLANES, states), lambda g, c: (g, 0, 0)),
                  pl.BlockSpec((None, states, width), lambda g, c: (g, 0, 0)),
                  pl.BlockSpec((1, width), lambda g, c: (0, g))],
        out_specs=pl.BlockSpec((BATCH, SSM_TCHUNK, width), lambda g, c: (0, c, g)),
        out_shape=jax.ShapeDtypeStruct((BATCH, SEQ, SSM_WIDTH), F32),
        scratch_shapes=[pltpu.VMEM((2, rows, SSM_GB_CH), F32), pltpu.VMEM((rows, states), F32),
                        pltpu.VMEM((2, rows, SSM_GB_CH), F32), pltpu.VMEM((SUBLANES, states), F32)],
        compiler_params=_cparams(2),
        name="s5_scan",
    )(z3, wb, a8, wc, d_row)


def _gelu_tanh(x):
    return 0.5 * x * (1.0 + jnp.tanh(math.sqrt(2.0 / math.pi) * (x + 0.044715 * (x * x * x))))


def _glu_kernel(y_ref, w_ref, b_ref, g_ref, o_ref):
    g = _gelu_tanh(y_ref[...])
    gate = jax.nn.sigmoid(jnp.dot(g.astype(BF16), w_ref[...], preferred_element_type=F32) + b_ref[...])
    o_ref[...] = _rms(g * gate, g_ref[...]).astype(o_ref.dtype)


def _glu(y, w, b, gain):
    tm = 256
    return pl.pallas_call(
        _glu_kernel,
        grid=(TOKENS // tm,),
        in_specs=[pl.BlockSpec((tm, SSM_WIDTH), lambda i: (i, 0)),
                  pl.BlockSpec((SSM_WIDTH, SSM_WIDTH), lambda i: (0, 0)),
                  pl.BlockSpec((1, SSM_WIDTH), lambda i: (0, 0)),
                  pl.BlockSpec((1, SSM_WIDTH), lambda i: (0, 0))],
        out_specs=pl.BlockSpec((tm, SSM_WIDTH), lambda i: (i, 0)),
        out_shape=jax.ShapeDtypeStruct((TOKENS, SSM_WIDTH), BF16),
        compiler_params=_cparams(1),
        name="s5_glu",
    )(y, w, b, gain)


def _out_proj_kernel(om_ref, os_ref, g_ref, w_ref, x_ref, ga_ref, o_ref, lhs_ref):
    @pl.when(pl.program_id(1) == 0)
    def _():
        lhs_ref[:, :MLA_WIDTH] = _rms(om_ref[...], g_ref[...]).astype(BF16)
        lhs_ref[:, MLA_WIDTH:] = os_ref[...]

    o_ref[...] = x_ref[...] + ga_ref[...] * jnp.dot(lhs_ref[...], w_ref[...], preferred_element_type=F32)


def _out_proj(o_mla, o_ssm, gain, w, x2, mod3, gate_chunk):
    tm, tn = 512, 1024
    per_b = SEQ // tm
    nj = D_MODEL // tn
    return pl.pallas_call(
        _out_proj_kernel,
        grid=(TOKENS // tm, nj),
        in_specs=[pl.BlockSpec((tm, MLA_WIDTH), lambda i, j: (i, 0)),
                  pl.BlockSpec((tm, SSM_WIDTH), lambda i, j: (i, 0)),
                  pl.BlockSpec((1, MLA_WIDTH), lambda i, j: (0, 0)),
                  pl.BlockSpec((D_MODEL, tn), lambda i, j: (0, j)),
                  pl.BlockSpec((tm, tn), lambda i, j: (i, j)),
                  pl.BlockSpec((None, 1, tn), lambda i, j: (i // per_b, 0, gate_chunk * nj + j))],
        out_specs=pl.BlockSpec((tm, tn), lambda i, j: (i, j)),
        out_shape=jax.ShapeDtypeStruct((TOKENS, D_MODEL), F32),
        scratch_shapes=[pltpu.VMEM((tm, D_MODEL), BF16)],
        compiler_params=_cparams(2),
        name="out_proj",
    )(o_mla, o_ssm, gain, w, x2, mod3)


def _router_kernel(x_ref, g_ref, sh_ref, sc_ref, whi_ref, wlo_ref, br_ref, h_ref, eid_ref, wt_ref):
    h = _rms(x_ref[...], g_ref[...]) * (1.0 + sc_ref[...]) + sh_ref[...]
    h_ref[...] = h
    h_hi = h.astype(BF16)
    h_lo = (h - h_hi.astype(F32)).astype(BF16)

    whi = whi_ref[...]
    logits = (jnp.dot(h_hi, whi, preferred_element_type=F32)
              + jnp.dot(h_lo, whi, preferred_element_type=F32)
              + jnp.dot(h_hi, wlo_ref[...], preferred_element_type=F32)
              + br_ref[...])

    col = lax.broadcasted_iota(jnp.int32, logits.shape, 1)
    neg = -jnp.inf
    is_g = col < N_EGROUPS
    gl = jnp.where(is_g, logits, neg)
    gmax = jnp.max(gl, axis=1, keepdims=True)
    grp = jnp.min(jnp.where(gl == gmax, col, LANES), axis=1, keepdims=True)
    p_grp = 1.0 / jnp.sum(jnp.where(is_g, jnp.exp(logits - gmax), 0.0), axis=1, keepdims=True)

    ecol = col - N_EGROUPS
    in_grp = (ecol >= 0) & (ecol < N_EXPERTS) & ((ecol // EXPERTS_PER_GROUP) == grp)
    el = jnp.where(in_grp, logits, neg)
    v1 = jnp.max(el, axis=1, keepdims=True)
    i1 = jnp.min(jnp.where(el == v1, col, LANES), axis=1, keepdims=True)
    el2 = jnp.where(col == i1, neg, el)
    v2 = jnp.max(el2, axis=1, keepdims=True)
    i2 = jnp.min(jnp.where(el2 == v2, col, LANES), axis=1, keepdims=True)

    s2 = jnp.exp(v2 - v1)
    w1 = p_grp / (1.0 + s2)
    w2 = p_grp * s2 / (1.0 + s2)
    eid_ref[...] = jnp.where(col == 0, i1 - N_EGROUPS, jnp.where(col == 1, i2 - N_EGROUPS, 0))
    wt_ref[...] = jnp.where(col == 0, w1, jnp.where(col == 1, w2, 0.0))


def _router(x1, gain, mod3, shift_chunk, scale_chunk, w_hi, w_lo, b_row):
    tm = 256
    per_b = SEQ // tm
    return pl.pallas_call(
        _router_kernel,
        grid=(TOKENS // tm,),
        in_specs=[pl.BlockSpec((tm, D_MODEL), lambda i: (i, 0)),
                  pl.BlockSpec((1, D_MODEL), lambda i: (0, 0)),
                  pl.BlockSpec((None, 1, D_MODEL), lambda i: (i // per_b, 0, shift_chunk)),
                  pl.BlockSpec((None, 1, D_MODEL), lambda i: (i // per_b, 0, scale_chunk)),
                  pl.BlockSpec((D_MODEL, LANES), lambda i: (0, 0)),
                  pl.BlockSpec((D_MODEL, LANES), lambda i: (0, 0)),
                  pl.BlockSpec((1, LANES), lambda i: (0, 0))],
        out_specs=[pl.BlockSpec((tm, D_MODEL), lambda i: (i, 0)),
                   pl.BlockSpec((tm, LANES), lambda i: (i, 0)),
                   pl.BlockSpec((tm, LANES), lambda i: (i, 0))],
        out_shape=[jax.ShapeDtypeStruct((TOKENS, D_MODEL), F32),
                   jax.ShapeDtypeStruct((TOKENS, LANES), jnp.int32),
                   jax.ShapeDtypeStruct((TOKENS, LANES), F32)],
        compiler_params=_cparams(1),
        name="ffn_router",
    )(x1, gain, mod3, mod3, w_hi, w_lo, b_row)


def _moe_plan(eid):
    n_assign = TOKENS * TOP_K
    e_flat = eid.reshape(n_assign)
    onehot = (e_flat[:, None] == jnp.arange(N_EXPERTS, dtype=jnp.int32)[None, :]).astype(jnp.int32)
    csum = jnp.cumsum(onehot, axis=0)
    counts = csum[-1]
    rank = jnp.sum((csum - onehot) * onehot, axis=1)
    nblk_e = (counts + MOE_BLK - 1) // MOE_BLK
    bend = jnp.cumsum(nblk_e)
    bstart = bend - nblk_e
    dest = jnp.sum(onehot * bstart[None, :], axis=1) * MOE_BLK + rank
    a_id = jnp.arange(n_assign, dtype=jnp.int32)
    row_dst = jnp.zeros((MOE_ROWS,), jnp.int32).at[dest].set((a_id % TOP_K) * TOKENS + a_id // TOP_K,
                                                              unique_indices=True)
    blk = jnp.arange(MOE_NBLK, dtype=jnp.int32)
    n_used = bend[-1]
    blk_e = jnp.minimum(jnp.sum((bend[None, :] <= blk[:, None]).astype(jnp.int32), axis=1), N_EXPERTS - 1)
    blk_nvalid = jnp.clip(counts[blk_e] - (blk - bstart[blk_e]) * MOE_BLK, 0, MOE_BLK)
    blk_nvalid = jnp.where(blk < n_used, blk_nvalid, 0).astype(jnp.int32)
    last_e = blk_e[jnp.maximum(n_used - 1, 0)]
    blk_e = jnp.where(blk < n_used, blk_e, last_e).astype(jnp.int32)
    return row_dst, blk_e, blk_nvalid, n_used.reshape(1).astype(jnp.int32)


def _copy_groups(n_rows):
    return lax.shift_right_logical(n_rows + (DMA_UNROLL - 1), DMA_UNROLL.bit_length() - 1)


def _expert_up_kernel(dst_ref, be_ref, nv_ref, nu_ref, h_hbm, w1_ref, w3_ref, o_ref, xbuf, sem):
    i = pl.program_id(0)
    slot = i % 2

    def row_copy(src_row, dst_slot, dst_row):
        return pltpu.make_async_copy(h_hbm.at[pl.ds(src_row, 1)], xbuf.at[dst_slot, pl.ds(dst_row, 1)],
                                     sem.at[dst_slot])

    def issue(block, dst_slot):
        base = block * MOE_BLK

        def body(g, c):
            for u in range(DMA_UNROLL):
                r = g * DMA_UNROLL + u
                row_copy(dst_ref[base + r] & (TOKENS - 1), dst_slot, r).start()
            return c

        lax.fori_loop(0, _copy_groups(nv_ref[block]), body, 0)

    @pl.when(i == 0)
    def _():
        xbuf[...] = jnp.zeros_like(xbuf)
        issue(0, 0)

    @pl.when(i + 1 < pl.num_programs(0))
    def _():
        issue(i + 1, 1 - slot)

    def wait_body(g, c):
        for u in range(DMA_UNROLL):
            row_copy(0, slot, g * DMA_UNROLL + u).wait()
        return c

    lax.fori_loop(0, _copy_groups(nv_ref[i]), wait_body, 0)

    def compute(rows):
        x = xbuf[slot, :rows].astype(BF16)
        a = jnp.dot(x, w1_ref[...].astype(BF16), preferred_element_type=F32)
        b = jnp.dot(x, w3_ref[...].astype(BF16), preferred_element_type=F32)
        o_ref[:rows] = (a * jax.nn.sigmoid(a) * b).astype(o_ref.dtype)
        if rows < MOE_BLK:
            o_ref[rows:] = jnp.zeros((MOE_BLK - rows, D_EXPERT), o_ref.dtype)

    nv = nv_ref[i]

    @pl.when(nv > MOE_BLK // 2)
    def _():
        compute(MOE_BLK)

    @pl.when((nv > 0) & (nv <= MOE_BLK // 2))
    def _():
        compute(MOE_BLK // 2)

    @pl.when(nv == 0)
    def _():
        o_ref[...] = jnp.zeros_like(o_ref)


def _expert_up(row_dst, blk_e, blk_nvalid, n_used, h2, w1, w3):
    wspec = pl.BlockSpec((None, D_MODEL, D_EXPERT), lambda i, tok, be, ng, nu: (be[i], 0, 0))
    return pl.pallas_call(
        _expert_up_kernel,
        grid_spec=pltpu.PrefetchScalarGridSpec(
            num_scalar_prefetch=4,
            grid=(MOE_NBLK,),
            in_specs=[pl.BlockSpec(memory_space=pl.ANY), wspec, wspec],
            out_specs=pl.BlockSpec((MOE_BLK, D_EXPERT), lambda i, tok, be, ng, nu: (i, 0)),
            scratch_shapes=[pltpu.VMEM((2, MOE_BLK, D_MODEL), F32),
                            pltpu.SemaphoreType.DMA((2,))]),
        out_shape=jax.ShapeDtypeStruct((MOE_ROWS, D_EXPERT), BF16),
        compiler_params=_cparams(1, 60 * 1024 * 1024),
        name="moe_expert_up",
    )(row_dst, blk_e, blk_nvalid, n_used, h2, w1, w3)


def _expert_down_kernel(dst_ref, be_ref, nv_ref, nu_ref, h_ref, w_ref, y_hbm, ybuf, sem):
    i = pl.program_id(0)
    slot = i % 2

    def row_copy(src_row, dst_row, src_slot):
        return pltpu.make_async_copy(ybuf.at[src_slot, pl.ds(src_row, 1)], y_hbm.at[pl.ds(dst_row, 1)],
                                     sem.at[src_slot])

    def drain(block, src_slot):
        def body(r, c):
            row_copy(r, 0, src_slot).wait()
            return c

        lax.fori_loop(0, nv_ref[block], body, 0)

    @pl.when(i >= 2)
    def _():
        drain(i - 2, slot)

    nv = nv_ref[i]

    @pl.when(nv > MOE_BLK // 2)
    def _():
        ybuf[slot] = jnp.dot(h_ref[...], w_ref[...].astype(BF16), preferred_element_type=F32)

    @pl.when((nv > 0) & (nv <= MOE_BLK // 2))
    def _():
        half = MOE_BLK // 2
        ybuf[slot, :half] = jnp.dot(h_ref[:half], w_ref[...].astype(BF16), preferred_element_type=F32)

    @pl.when(nv > 0)
    def _():
        base = i * MOE_BLK
        full = lax.shift_right_logical(nv, DMA_UNROLL.bit_length() - 1)

        def body(g, c):
            for u in range(DMA_UNROLL):
                r = g * DMA_UNROLL + u
                row_copy(r, dst_ref[base + r], slot).start()
            return c

        lax.fori_loop(0, full, body, 0)

        def tail(r, c):
            row_copy(r, dst_ref[base + r], slot).start()
            return c

        lax.fori_loop(full * DMA_UNROLL, nv, tail, 0)

    @pl.when(i == pl.num_programs(0) - 1)
    def _():
        drain(i - 1, 1 - slot)
        drain(i, slot)


def _expert_down(row_dst, blk_e, blk_nvalid, n_used, hmid, w2):
    return pl.pallas_call(
        _expert_down_kernel,
        grid_spec=pltpu.PrefetchScalarGridSpec(
            num_scalar_prefetch=4,
            grid=(MOE_NBLK,),
            in_specs=[pl.BlockSpec((MOE_BLK, D_EXPERT), lambda i, dst, be, nv, nu: (i, 0)),
                      pl.BlockSpec((None, D_EXPERT, D_MODEL), lambda i, dst, be, nv, nu: (be[i], 0, 0))],
            out_specs=pl.BlockSpec(memory_space=pl.ANY),
            scratch_shapes=[pltpu.VMEM((2, MOE_BLK, D_MODEL), F32),
                            pltpu.SemaphoreType.DMA((2,))]),
        out_shape=jax.ShapeDtypeStruct((TOP_K * TOKENS, D_MODEL), F32),
        compiler_params=_cparams(1),
        name="moe_expert_down",
    )(row_dst, blk_e, blk_nvalid, n_used, hmid, w2)


def _combine_kernel(y0_ref, y1_ref, x_ref, wt_ref, gf_ref, fg_ref, o_ref):
    wt = wt_ref[...]
    moe = wt[:, 0:1] * y0_ref[...] + wt[:, 1:2] * y1_ref[...]
    o_ref[...] = _rms(x_ref[...] + gf_ref[...] * moe, fg_ref[...])


def _combine(ys, x1, wt, mod3, gate_chunk, final_gain):
    per_b = SEQ // CMB_TM
    nblk = TOKENS // CMB_TM
    return pl.pallas_call(
        _combine_kernel,
        grid=(nblk,),
        in_specs=[pl.BlockSpec((CMB_TM, D_MODEL), lambda i: (i, 0)),
                  pl.BlockSpec((CMB_TM, D_MODEL), lambda i: (i + nblk, 0)),
                  pl.BlockSpec((CMB_TM, D_MODEL), lambda i: (i, 0)),
                  pl.BlockSpec((CMB_TM, LANES), lambda i: (i, 0)),
                  pl.BlockSpec((None, 1, D_MODEL), lambda i: (i // per_b, 0, gate_chunk)),
                  pl.BlockSpec((1, D_MODEL), lambda i: (0, 0))],
        out_specs=pl.BlockSpec((CMB_TM, D_MODEL), lambda i: (i, 0)),
        out_shape=jax.ShapeDtypeStruct((TOKENS, D_MODEL), F32),
        compiler_params=_cparams(1),
        name="moe_combine_final_norm",
    )(ys, ys, x1, wt, mod3, final_gain)


def _rotate_half_cols(w):
    half = w.shape[-1] // 2
    return jnp.concatenate([-w[..., half:], w[..., :half]], axis=-1)


def _block_diag_in(bbar):
    bb = bbar.reshape(SSM_GROUPS // SSM_GB, SSM_GB, SSM_STATE, SSM_GROUP_CH)
    w = jnp.einsum('bgpn,gh->bgnhp', bb, jnp.eye(SSM_GB, dtype=bbar.dtype))
    return w.reshape(SSM_GROUPS // SSM_GB, SSM_GB_CH, SSM_GB_ST)


def _block_diag_out(cmat):
    cc = cmat.reshape(SSM_GROUPS // SSM_GB, SSM_GB, SSM_GROUP_CH, SSM_STATE)
    w = jnp.einsum('bgnp,gh->bgphn', cc, jnp.eye(SSM_GB, dtype=cmat.dtype))
    return w.reshape(SSM_GROUPS // SSM_GB, SSM_GB_ST, SSM_GB_CH)


def kernel(x, c, positions, w_ada, b_ada, norm_mix_gain, w_in, q_lat_gain, w_uq, kv_lat_gain, w_ukv,
           ssm_lam_re, ssm_lam_im, ssm_log_dt, ssm_b_re, ssm_b_im, ssm_c_re, ssm_c_im, ssm_d,
           w_glu, b_glu, mla_out_gain, ssm_out_gain, w_out, norm_ffn_gain,
           w_group_router, b_group_router, w_expert_router, b_expert_router,
           w1_experts, w3_experts, w2_experts, final_gain):
    x2 = x.reshape(TOKENS, D_MODEL)

    c8 = jnp.zeros((SUBLANES, D_MODEL), F32).at[:BATCH].set(c)
    mod = _ada(c8, w_ada[0], b_ada[0].reshape(1, 6 * D_MODEL))
    mod3 = mod.reshape(SUBLANES, 1, 6 * D_MODEL)

    z = _in_proj(x2, norm_mix_gain[0].reshape(1, D_MODEL), mod3, 0, 1, _wz_prep(w_in[0].T))

    cs = _rope_table(positions.reshape(TOKENS, 1))

    wq = w_uq[0].reshape(Q_RANK, MLA_HEADS, QK_HEAD)
    wq_heads = jnp.concatenate([wq, _rotate_half_cols(wq[:, :, QK_NOPE:])], axis=-1)
    wq_heads = wq_heads.astype(BF16).reshape(Q_RANK, MLA_HEADS * 2 * LANES)
    q = _q_proj(z, q_lat_gain[0].reshape(1, Q_RANK), wq_heads, cs)
    wkv = w_ukv[0].reshape(KV_RANK, MLA_HEADS, QK_NOPE + V_HEAD)
    wk_heads = wkv[:, :, :QK_NOPE].astype(BF16).reshape(KV_RANK, MLA_HEADS * QK_NOPE)
    wvt_heads = wkv[:, :, QK_NOPE:].transpose(1, 2, 0).astype(BF16)
    k, vt = _kv_proj(z, kv_lat_gain[0].reshape(1, KV_RANK), wk_heads, wvt_heads, cs)
    o_mla = _attention(q, k, vt)

    n_st = SSM_GROUPS * SSM_STATE
    abar_re, abar_im, bbar_re, bbar_im = _s5_disc(
        ssm_lam_re[0].reshape(n_st, 1), ssm_lam_im[0].reshape(n_st, 1),
        jnp.repeat(ssm_log_dt[0], SSM_STATE).reshape(n_st, 1),
        ssm_b_re[0].reshape(n_st, SSM_GROUP_CH), ssm_b_im[0].reshape(n_st, SSM_GROUP_CH))
    n_gb = SSM_GROUPS // SSM_GB
    wb = jnp.concatenate([_block_diag_in(bbar_re), _block_diag_in(bbar_im)], axis=-1)
    wb = wb.reshape(SSM_PAIRS, 2 * SSM_GB_CH, 2 * SSM_GB_ST).astype(BF16)
    wc = jnp.concatenate([_block_diag_out(ssm_c_re[0]), -_block_diag_out(ssm_c_im[0])], axis=1)
    wc = wc.reshape(SSM_PAIRS, 2, 2 * SSM_GB_ST, SSM_GB_CH).transpose(0, 2, 1, 3)
    wc = wc.reshape(SSM_PAIRS, 2 * SSM_GB_ST, 2 * SSM_GB_CH).astype(BF16)
    a16 = jnp.concatenate([abar_re.reshape(n_gb, SSM_GB_ST), abar_im.reshape(n_gb, SSM_GB_ST)], axis=-1)
    a8 = jnp.repeat(a16.reshape(SSM_PAIRS, 2, 2 * SSM_GB_ST), BATCH, axis=1)

    y = _s5_scan(z.reshape(BATCH, SEQ, Z_COLS), wb, a8, wc, ssm_d[0].reshape(1, SSM_WIDTH))
    o_ssm = _glu(y.reshape(TOKENS, SSM_WIDTH), w_glu[0].astype(BF16), b_glu[0].reshape(1, SSM_WIDTH),
                 ssm_out_gain[0].reshape(1, SSM_WIDTH))

    x1 = _out_proj(o_mla, o_ssm, mla_out_gain[0].reshape(1, MLA_WIDTH), w_out[0].astype(BF16), x2, mod3, 2)

    w_r = jnp.concatenate([w_group_router[0], w_expert_router[0],
                           jnp.zeros((D_MODEL, LANES - N_EGROUPS - N_EXPERTS), F32)], axis=1)
    w_r_hi = w_r.astype(BF16)
    w_r_lo = (w_r - w_r_hi.astype(F32)).astype(BF16)
    b_r = jnp.concatenate([b_group_router[0], b_expert_router[0],
                           jnp.zeros((LANES - N_EGROUPS - N_EXPERTS,), F32)]).reshape(1, LANES)
    h2, eid, wt = _router(x1, norm_ffn_gain[0].reshape(1, D_MODEL), mod3, 3, 4, w_r_hi, w_r_lo, b_r)

    row_dst, blk_e, blk_nvalid, n_used = _moe_plan(eid[:, :TOP_K])
    hmid = _expert_up(row_dst, blk_e, blk_nvalid, n_used, h2, w1_experts[0], w3_experts[0])
    ys = _expert_down(row_dst, blk_e, blk_nvalid, n_used, hmid, w2_experts[0])
    out = _combine(ys, x1, wt, mod3, 5, final_gain.reshape(1, D_MODEL))
    return out.reshape(BATCH, SEQ, D_MODEL)
```

```python
import math

import jax
import jax.numpy as jnp
from jax import lax
from jax.experimental import pallas as pl
from jax.experimental.pallas import tpu as pltpu

F32 = jnp.float32
BF16 = jnp.bfloat16

D_MODEL = 4096
BATCH = 4
SEQ = 2048
TOKENS = BATCH * SEQ
CHUNK = 64
EPS = 1e-6

MLA_HEADS = 16
QK_NOPE = 128
QK_ROPE = 64
QK_HEAD = QK_NOPE + QK_ROPE
V_HEAD = 128
Q_RANK = 768
KV_RANK = 512
ROPE_THETA = 10000.0
MLA_WIDTH = MLA_HEADS * V_HEAD

SSM_WIDTH = D_MODEL - MLA_WIDTH
SSM_GROUP_CH = 16
SSM_GROUPS = SSM_WIDTH // SSM_GROUP_CH
SSM_STATE = 64

N_EGROUPS = 8
EXPERTS_PER_GROUP = 8
N_EXPERTS = N_EGROUPS * EXPERTS_PER_GROUP
TOP_K = 2
D_EXPERT = 512

LANES = 128
SUBLANES = 8
VMEM_LIMIT = 56 * 1024 * 1024

Z_Q = 0
Z_SSM = Q_RANK
Z_KV = Z_SSM + SSM_WIDTH
Z_KR = Z_KV + KV_RANK
Z_COLS = Z_KR + 2 * QK_ROPE

ATT_BLK = 256
ATT_NBLK = SEQ // ATT_BLK

SSM_GB = 8
SSM_GB_CH = SSM_GB * SSM_GROUP_CH
SSM_GB_ST = SSM_GB * SSM_STATE
SSM_PAIRS = SSM_GROUPS // (2 * SSM_GB)
SSM_TCHUNK = 256

MOE_BLK = 384
MOE_NBLK = (TOKENS * TOP_K + N_EXPERTS * (MOE_BLK - 1)) // MOE_BLK + 1
MOE_ROWS = MOE_NBLK * MOE_BLK
DMA_UNROLL = 8
CMB_TM = 128


def _cparams(n_axes, vmem=VMEM_LIMIT):
    return pltpu.CompilerParams(dimension_semantics=("arbitrary",) * n_axes, vmem_limit_bytes=vmem)


def _ada_kernel(c_ref, w_ref, b_ref, o_ref):
    c = c_ref[...]
    ca = (c * jax.nn.sigmoid(c)).astype(BF16)
    o_ref[...] = jnp.dot(ca, w_ref[...].astype(BF16), preferred_element_type=F32) + b_ref[...]


def _ada(c8, w, b):
    n = w.shape[1]
    tn = 512
    return pl.pallas_call(
        _ada_kernel,
        grid=(n // tn,),
        in_specs=[pl.BlockSpec((SUBLANES, D_MODEL), lambda j: (0, 0)),
                  pl.BlockSpec((D_MODEL, tn), lambda j: (0, j)),
                  pl.BlockSpec((1, tn), lambda j: (0, j))],
        out_specs=pl.BlockSpec((SUBLANES, tn), lambda j: (0, j)),
        out_shape=jax.ShapeDtypeStruct((SUBLANES, n), F32),
        compiler_params=_cparams(1),
        name="ada_mod",
    )(c8, w, b)


def _rms(x, gain):
    return x * lax.rsqrt(jnp.mean(x * x, axis=-1, keepdims=True) + EPS) * gain


def _in_proj_kernel(x_ref, g_ref, sh_ref, sc_ref, w_ref, o_ref, h_ref):
    @pl.when(pl.program_id(1) == 0)
    def _():
        y = _rms(x_ref[...], g_ref[...])
        h_ref[...] = (y * (1.0 + sc_ref[...]) + sh_ref[...]).astype(h_ref.dtype)

    o_ref[...] = lax.dot_general(h_ref[...], w_ref[...], (((1,), (1,)), ((), ())), preferred_element_type=F32)


def _in_proj(x2, gain, mod3, shift_chunk, scale_chunk, w_t):
    tm = 512
    n = w_t.shape[0]
    tn = n // 3
    per_b = SEQ // tm
    return pl.pallas_call(
        _in_proj_kernel,
        grid=(TOKENS // tm, n // tn),
        in_specs=[pl.BlockSpec((tm, D_MODEL), lambda i, j: (i, 0)),
                  pl.BlockSpec((1, D_MODEL), lambda i, j: (0, 0)),
                  pl.BlockSpec((None, 1, D_MODEL), lambda i, j: (i // per_b, 0, shift_chunk)),
                  pl.BlockSpec((None, 1, D_MODEL), lambda i, j: (i // per_b, 0, scale_chunk)),
                  pl.BlockSpec((tn, D_MODEL), lambda i, j: (j, 0))],
        out_specs=pl.BlockSpec((tm, tn), lambda i, j: (i, j)),
        out_shape=jax.ShapeDtypeStruct((TOKENS, n), F32),
        scratch_shapes=[pltpu.VMEM((tm, D_MODEL), BF16)],
        compiler_params=_cparams(2),
        name="in_proj",
    )(x2, gain, mod3, mod3, w_t)


W_IN_KV = Q_RANK
W_IN_KR = Q_RANK + KV_RANK
W_IN_SSM = W_IN_KR + QK_ROPE
W_IN_COLS = W_IN_SSM + SSM_WIDTH


def _wz_kernel(w_ref, o_ref):
    o_ref[Z_Q:Z_Q + Q_RANK] = w_ref[:W_IN_KV].astype(o_ref.dtype)
    o_ref[Z_SSM:Z_SSM + SSM_WIDTH] = w_ref[W_IN_SSM:].astype(o_ref.dtype)
    o_ref[Z_KV:Z_KV + KV_RANK] = w_ref[W_IN_KV:W_IN_KR].astype(o_ref.dtype)
    half = QK_ROPE // 2
    o_ref[Z_KR:Z_KR + QK_ROPE] = w_ref[W_IN_KR:W_IN_SSM].astype(o_ref.dtype)
    o_ref[Z_KR + QK_ROPE:Z_KR + QK_ROPE + half] = (-w_ref[W_IN_KR + half:W_IN_SSM]).astype(o_ref.dtype)
    o_ref[Z_KR + QK_ROPE + half:] = w_ref[W_IN_KR:W_IN_KR + half].astype(o_ref.dtype)


def _wz_prep(w_in_t):
    tk = 512
    return pl.pallas_call(
        _wz_kernel,
        grid=(D_MODEL // tk,),
        in_specs=[pl.BlockSpec((W_IN_COLS, tk), lambda i: (0, i))],
        out_specs=pl.BlockSpec((Z_COLS, tk), lambda i: (0, i)),
        out_shape=jax.ShapeDtypeStruct((Z_COLS, D_MODEL), BF16),
        compiler_params=_cparams(1),
        name="in_proj_weight_prep",
    )(w_in_t)


def _rope_table_kernel(pos_ref, o_ref):
    lane = lax.broadcasted_iota(jnp.int32, (1, LANES), 1)
    pair = (lane % (QK_ROPE // 2)).astype(F32)
    inv_freq = jnp.exp(-math.log(ROPE_THETA) * (2.0 * pair) / QK_ROPE)
    ang = pos_ref[...].astype(F32) * inv_freq
    o_ref[...] = jnp.where(lane < QK_ROPE, jnp.cos(ang), jnp.sin(ang))


def _rope_table(pos_col):
    tm = 1024
    return pl.pallas_call(
        _rope_table_kernel,
        grid=(TOKENS // tm,),
        in_specs=[pl.BlockSpec((tm, 1), lambda i: (i, 0))],
        out_specs=pl.BlockSpec((tm, LANES), lambda i: (i, 0)),
        out_shape=jax.ShapeDtypeStruct((TOKENS, LANES), F32),
        compiler_params=_cparams(1),
        name="rope_table",
    )(pos_col)


def _rope_pair(t, cs):
    u = t * cs
    return u + pltpu.roll(u, QK_ROPE, axis=1)


def _q_proj_kernel(ql_ref, g_ref, w_ref, cs_ref, o_ref):
    hn = _rms(ql_ref[...], g_ref[...]).astype(BF16)
    cs = cs_ref[...]
    for h in range(MLA_HEADS):
        r = jnp.dot(hn, w_ref[:, h * 2 * LANES:(h + 1) * 2 * LANES], preferred_element_type=F32)
        o_ref[h, :, :QK_NOPE] = r[:, :QK_NOPE].astype(o_ref.dtype)
        o_ref[h, :, QK_NOPE:] = _rope_pair(r[:, QK_NOPE:], cs)[:, :QK_ROPE].astype(o_ref.dtype)


def _q_proj(z, gain, w_heads, cs):
    tm = ATT_BLK
    per_b = SEQ // tm
    return pl.pallas_call(
        _q_proj_kernel,
        grid=(TOKENS // tm,),
        in_specs=[pl.BlockSpec((tm, Q_RANK), lambda i: (i, Z_Q // Q_RANK)),
                  pl.BlockSpec((1, Q_RANK), lambda i: (0, 0)),
                  pl.BlockSpec((Q_RANK, MLA_HEADS * 2 * LANES), lambda i: (0, 0)),
                  pl.BlockSpec((tm, LANES), lambda i: (i, 0))],
        out_specs=pl.BlockSpec((None, MLA_HEADS, tm, QK_HEAD), lambda i: (i // per_b, 0, i % per_b, 0)),
        out_shape=jax.ShapeDtypeStruct((BATCH, MLA_HEADS, SEQ, QK_HEAD), BF16),
        compiler_params=_cparams(1),
        name="mla_q_proj",
    )(z, gain, w_heads, cs)


def _kv_proj_kernel(kva_ref, kvb_ref, kr_ref, g_ref, wk_ref, wvt_ref, cs_ref, k_ref, vt_ref):
    kvl = jnp.concatenate([kva_ref[...], kvb_ref[...]], axis=1)
    hn = _rms(kvl, g_ref[...]).astype(BF16)
    kr = _rope_pair(kr_ref[...], cs_ref[...])[:, :QK_ROPE].astype(k_ref.dtype)
    for h in range(MLA_HEADS):
        k_ref[h, :, :QK_NOPE] = jnp.dot(hn, wk_ref[:, h * QK_NOPE:(h + 1) * QK_NOPE],
                                        preferred_element_type=F32).astype(k_ref.dtype)
        k_ref[h, :, QK_NOPE:] = kr
        vt_ref[h] = lax.dot_general(wvt_ref[h], hn, (((1,), (1,)), ((), ())),
                                    preferred_element_type=F32).astype(vt_ref.dtype)


def _kv_proj(z, gain, wk_heads, wvt_heads, cs):
    tm = ATT_BLK
    per_b = SEQ // tm
    half = KV_RANK // 2
    return pl.pallas_call(
        _kv_proj_kernel,
        grid=(TOKENS // tm,),
        in_specs=[pl.BlockSpec((tm, half), lambda i: (i, Z_KV // half)),
                  pl.BlockSpec((tm, half), lambda i: (i, Z_KV // half + 1)),
                  pl.BlockSpec((tm, LANES), lambda i: (i, Z_KR // LANES)),
                  pl.BlockSpec((1, KV_RANK), lambda i: (0, 0)),
                  pl.BlockSpec((KV_RANK, MLA_HEADS * QK_NOPE), lambda i: (0, 0)),
                  pl.BlockSpec((MLA_HEADS, V_HEAD, KV_RANK), lambda i: (0, 0, 0)),
                  pl.BlockSpec((tm, LANES), lambda i: (i, 0))],
        out_specs=[pl.BlockSpec((None, MLA_HEADS, tm, QK_HEAD), lambda i: (i // per_b, 0, i % per_b, 0)),
                   pl.BlockSpec((None, MLA_HEADS, None, V_HEAD, tm), lambda i: (i // per_b, 0, i % per_b, 0, 0))],
        out_shape=[jax.ShapeDtypeStruct((BATCH, MLA_HEADS, SEQ, QK_HEAD), BF16),
                   jax.ShapeDtypeStruct((BATCH, MLA_HEADS, ATT_NBLK, V_HEAD, ATT_BLK), BF16)],
        compiler_params=_cparams(1),
        name="mla_kv_proj",
    )(z, z, z, gain, wk_heads, wvt_heads, cs)


ATT_HEADS = 4


def _attn_kernel(q_ref, k_ref, vt_ref, o_ref, acc_ref):
    log2_scale = (QK_HEAD ** -0.5) * math.log2(math.e)
    key_chunk = lax.broadcasted_iota(jnp.int32, (ATT_BLK, ATT_BLK), 0) // CHUNK
    qry_chunk = lax.broadcasted_iota(jnp.int32, (ATT_BLK, ATT_BLK), 1) // CHUNK
    diag_mask = key_chunk <= qry_chunk

    def q_body(qi, carry):
        q0 = pl.multiple_of(qi * ATT_BLK, ATT_BLK)

        def kv_block(j, stats, n_blk, masked):
            keys = n_blk * ATT_BLK
            k0 = pl.multiple_of(j * ATT_BLK, ATT_BLK)
            sts = [lax.dot_general(k_ref[h, pl.ds(k0, keys), :], q_ref[h, pl.ds(q0, ATT_BLK), :],
                                   (((1,), (1,)), ((), ())), preferred_element_type=F32)
                   for h in range(ATT_HEADS)]
            new_stats, ps, alphas = [], [], []
            for h in range(ATT_HEADS):
                m, l = stats[h]
                st = sts[h] * log2_scale
                if masked:
                    mask = diag_mask if n_blk == 1 else jnp.concatenate(
                        [jnp.ones(((n_blk - 1) * ATT_BLK, ATT_BLK), jnp.bool_), diag_mask], axis=0)
                    st = jnp.where(mask, st, -jnp.inf)
                m_new = jnp.maximum(m, jnp.max(st, axis=0, keepdims=True))
                alpha = jnp.exp2(m - m_new)
                p = jnp.exp2(st - m_new)
                new_stats.append((m_new, alpha * l + jnp.sum(p, axis=0, keepdims=True)))
                ps.append(p.astype(BF16))
                alphas.append(alpha)
            for h in range(ATT_HEADS):
                pv = jnp.dot(vt_ref[h, j], ps[h][:ATT_BLK], preferred_element_type=F32)
                for b in range(1, n_blk):
                    pv = pv + jnp.dot(vt_ref[h, j + b], ps[h][b * ATT_BLK:(b + 1) * ATT_BLK],
                                      preferred_element_type=F32)
                acc_ref[h] = alphas[h] * acc_ref[h] + pv
            return tuple(new_stats)

        acc_ref[...] = jnp.zeros_like(acc_ref)
        stats = tuple((jnp.full((1, ATT_BLK), -jnp.inf, F32), jnp.zeros((1, ATT_BLK), F32))
                      for _ in range(ATT_HEADS))
        n_pairs = lax.shift_right_logical(qi, 1)
        stats = lax.fori_loop(0, n_pairs, lambda jj, st: kv_block(2 * jj, st, 2, False), stats)
        stats = lax.cond(qi % 2 == 1, lambda st: kv_block(qi - 1, st, 2, True),
                         lambda st: kv_block(qi, st, 1, True), stats)
        for h in range(ATT_HEADS):
            o_ref[pl.ds(q0, ATT_BLK), h * V_HEAD:(h + 1) * V_HEAD] = (acc_ref[h] / stats[h][1]).T
        return carry

    lax.fori_loop(0, ATT_NBLK, q_body, 0)


def _attention(q, k, vt):
    return pl.pallas_call(
        _attn_kernel,
        grid=(BATCH, MLA_HEADS // ATT_HEADS),
        in_specs=[pl.BlockSpec((None, ATT_HEADS, SEQ, QK_HEAD), lambda b, h: (b, h, 0, 0)),
                  pl.BlockSpec((None, ATT_HEADS, SEQ, QK_HEAD), lambda b, h: (b, h, 0, 0)),
                  pl.BlockSpec((None, ATT_HEADS, ATT_NBLK, V_HEAD, ATT_BLK), lambda b, h: (b, h, 0, 0, 0))],
        out_specs=pl.BlockSpec((SEQ, ATT_HEADS * V_HEAD), lambda b, h: (b, h)),
        out_shape=jax.ShapeDtypeStruct((TOKENS, MLA_WIDTH), F32),
        scratch_shapes=[pltpu.VMEM((ATT_HEADS, V_HEAD, ATT_BLK), F32)],
        compiler_params=_cparams(2),
        name="mla_attention",
    )(q, k, vt)


def _s5_disc_kernel(lre_ref, lim_ref, ldt_ref, bre_ref, bim_ref, are_ref, aim_ref, bbre_ref, bbim_ref):
    lam_re = lre_ref[...]
    lam_im = lim_ref[...]
    dt = jnp.exp(ldt_ref[...])
    mag = jnp.exp(lam_re * dt)
    abar_re = mag * jnp.cos(lam_im * dt)
    abar_im = mag * jnp.sin(lam_im * dt)
    nr = abar_re - 1.0
    ni = abar_im
    den = lam_re * lam_re + lam_im * lam_im
    f_re = (nr * lam_re + ni * lam_im) / den
    f_im = (ni * lam_re - nr * lam_im) / den
    are_ref[...] = abar_re
    aim_ref[...] = abar_im
    b_re = bre_ref[...]
    b_im = bim_ref[...]
    bbre_ref[...] = f_re * b_re - f_im * b_im
    bbim_ref[...] = f_re * b_im + f_im * b_re


def _s5_disc(lam_re, lam_im, log_dt, b_re, b_im):
    n = SSM_GROUPS * SSM_STATE
    tm = 1024
    col = pl.BlockSpec((tm, 1), lambda i: (i, 0))
    mat = pl.BlockSpec((tm, SSM_GROUP_CH), lambda i: (i, 0))
    return pl.pallas_call(
        _s5_disc_kernel,
        grid=(n // tm,),
        in_specs=[col, col, col, mat, mat],
        out_specs=[col, col, mat, mat],
        out_shape=[jax.ShapeDtypeStruct((n, 1), F32), jax.ShapeDtypeStruct((n, 1), F32),
                   jax.ShapeDtypeStruct((n, SSM_GROUP_CH), F32), jax.ShapeDtypeStruct((n, SSM_GROUP_CH), F32)],
        compiler_params=_cparams(1),
        name="s5_discretise",
    )(lam_re, lam_im, log_dt, b_re, b_im)


def _s5_scan_kernel(u_ref, wb_ref, a_ref, wc_ref, d_ref, y_ref, lhs_ref, s_ref, yb_ref, h_ref):
    @pl.when(pl.program_id(1) == 0)
    def _():
        h_ref[...] = jnp.zeros_like(h_ref)
        lhs_ref[...] = jnp.zeros_like(lhs_ref)

    for b in range(BATCH):
        lhs_ref[0, pl.ds(b, SSM_TCHUNK, stride=SUBLANES), :] = u_ref[b, :, :SSM_GB_CH]
        lhs_ref[1, pl.ds(b + BATCH, SSM_TCHUNK, stride=SUBLANES), :] = u_ref[b, :, SSM_GB_CH:]
    half = SSM_TCHUNK * SUBLANES // 2
    wb = wb_ref[...]
    for r0 in (0, half):
        um = jnp.concatenate([lhs_ref[0, r0:r0 + half], lhs_ref[1, r0:r0 + half]], axis=1).astype(BF16)
        s_ref[r0:r0 + half] = jnp.dot(um, wb, preferred_element_type=F32)

    a_re = a_ref[:, :SSM_GB_ST]
    a_im = a_ref[:, SSM_GB_ST:]

    def step(t, carry):
        h_re, h_im = carry
        r0 = pl.multiple_of(t * SUBLANES, SUBLANES)
        n_re = a_re * h_re - a_im * h_im + s_ref[pl.ds(r0, SUBLANES), :SSM_GB_ST]
        n_im = a_re * h_im + a_im * h_re + s_ref[pl.ds(r0, SUBLANES), SSM_GB_ST:]
        s_ref[pl.ds(r0, SUBLANES), :SSM_GB_ST] = n_re
        s_ref[pl.ds(r0, SUBLANES), SSM_GB_ST:] = n_im
        return n_re, n_im

    h_re, h_im = lax.fori_loop(0, SSM_TCHUNK, step, (h_ref[:, :SSM_GB_ST], h_ref[:, SSM_GB_ST:]), unroll=8)
    h_ref[:, :SSM_GB_ST] = h_re
    h_ref[:, SSM_GB_ST:] = h_im

    wc = wc_ref[...]
    for r0 in (0, half):
        y8 = jnp.dot(s_ref[r0:r0 + half].astype(BF16), wc, preferred_element_type=F32)
        yb_ref[0, r0:r0 + half] = y8[:, :SSM_GB_CH]
        yb_ref[1, r0:r0 + half] = y8[:, SSM_GB_CH:]
    d = d_ref[...]
    for b in range(BATCH):
        y_ref[b, :, :SSM_GB_CH] = (yb_ref[0, pl.ds(b, SSM_TCHUNK, stride=SUBLANES), :]
                                   + d[:, :SSM_GB_CH] * u_ref[b, :, :SSM_GB_CH])
        y_ref[b, :, SSM_GB_CH:] = (yb_ref[1, pl.ds(b + BATCH, SSM_TCHUNK, stride=SUBLANES), :]
                                   + d[:, SSM_GB_CH:] * u_ref[b, :, SSM_GB_CH:])


def _s5_scan(z3, wb, a8, wc, d_row):
    rows = SSM_TCHUNK * SUBLANES
    width = 2 * SSM_GB_CH
    states = 2 * SSM_GB_ST
    return pl.pallas_call(
        _s5_scan_kernel,
        grid=(SSM_PAIRS, SEQ // SSM_TCHUNK),
        in_specs=[pl.BlockSpec((BATCH, SSM_TCHUNK, width), lambda g, c: (0, c, Z_SSM // width + g)),
                  pl.BlockSpec((None, width, states), lambda g, c: (g, 0, 0)),
                  pl.BlockSpec((None, SUBLANES, states), lambda g, c: (g, 0, 0)),
                  pl.BlockSpec((None, states, width), lambda g, c: (g, 0, 0)),
                  pl.BlockSpec((1, width), lambda g, c: (0, g))],
        out_specs=pl.BlockSpec((BATCH, SSM_TCHUNK, width), lambda g, c: (0, c, g)),
        out_shape=jax.ShapeDtypeStruct((BATCH, SEQ, SSM_WIDTH), F32),
        scratch_shapes=[pltpu.VMEM((2, rows, SSM_GB_CH), F32), pltpu.VMEM((rows, states), F32),
                        pltpu.VMEM((2, rows, SSM_GB_CH), F32), pltpu.VMEM((SUBLANES, states), F32)],
        compiler_params=_cparams(2),
        name="s5_scan",
    )(z3, wb, a8, wc, d_row)


def _gelu_tanh(x):
    return 0.5 * x * (1.0 + jnp.tanh(math.sqrt(2.0 / math.pi) * (x + 0.044715 * (x * x * x))))


def _glu_kernel(y_ref, w_ref, b_ref, g_ref, o_ref):
    g = _gelu_tanh(y_ref[...])
    gate = jax.nn.sigmoid(jnp.dot(g.astype(BF16), w_ref[...], preferred_element_type=F32) + b_ref[...])
    o_ref[...] = _rms(g * gate, g_ref[...]).astype(o_ref.dtype)


def _glu(y, w, b, gain):
    tm = 256
    return pl.pallas_call(
        _glu_kernel,
        grid=(TOKENS // tm,),
        in_specs=[pl.BlockSpec((tm, SSM_WIDTH), lambda i: (i, 0)),
                  pl.BlockSpec((SSM_WIDTH, SSM_WIDTH), lambda i: (0, 0)),
                  pl.BlockSpec((1, SSM_WIDTH), lambda i: (0, 0)),
                  pl.BlockSpec((1, SSM_WIDTH), lambda i: (0, 0))],
        out_specs=pl.BlockSpec((tm, SSM_WIDTH), lambda i: (i, 0)),
        out_shape=jax.ShapeDtypeStruct((TOKENS, SSM_WIDTH), BF16),
        compiler_params=_cparams(1),
        name="s5_glu",
    )(y, w, b, gain)


def _out_proj_kernel(om_ref, os_ref, g_ref, w_ref, x_ref, ga_ref, o_ref, lhs_ref):
    @pl.when(pl.program_id(1) == 0)
    def _():
        lhs_ref[:, :MLA_WIDTH] = _rms(om_ref[...], g_ref[...]).astype(BF16)
        lhs_ref[:, MLA_WIDTH:] = os_ref[...]

    o_ref[...] = x_ref[...] + ga_ref[...] * jnp.dot(lhs_ref[...], w_ref[...], preferred_element_type=F32)


def _out_proj(o_mla, o_ssm, gain, w, x2, mod3, gate_chunk):
    tm, tn = 512, 1024
    per_b = SEQ // tm
    nj = D_MODEL // tn
    return pl.pallas_call(
        _out_proj_kernel,
        grid=(TOKENS // tm, nj),
        in_specs=[pl.BlockSpec((tm, MLA_WIDTH), lambda i, j: (i, 0)),
                  pl.BlockSpec((tm, SSM_WIDTH), lambda i, j: (i, 0)),
                  pl.BlockSpec((1, MLA_WIDTH), lambda i, j: (0, 0)),
                  pl.BlockSpec((D_MODEL, tn), lambda i, j: (0, j)),
                  pl.BlockSpec((tm, tn), lambda i, j: (i, j)),
                  pl.BlockSpec((None, 1, tn), lambda i, j: (i // per_b, 0, gate_chunk * nj + j))],
        out_specs=pl.BlockSpec((tm, tn), lambda i, j: (i, j)),
        out_shape=jax.ShapeDtypeStruct((TOKENS, D_MODEL), F32),
        scratch_shapes=[pltpu.VMEM((tm, D_MODEL), BF16)],
        compiler_params=_cparams(2),
        name="out_proj",
    )(o_mla, o_ssm, gain, w, x2, mod3)


def _router_kernel(x_ref, g_ref, sh_ref, sc_ref, whi_ref, wlo_ref, br_ref, h_ref, eid_ref, wt_ref):
    h = _rms(x_ref[...], g_ref[...]) * (1.0 + sc_ref[...]) + sh_ref[...]
    h_ref[...] = h
    h_hi = h.astype(BF16)
    h_lo = (h - h_hi.astype(F32)).astype(BF16)

    whi = whi_ref[...]
    logits = (jnp.dot(h_hi, whi, preferred_element_type=F32)
              + jnp.dot(h_lo, whi, preferred_element_type=F32)
              + jnp.dot(h_hi, wlo_ref[...], preferred_element_type=F32)
              + br_ref[...])

    col = lax.broadcasted_iota(jnp.int32, logits.shape, 1)
    neg = -jnp.inf
    is_g = col < N_EGROUPS
    gl = jnp.where(is_g, logits, neg)
    gmax = jnp.max(gl, axis=1, keepdims=True)
    grp = jnp.min(jnp.where(gl == gmax, col, LANES), axis=1, keepdims=True)
    p_grp = 1.0 / jnp.sum(jnp.where(is_g, jnp.exp(logits - gmax), 0.0), axis=1, keepdims=True)

    ecol = col - N_EGROUPS
    in_grp = (ecol >= 0) & (ecol < N_EXPERTS) & ((ecol // EXPERTS_PER_GROUP) == grp)
    el = jnp.where(in_grp, logits, neg)
    v1 = jnp.max(el, axis=1, keepdims=True)
    i1 = jnp.min(jnp.where(el == v1, col, LANES), axis=1, keepdims=True)
    el2 = jnp.where(col == i1, neg, el)
    v2 = jnp.max(el2, axis=1, keepdims=True)
    i2 = jnp.min(jnp.where(el2 == v2, col, LANES), axis=1, keepdims=True)

    s2 = jnp.exp(v2 - v1)
    w1 = p_grp / (1.0 + s2)
    w2 = p_grp * s2 / (1.0 + s2)
    eid_ref[...] = jnp.where(col == 0, i1 - N_EGROUPS, jnp.where(col == 1, i2 - N_EGROUPS, 0))
    wt_ref[...] = jnp.where(col == 0, w1, jnp.where(col == 1, w2, 0.0))


def _router(x1, gain, mod3, shift_chunk, scale_chunk, w_hi, w_lo, b_row):
    tm = 256
    per_b = SEQ // tm
    return pl.pallas_call(
        _router_kernel,
        grid=(TOKENS // tm,),
        in_specs=[pl.BlockSpec((tm, D_MODEL), lambda i: (i, 0)),
                  pl.BlockSpec((1, D_MODEL), lambda i: (0, 0)),
                  pl.BlockSpec((None, 1, D_MODEL), lambda i: (i // per_b, 0, shift_chunk)),
                  pl.BlockSpec((None, 1, D_MODEL), lambda i: (i // per_b, 0, scale_chunk)),
                  pl.BlockSpec((D_MODEL, LANES), lambda i: (0, 0)),
                  pl.BlockSpec((D_MODEL, LANES), lambda i: (0, 0)),
                  pl.BlockSpec((1, LANES), lambda i: (0, 0))],
        out_specs=[pl.BlockSpec((tm, D_MODEL), lambda i: (i, 0)),
                   pl.BlockSpec((tm, LANES), lambda i: (i, 0)),
                   pl.BlockSpec((tm, LANES), lambda i: (i, 0))],
        out_shape=[jax.ShapeDtypeStruct((TOKENS, D_MODEL), F32),
                   jax.ShapeDtypeStruct((TOKENS, LANES), jnp.int32),
                   jax.ShapeDtypeStruct((TOKENS, LANES), F32)],
        compiler_params=_cparams(1),
        name="ffn_router",
    )(x1, gain, mod3, mod3, w_hi, w_lo, b_row)


def _moe_plan(eid):
    n_assign = TOKENS * TOP_K
    e_flat = eid.reshape(n_assign)
    onehot = (e_flat[:, None] == jnp.arange(N_EXPERTS, dtype=jnp.int32)[None, :]).astype(jnp.int32)
    csum = jnp.cumsum(onehot, axis=0)
    counts = csum[-1]
    rank = jnp.sum((csum - onehot) * onehot, axis=1)
    nblk_e = (counts + MOE_BLK - 1) // MOE_BLK
    bend = jnp.cumsum(nblk_e)
    bstart = bend - nblk_e
    dest = jnp.sum(onehot * bstart[None, :], axis=1) * MOE_BLK + rank
    a_id = jnp.arange(n_assign, dtype=jnp.int32)
    row_dst = jnp.zeros((MOE_ROWS,), jnp.int32).at[dest].set((a_id % TOP_K) * TOKENS + a_id // TOP_K,
                                                              unique_indices=True)
    blk = jnp.arange(MOE_NBLK, dtype=jnp.int32)
    n_used = bend[-1]
    blk_e = jnp.minimum(jnp.sum((bend[None, :] <= blk[:, None]).astype(jnp.int32), axis=1), N_EXPERTS - 1)
    blk_nvalid = jnp.clip(counts[blk_e] - (blk - bstart[blk_e]) * MOE_BLK, 0, MOE_BLK)
    blk_nvalid = jnp.where(blk < n_used, blk_nvalid, 0).astype(jnp.int32)
    last_e = blk_e[jnp.maximum(n_used - 1, 0)]
    blk_e = jnp.where(blk < n_used, blk_e, last_e).astype(jnp.int32)
    return row_dst, blk_e, blk_nvalid, n_used.reshape(1).astype(jnp.int32)


def _copy_groups(n_rows):
    return lax.shift_right_logical(n_rows + (DMA_UNROLL - 1), DMA_UNROLL.bit_length() - 1)


def _expert_up_kernel(dst_ref, be_ref, nv_ref, nu_ref, h_hbm, w1_ref, w3_ref, o_ref, xbuf, sem):
    i = pl.program_id(0)
    slot = i % 2

    def row_copy(src_row, dst_slot, dst_row):
        return pltpu.make_async_copy(h_hbm.at[pl.ds(src_row, 1)], xbuf.at[dst_slot, pl.ds(dst_row, 1)],
                                     sem.at[dst_slot])

    def issue(block, dst_slot):
        base = block * MOE_BLK

        def body(g, c):
            for u in range(DMA_UNROLL):
                r = g * DMA_UNROLL + u
                row_copy(dst_ref[base + r] & (TOKENS - 1), dst_slot, r).start()
            return c

        lax.fori_loop(0, _copy_groups(nv_ref[block]), body, 0)

    @pl.when(i == 0)
    def _():
        xbuf[...] = jnp.zeros_like(xbuf)
        issue(0, 0)

    @pl.when(i + 1 < pl.num_programs(0))
    def _():
        issue(i + 1, 1 - slot)

    def wait_body(g, c):
        for u in range(DMA_UNROLL):
            row_copy(0, slot, g * DMA_UNROLL + u).wait()
        return c

    lax.fori_loop(0, _copy_groups(nv_ref[i]), wait_body, 0)

    @pl.when(i < nu_ref[0])
    def _():
        x = xbuf[slot].astype(BF16)
        a = jnp.dot(x, w1_ref[...].astype(BF16), preferred_element_type=F32)
        b = jnp.dot(x, w3_ref[...].astype(BF16), preferred_element_type=F32)
        o_ref[...] = (a * jax.nn.sigmoid(a) * b).astype(o_ref.dtype)

    @pl.when(i >= nu_ref[0])
    def _():
        o_ref[...] = jnp.zeros_like(o_ref)


def _expert_up(row_dst, blk_e, blk_nvalid, n_used, h2, w1, w3):
    wspec = pl.BlockSpec((None, D_MODEL, D_EXPERT), lambda i, tok, be, ng, nu: (be[i], 0, 0))
    return pl.pallas_call(
        _expert_up_kernel,
        grid_spec=pltpu.PrefetchScalarGridSpec(
            num_scalar_prefetch=4,
            grid=(MOE_NBLK,),
            in_specs=[pl.BlockSpec(memory_space=pl.ANY), wspec, wspec],
            out_specs=pl.BlockSpec((MOE_BLK, D_EXPERT), lambda i, tok, be, ng, nu: (i, 0)),
            scratch_shapes=[pltpu.VMEM((2, MOE_BLK, D_MODEL), F32),
                            pltpu.SemaphoreType.DMA((2,))]),
        out_shape=jax.ShapeDtypeStruct((MOE_ROWS, D_EXPERT), BF16),
        compiler_params=_cparams(1, 60 * 1024 * 1024),
        name="moe_expert_up",
    )(row_dst, blk_e, blk_nvalid, n_used, h2, w1, w3)


def _expert_down_kernel(dst_ref, be_ref, nv_ref, nu_ref, h_ref, w_ref, y_hbm, ybuf, sem):
    i = pl.program_id(0)
    slot = i % 2

    def row_copy(src_row, dst_row, src_slot):
        return pltpu.make_async_copy(ybuf.at[src_slot, pl.ds(src_row, 1)], y_hbm.at[pl.ds(dst_row, 1)],
                                     sem.at[src_slot])

    def drain(block, src_slot):
        nv = nv_ref[block]
        full = lax.shift_right_logical(nv, DMA_UNROLL.bit_length() - 1)

        def body(g, c):
            for u in range(DMA_UNROLL):
                row_copy(g * DMA_UNROLL + u, 0, src_slot).wait()
            return c

        lax.fori_loop(0, full, body, 0)

        def tail(r, c):
            row_copy(r, 0, src_slot).wait()
            return c

        lax.fori_loop(full * DMA_UNROLL, nv, tail, 0)

    @pl.when(i >= 2)
    def _():
        drain(i - 2, slot)

    @pl.when(i < nu_ref[0])
    def _():
        ybuf[slot] = jnp.dot(h_ref[...], w_ref[...].astype(BF16), preferred_element_type=F32)
        base = i * MOE_BLK
        nv = nv_ref[i]
        full = lax.shift_right_logical(nv, DMA_UNROLL.bit_length() - 1)

        def body(g, c):
            for u in range(DMA_UNROLL):
                r = g * DMA_UNROLL + u
                row_copy(r, dst_ref[base + r], slot).start()
            return c

        lax.fori_loop(0, full, body, 0)

        def tail(r, c):
            row_copy(r, dst_ref[base + r], slot).start()
            return c

        lax.fori_loop(full * DMA_UNROLL, nv, tail, 0)

    @pl.when(i == pl.num_programs(0) - 1)
    def _():
        drain(i - 1, 1 - slot)
        drain(i, slot)


def _expert_down(row_dst, blk_e, blk_nvalid, n_used, hmid, w2):
    return pl.pallas_call(
        _expert_down_kernel,
        grid_spec=pltpu.PrefetchScalarGridSpec(
            num_scalar_prefetch=4,
            grid=(MOE_NBLK,),
            in_specs=[pl.BlockSpec((MOE_BLK, D_EXPERT), lambda i, dst, be, nv, nu: (i, 0)),
                      pl.BlockSpec((None, D_EXPERT, D_MODEL), lambda i, dst, be, nv, nu: (be[i], 0, 0))],
            out_specs=pl.BlockSpec(memory_space=pl.ANY),
            scratch_shapes=[pltpu.VMEM((2, MOE_BLK, D_MODEL), F32),
                            pltpu.SemaphoreType.DMA((2,))]),
        out_shape=jax.ShapeDtypeStruct((TOP_K * TOKENS, D_MODEL), F32),
        compiler_params=_cparams(1),
        name="moe_expert_down",
    )(row_dst, blk_e, blk_nvalid, n_used, hmid, w2)


def _combine_kernel(y0_ref, y1_ref, x_ref, wt_ref, gf_ref, fg_ref, o_ref):
    wt = wt_ref[...]
    moe = wt[:, 0:1] * y0_ref[...] + wt[:, 1:2] * y1_ref[...]
    o_ref[...] = _rms(x_ref[...] + gf_ref[...] * moe, fg_ref[...])


def _combine(ys, x1, wt, mod3, gate_chunk, final_gain):
    per_b = SEQ // CMB_TM
    nblk = TOKENS // CMB_TM
    return pl.pallas_call(
        _combine_kernel,
        grid=(nblk,),
        in_specs=[pl.BlockSpec((CMB_TM, D_MODEL), lambda i: (i, 0)),
                  pl.BlockSpec((CMB_TM, D_MODEL), lambda i: (i + nblk, 0)),
                  pl.BlockSpec((CMB_TM, D_MODEL), lambda i: (i, 0)),
                  pl.BlockSpec((CMB_TM, LANES), lambda i: (i, 0)),
                  pl.BlockSpec((None, 1, D_MODEL), lambda i: (i // per_b, 0, gate_chunk)),
                  pl.BlockSpec((1, D_MODEL), lambda i: (0, 0))],
        out_specs=pl.BlockSpec((CMB_TM, D_MODEL), lambda i: (i, 0)),
        out_shape=jax.ShapeDtypeStruct((TOKENS, D_MODEL), F32),
        compiler_params=_cparams(1),
        name="moe_combine_final_norm",
    )(ys, ys, x1, wt, mod3, final_gain)


def _rotate_half_cols(w):
    half = w.shape[-1] // 2
    return jnp.concatenate([-w[..., half:], w[..., :half]], axis=-1)


def _block_diag_in(bbar):
    bb = bbar.reshape(SSM_GROUPS // SSM_GB, SSM_GB, SSM_STATE, SSM_GROUP_CH)
    w = jnp.einsum('bgpn,gh->bgnhp', bb, jnp.eye(SSM_GB, dtype=bbar.dtype))
    return w.reshape(SSM_GROUPS // SSM_GB, SSM_GB_CH, SSM_GB_ST)


def _block_diag_out(cmat):
    cc = cmat.reshape(SSM_GROUPS // SSM_GB, SSM_GB, SSM_GROUP_CH, SSM_STATE)
    w = jnp.einsum('bgnp,gh->bgphn', cc, jnp.eye(SSM_GB, dtype=cmat.dtype))
    return w.reshape(SSM_GROUPS // SSM_GB, SSM_GB_ST, SSM_GB_CH)


def kernel(x, c, positions, w_ada, b_ada, norm_mix_gain, w_in, q_lat_gain, w_uq, kv_lat_gain, w_ukv,
           ssm_lam_re, ssm_lam_im, ssm_log_dt, ssm_b_re, ssm_b_im, ssm_c_re, ssm_c_im, ssm_d,
           w_glu, b_glu, mla_out_gain, ssm_out_gain, w_out, norm_ffn_gain,
           w_group_router, b_group_router, w_expert_router, b_expert_router,
           w1_experts, w3_experts, w2_experts, final_gain):
    x2 = x.reshape(TOKENS, D_MODEL)

    c8 = jnp.zeros((SUBLANES, D_MODEL), F32).at[:BATCH].set(c)
    mod = _ada(c8, w_ada[0], b_ada[0].reshape(1, 6 * D_MODEL))
    mod3 = mod.reshape(SUBLANES, 1, 6 * D_MODEL)

    z = _in_proj(x2, norm_mix_gain[0].reshape(1, D_MODEL), mod3, 0, 1, _wz_prep(w_in[0].T))

    cs = _rope_table(positions.reshape(TOKENS, 1))

    wq = w_uq[0].reshape(Q_RANK, MLA_HEADS, QK_HEAD)
    wq_heads = jnp.concatenate([wq, _rotate_half_cols(wq[:, :, QK_NOPE:])], axis=-1)
    wq_heads = wq_heads.astype(BF16).reshape(Q_RANK, MLA_HEADS * 2 * LANES)
    q = _q_proj(z, q_lat_gain[0].reshape(1, Q_RANK), wq_heads, cs)
    wkv = w_ukv[0].reshape(KV_RANK, MLA_HEADS, QK_NOPE + V_HEAD)
    wk_heads = wkv[:, :, :QK_NOPE].astype(BF16).reshape(KV_RANK, MLA_HEADS * QK_NOPE)
    wvt_heads = wkv[:, :, QK_NOPE:].transpose(1, 2, 0).astype(BF16)
    k, vt = _kv_proj(z, kv_lat_gain[0].reshape(1, KV_RANK), wk_heads, wvt_heads, cs)
    o_mla = _attention(q, k, vt)

    n_st = SSM_GROUPS * SSM_STATE
    abar_re, abar_im, bbar_re, bbar_im = _s5_disc(
        ssm_lam_re[0].reshape(n_st, 1), ssm_lam_im[0].reshape(n_st, 1),
        jnp.repeat(ssm_log_dt[0], SSM_STATE).reshape(n_st, 1),
        ssm_b_re[0].reshape(n_st, SSM_GROUP_CH), ssm_b_im[0].reshape(n_st, SSM_GROUP_CH))
    n_gb = SSM_GROUPS // SSM_GB
    wb = jnp.concatenate([_block_diag_in(bbar_re), _block_diag_in(bbar_im)], axis=-1)
    wb = wb.reshape(SSM_PAIRS, 2 * SSM_GB_CH, 2 * SSM_GB_ST).astype(BF16)
    wc = jnp.concatenate([_block_diag_out(ssm_c_re[0]), -_block_diag_out(ssm_c_im[0])], axis=1)
    wc = wc.reshape(SSM_PAIRS, 2, 2 * SSM_GB_ST, SSM_GB_CH).transpose(0, 2, 1, 3)
    wc = wc.reshape(SSM_PAIRS, 2 * SSM_GB_ST, 2 * SSM_GB_CH).astype(BF16)
    a16 = jnp.concatenate([abar_re.reshape(n_gb, SSM_GB_ST), abar_im.reshape(n_gb, SSM_GB_ST)], axis=-1)
    a8 = jnp.repeat(a16.reshape(SSM_PAIRS, 2, 2 * SSM_GB_ST), BATCH, axis=1)

    y = _s5_scan(z.reshape(BATCH, SEQ, Z_COLS), wb, a8, wc, ssm_d[0].reshape(1, SSM_WIDTH))
    o_ssm = _glu(y.reshape(TOKENS, SSM_WIDTH), w_glu[0].astype(BF16), b_glu[0].reshape(1, SSM_WIDTH),
                 ssm_out_gain[0].reshape(1, SSM_WIDTH))

    x1 = _out_proj(o_mla, o_ssm, mla_out_gain[0].reshape(1, MLA_WIDTH), w_out[0].astype(BF16), x2, mod3, 2)

    w_r = jnp.concatenate([w_group_router[0], w_expert_router[0],
                           jnp.zeros((D_MODEL, LANES - N_EGROUPS - N_EXPERTS), F32)], axis=1)
    w_r_hi = w_r.astype(BF16)
    w_r_lo = (w_r - w_r_hi.astype(F32)).astype(BF16)
    b_r = jnp.concatenate([b_group_router[0], b_expert_router[0],
                           jnp.zeros((LANES - N_EGROUPS - N_EXPERTS,), F32)]).reshape(1, LANES)
    h2, eid, wt = _router(x1, norm_ffn_gain[0].reshape(1, D_MODEL), mod3, 3, 4, w_r_hi, w_r_lo, b_r)

    row_dst, blk_e, blk_nvalid, n_used = _moe_plan(eid[:, :TOP_K])
    hmid = _expert_up(row_dst, blk_e, blk_nvalid, n_used, h2, w1_experts[0], w3_experts[0])
    ys = _expert_down(row_dst, blk_e, blk_nvalid, n_used, hmid, w2_experts[0])
    out = _combine(ys, x1, wt, mod3, 5, final_gain.reshape(1, D_MODEL))
    return out.reshape(BATCH, SEQ, D_MODEL)
```

```python
import math

import jax
import jax.numpy as jnp
from jax import lax
from jax.experimental import pallas as pl
from jax.experimental.pallas import tpu as pltpu

F32 = jnp.float32
BF16 = jnp.bfloat16

D_MODEL = 4096
BATCH = 4
SEQ = 2048
TOKENS = BATCH * SEQ
CHUNK = 64
EPS = 1e-6

MLA_HEADS = 16
QK_NOPE = 128
QK_ROPE = 64
QK_HEAD = QK_NOPE + QK_ROPE
V_HEAD = 128
Q_RANK = 768
KV_RANK = 512
ROPE_THETA = 10000.0
MLA_WIDTH = MLA_HEADS * V_HEAD

SSM_WIDTH = D_MODEL - MLA_WIDTH
SSM_GROUP_CH = 16
SSM_GROUPS = SSM_WIDTH // SSM_GROUP_CH
SSM_STATE = 64

N_EGROUPS = 8
EXPERTS_PER_GROUP = 8
N_EXPERTS = N_EGROUPS * EXPERTS_PER_GROUP
TOP_K = 2
D_EXPERT = 512

LANES = 128
SUBLANES = 8
VMEM_LIMIT = 56 * 1024 * 1024

Z_Q = 0
Z_SSM = Q_RANK
Z_KV = Z_SSM + SSM_WIDTH
Z_KR = Z_KV + KV_RANK
Z_COLS = Z_KR + 2 * QK_ROPE

ATT_BLK = 256
ATT_NBLK = SEQ // ATT_BLK

SSM_GB = 8
SSM_GB_CH = SSM_GB * SSM_GROUP_CH
SSM_GB_ST = SSM_GB * SSM_STATE
SSM_PAIRS = SSM_GROUPS // (2 * SSM_GB)
SSM_TCHUNK = 256

MOE_BLK = 384
MOE_NBLK = (TOKENS * TOP_K + N_EXPERTS * (MOE_BLK - 1)) // MOE_BLK + 1
MOE_ROWS = MOE_NBLK * MOE_BLK
DMA_UNROLL = 8
CMB_TM = 128


def _cparams(n_axes, vmem=VMEM_LIMIT):
    return pltpu.CompilerParams(dimension_semantics=("arbitrary",) * n_axes, vmem_limit_bytes=vmem)


def _ada_kernel(c_ref, w_ref, b_ref, o_ref):
    c = c_ref[...]
    ca = (c * jax.nn.sigmoid(c)).astype(BF16)
    o_ref[...] = jnp.dot(ca, w_ref[...].astype(BF16), preferred_element_type=F32) + b_ref[...]


def _ada(c8, w, b):
    n = w.shape[1]
    tn = 512
    return pl.pallas_call(
        _ada_kernel,
        grid=(n // tn,),
        in_specs=[pl.BlockSpec((SUBLANES, D_MODEL), lambda j: (0, 0)),
                  pl.BlockSpec((D_MODEL, tn), lambda j: (0, j)),
                  pl.BlockSpec((1, tn), lambda j: (0, j))],
        out_specs=pl.BlockSpec((SUBLANES, tn), lambda j: (0, j)),
        out_shape=jax.ShapeDtypeStruct((SUBLANES, n), F32),
        compiler_params=_cparams(1),
        name="ada_mod",
    )(c8, w, b)


def _rms(x, gain):
    return x * lax.rsqrt(jnp.mean(x * x, axis=-1, keepdims=True) + EPS) * gain


def _in_proj_kernel(x_ref, g_ref, sh_ref, sc_ref, w_ref, o_ref, h_ref):
    @pl.when(pl.program_id(1) == 0)
    def _():
        y = _rms(x_ref[...], g_ref[...])
        h_ref[...] = (y * (1.0 + sc_ref[...]) + sh_ref[...]).astype(h_ref.dtype)

    o_ref[...] = lax.dot_general(h_ref[...], w_ref[...], (((1,), (1,)), ((), ())), preferred_element_type=F32)


def _in_proj(x2, gain, mod3, shift_chunk, scale_chunk, w_t):
    tm = 512
    n = w_t.shape[0]
    tn = n // 3
    per_b = SEQ // tm
    return pl.pallas_call(
        _in_proj_kernel,
        grid=(TOKENS // tm, n // tn),
        in_specs=[pl.BlockSpec((tm, D_MODEL), lambda i, j: (i, 0)),
                  pl.BlockSpec((1, D_MODEL), lambda i, j: (0, 0)),
                  pl.BlockSpec((None, 1, D_MODEL), lambda i, j: (i // per_b, 0, shift_chunk)),
                  pl.BlockSpec((None, 1, D_MODEL), lambda i, j: (i // per_b, 0, scale_chunk)),
                  pl.BlockSpec((tn, D_MODEL), lambda i, j: (j, 0))],
        out_specs=pl.BlockSpec((tm, tn), lambda i, j: (i, j)),
        out_shape=jax.ShapeDtypeStruct((TOKENS, n), F32),
        scratch_shapes=[pltpu.VMEM((tm, D_MODEL), BF16)],
        compiler_params=_cparams(2),
        name="in_proj",
    )(x2, gain, mod3, mod3, w_t)


W_IN_KV = Q_RANK
W_IN_KR = Q_RANK + KV_RANK
W_IN_SSM = W_IN_KR + QK_ROPE
W_IN_COLS = W_IN_SSM + SSM_WIDTH


def _wz_kernel(w_ref, o_ref):
    o_ref[Z_Q:Z_Q + Q_RANK] = w_ref[:W_IN_KV].astype(o_ref.dtype)
    o_ref[Z_SSM:Z_SSM + SSM_WIDTH] = w_ref[W_IN_SSM:].astype(o_ref.dtype)
    o_ref[Z_KV:Z_KV + KV_RANK] = w_ref[W_IN_KV:W_IN_KR].astype(o_ref.dtype)
    half = QK_ROPE // 2
    o_ref[Z_KR:Z_KR + QK_ROPE] = w_ref[W_IN_KR:W_IN_SSM].astype(o_ref.dtype)
    o_ref[Z_KR + QK_ROPE:Z_KR + QK_ROPE + half] = (-w_ref[W_IN_KR + half:W_IN_SSM]).astype(o_ref.dtype)
    o_ref[Z_KR + QK_ROPE + half:] = w_ref[W_IN_KR:W_IN_KR + half].astype(o_ref.dtype)


def _wz_prep(w_in_t):
    tk = 512
    return pl.pallas_call(
        _wz_kernel,
        grid=(D_MODEL // tk,),
        in_specs=[pl.BlockSpec((W_IN_COLS, tk), lambda i: (0, i))],
        out_specs=pl.BlockSpec((Z_COLS, tk), lambda i: (0, i)),
        out_shape=jax.ShapeDtypeStruct((Z_COLS, D_MODEL), BF16),
        compiler_params=_cparams(1),
        name="in_proj_weight_prep",
    )(w_in_t)


def _rope_table_kernel(pos_ref, o_ref):
    lane = lax.broadcasted_iota(jnp.int32, (1, LANES), 1)
    pair = (lane % (QK_ROPE // 2)).astype(F32)
    inv_freq = jnp.exp(-math.log(ROPE_THETA) * (2.0 * pair) / QK_ROPE)
    ang = pos_ref[...].astype(F32) * inv_freq
    o_ref[...] = jnp.where(lane < QK_ROPE, jnp.cos(ang), jnp.sin(ang))


def _rope_table(pos_col):
    tm = 1024
    return pl.pallas_call(
        _rope_table_kernel,
        grid=(TOKENS // tm,),
        in_specs=[pl.BlockSpec((tm, 1), lambda i: (i, 0))],
        out_specs=pl.BlockSpec((tm, LANES), lambda i: (i, 0)),
        out_shape=jax.ShapeDtypeStruct((TOKENS, LANES), F32),
        compiler_params=_cparams(1),
        name="rope_table",
    )(pos_col)


def _rope_pair(t, cs):
    u = t * cs
    return u + pltpu.roll(u, QK_ROPE, axis=1)


def _q_proj_kernel(ql_ref, g_ref, w_ref, cs_ref, o_ref):
    hn = _rms(ql_ref[...], g_ref[...]).astype(BF16)
    cs = cs_ref[...]
    for h in range(MLA_HEADS):
        r = jnp.dot(hn, w_ref[:, h * 2 * LANES:(h + 1) * 2 * LANES], preferred_element_type=F32)
        o_ref[h, :, :QK_NOPE] = r[:, :QK_NOPE].astype(o_ref.dtype)
        o_ref[h, :, QK_NOPE:] = _rope_pair(r[:, QK_NOPE:], cs)[:, :QK_ROPE].astype(o_ref.dtype)


def _q_proj(z, gain, w_heads, cs):
    tm = ATT_BLK
    per_b = SEQ // tm
    return pl.pallas_call(
        _q_proj_kernel,
        grid=(TOKENS // tm,),
        in_specs=[pl.BlockSpec((tm, Q_RANK), lambda i: (i, Z_Q // Q_RANK)),
                  pl.BlockSpec((1, Q_RANK), lambda i: (0, 0)),
                  pl.BlockSpec((Q_RANK, MLA_HEADS * 2 * LANES), lambda i: (0, 0)),
                  pl.BlockSpec((tm, LANES), lambda i: (i, 0))],
        out_specs=pl.BlockSpec((None, MLA_HEADS, tm, QK_HEAD), lambda i: (i // per_b, 0, i % per_b, 0)),
        out_shape=jax.ShapeDtypeStruct((BATCH, MLA_HEADS, SEQ, QK_HEAD), BF16),
        compiler_params=_cparams(1),
        name="mla_q_proj",
    )(z, gain, w_heads, cs)


def _kv_proj_kernel(kva_ref, kvb_ref, kr_ref, g_ref, wk_ref, wvt_ref, cs_ref, k_ref, vt_ref):
    kvl = jnp.concatenate([kva_ref[...], kvb_ref[...]], axis=1)
    hn = _rms(kvl, g_ref[...]).astype(BF16)
    kr = _rope_pair(kr_ref[...], cs_ref[...])[:, :QK_ROPE].astype(k_ref.dtype)
    for h in range(MLA_HEADS):
        k_ref[h, :, :QK_NOPE] = jnp.dot(hn, wk_ref[:, h * QK_NOPE:(h + 1) * QK_NOPE],
                                        preferred_element_type=F32).astype(k_ref.dtype)
        k_ref[h, :, QK_NOPE:] = kr
        vt_ref[h] = lax.dot_general(wvt_ref[h], hn, (((1,), (1,)), ((), ())),
                                    preferred_element_type=F32).astype(vt_ref.dtype)


def _kv_proj(z, gain, wk_heads, wvt_heads, cs):
    tm = ATT_BLK
    per_b = SEQ // tm
    half = KV_RANK // 2
    return pl.pallas_call(
        _kv_proj_kernel,
        grid=(TOKENS // tm,),
        in_specs=[pl.BlockSpec((tm, half), lambda i: (i, Z_KV // half)),
                  pl.BlockSpec((tm, half), lambda i: (i, Z_KV // half + 1)),
                  pl.BlockSpec((tm, LANES), lambda i: (i, Z_KR // LANES)),
                  pl.BlockSpec((1, KV_RANK), lambda i: (0, 0)),
                  pl.BlockSpec((KV_RANK, MLA_HEADS * QK_NOPE), lambda i: (0, 0)),
                  pl.BlockSpec((MLA_HEADS, V_HEAD, KV_RANK), lambda i: (0, 0, 0)),
                  pl.BlockSpec((tm, LANES), lambda i: (i, 0))],
        out_specs=[pl.BlockSpec((None, MLA_HEADS, tm, QK_HEAD), lambda i: (i // per_b, 0, i % per_b, 0)),
                   pl.BlockSpec((None, MLA_HEADS, None, V_HEAD, tm), lambda i: (i // per_b, 0, i % per_b, 0, 0))],
        out_shape=[jax.ShapeDtypeStruct((BATCH, MLA_HEADS, SEQ, QK_HEAD), BF16),
                   jax.ShapeDtypeStruct((BATCH, MLA_HEADS, ATT_NBLK, V_HEAD, ATT_BLK), BF16)],
        compiler_params=_cparams(1),
        name="mla_kv_proj",
    )(z, z, z, gain, wk_heads, wvt_heads, cs)


ATT_HEADS = 8


def _attn_kernel(q_ref, k_ref, vt_ref, o_ref, acc_ref):
    log2_scale = (QK_HEAD ** -0.5) * math.log2(math.e)
    key_chunk = lax.broadcasted_iota(jnp.int32, (ATT_BLK, ATT_BLK), 0) // CHUNK
    qry_chunk = lax.broadcasted_iota(jnp.int32, (ATT_BLK, ATT_BLK), 1) // CHUNK
    diag_mask = key_chunk <= qry_chunk

    qi = pl.program_id(2)
    if True:
        def kv_block(j, stats, n_blk, masked):
            keys = n_blk * ATT_BLK
            k0 = pl.multiple_of(j * ATT_BLK, ATT_BLK)
            sts = [lax.dot_general(k_ref[h, pl.ds(k0, keys), :], q_ref[h],
                                   (((1,), (1,)), ((), ())), preferred_element_type=F32)
                   for h in range(ATT_HEADS)]
            new_stats, ps, alphas = [], [], []
            for h in range(ATT_HEADS):
                m, l = stats[h]
                st = sts[h] * log2_scale
                if masked:
                    mask = diag_mask if n_blk == 1 else jnp.concatenate(
                        [jnp.ones(((n_blk - 1) * ATT_BLK, ATT_BLK), jnp.bool_), diag_mask], axis=0)
                    st = jnp.where(mask, st, -jnp.inf)
                m_new = jnp.maximum(m, jnp.max(st, axis=0, keepdims=True))
                alpha = jnp.exp2(m - m_new)
                p = jnp.exp2(st - m_new)
                new_stats.append((m_new, alpha * l + jnp.sum(p, axis=0, keepdims=True)))
                ps.append(p.astype(BF16))
                alphas.append(alpha)
            for h in range(ATT_HEADS):
                pv = jnp.dot(vt_ref[h, j], ps[h][:ATT_BLK], preferred_element_type=F32)
                for b in range(1, n_blk):
                    pv = pv + jnp.dot(vt_ref[h, j + b], ps[h][b * ATT_BLK:(b + 1) * ATT_BLK],
                                      preferred_element_type=F32)
                acc_ref[h] = alphas[h] * acc_ref[h] + pv
            return tuple(new_stats)

        acc_ref[...] = jnp.zeros_like(acc_ref)
        stats = tuple((jnp.full((1, ATT_BLK), -jnp.inf, F32), jnp.zeros((1, ATT_BLK), F32))
                      for _ in range(ATT_HEADS))
        n_pairs = lax.shift_right_logical(qi, 1)
        stats = lax.fori_loop(0, n_pairs, lambda jj, st: kv_block(2 * jj, st, 2, False), stats)
        stats = lax.cond(qi % 2 == 1, lambda st: kv_block(qi - 1, st, 2, True),
                         lambda st: kv_block(qi, st, 1, True), stats)
        for h in range(ATT_HEADS):
            o_ref[:, h * V_HEAD:(h + 1) * V_HEAD] = (acc_ref[h] / stats[h][1]).T


def _attention(q, k, vt):
    return pl.pallas_call(
        _attn_kernel,
        grid=(BATCH, MLA_HEADS // ATT_HEADS, ATT_NBLK),
        in_specs=[pl.BlockSpec((None, ATT_HEADS, ATT_BLK, QK_HEAD), lambda b, h, i: (b, h, i, 0)),
                  pl.BlockSpec((None, ATT_HEADS, SEQ, QK_HEAD), lambda b, h, i: (b, h, 0, 0)),
                  pl.BlockSpec((None, ATT_HEADS, ATT_NBLK, V_HEAD, ATT_BLK), lambda b, h, i: (b, h, 0, 0, 0))],
        out_specs=pl.BlockSpec((ATT_BLK, ATT_HEADS * V_HEAD), lambda b, h, i: (b * ATT_NBLK + i, h)),
        out_shape=jax.ShapeDtypeStruct((TOKENS, MLA_WIDTH), F32),
        scratch_shapes=[pltpu.VMEM((ATT_HEADS, V_HEAD, ATT_BLK), F32)],
        compiler_params=_cparams(3),
        name="mla_attention",
    )(q, k, vt)


def _s5_disc_kernel(lre_ref, lim_ref, ldt_ref, bre_ref, bim_ref, are_ref, aim_ref, bbre_ref, bbim_ref):
    lam_re = lre_ref[...]
    lam_im = lim_ref[...]
    dt = jnp.exp(ldt_ref[...])
    mag = jnp.exp(lam_re * dt)
    abar_re = mag * jnp.cos(lam_im * dt)
    abar_im = mag * jnp.sin(lam_im * dt)
    nr = abar_re - 1.0
    ni = abar_im
    den = lam_re * lam_re + lam_im * lam_im
    f_re = (nr * lam_re + ni * lam_im) / den
    f_im = (ni * lam_re - nr * lam_im) / den
    are_ref[...] = abar_re
    aim_ref[...] = abar_im
    b_re = bre_ref[...]
    b_im = bim_ref[...]
    bbre_ref[...] = f_re * b_re - f_im * b_im
    bbim_ref[...] = f_re * b_im + f_im * b_re


def _s5_disc(lam_re, lam_im, log_dt, b_re, b_im):
    n = SSM_GROUPS * SSM_STATE
    tm = 1024
    col = pl.BlockSpec((tm, 1), lambda i: (i, 0))
    mat = pl.BlockSpec((tm, SSM_GROUP_CH), lambda i: (i, 0))
    return pl.pallas_call(
        _s5_disc_kernel,
        grid=(n // tm,),
        in_specs=[col, col, col, mat, mat],
        out_specs=[col, col, mat, mat],
        out_shape=[jax.ShapeDtypeStruct((n, 1), F32), jax.ShapeDtypeStruct((n, 1), F32),
                   jax.ShapeDtypeStruct((n, SSM_GROUP_CH), F32), jax.ShapeDtypeStruct((n, SSM_GROUP_CH), F32)],
        compiler_params=_cparams(1),
        name="s5_discretise",
    )(lam_re, lam_im, log_dt, b_re, b_im)


def _s5_scan_kernel(u_ref, wb_ref, a_ref, wc_ref, d_ref, y_ref, lhs_ref, s_ref, yb_ref, h_ref):
    @pl.when(pl.program_id(1) == 0)
    def _():
        h_ref[...] = jnp.zeros_like(h_ref)
        lhs_ref[...] = jnp.zeros_like(lhs_ref)

    for b in range(BATCH):
        lhs_ref[0, pl.ds(b, SSM_TCHUNK, stride=SUBLANES), :] = u_ref[b, :, :SSM_GB_CH]
        lhs_ref[1, pl.ds(b + BATCH, SSM_TCHUNK, stride=SUBLANES), :] = u_ref[b, :, SSM_GB_CH:]
    half = SSM_TCHUNK * SUBLANES // 2
    wb = wb_ref[...]
    for r0 in (0, half):
        um = jnp.concatenate([lhs_ref[0, r0:r0 + half], lhs_ref[1, r0:r0 + half]], axis=1).astype(BF16)
        s_ref[r0:r0 + half] = jnp.dot(um, wb, preferred_element_type=F32)

    a_re = a_ref[:, :SSM_GB_ST]
    a_im = a_ref[:, SSM_GB_ST:]

    def step(t, carry):
        h_re, h_im = carry
        r0 = pl.multiple_of(t * SUBLANES, SUBLANES)
        n_re = a_re * h_re - a_im * h_im + s_ref[pl.ds(r0, SUBLANES), :SSM_GB_ST]
        n_im = a_re * h_im + a_im * h_re + s_ref[pl.ds(r0, SUBLANES), SSM_GB_ST:]
        s_ref[pl.ds(r0, SUBLANES), :SSM_GB_ST] = n_re
        s_ref[pl.ds(r0, SUBLANES), SSM_GB_ST:] = n_im
        return n_re, n_im

    h_re, h_im = lax.fori_loop(0, SSM_TCHUNK, step, (h_ref[:, :SSM_GB_ST], h_ref[:, SSM_GB_ST:]), unroll=8)
    h_ref[:, :SSM_GB_ST] = h_re
    h_ref[:, SSM_GB_ST:] = h_im

    wc = wc_ref[...]
    for r0 in (0, half):
        y8 = jnp.dot(s_ref[r0:r0 + half].astype(BF16), wc, preferred_element_type=F32)
        yb_ref[0, r0:r0 + half] = y8[:, :SSM_GB_CH]
        yb_ref[1, r0:r0 + half] = y8[:, SSM_GB_CH:]
    d = d_ref[...]
    for b in range(BATCH):
        y_ref[b, :, :SSM_GB_CH] = (yb_ref[0, pl.ds(b, SSM_TCHUNK, stride=SUBLANES), :]
                                   + d[:, :SSM_GB_CH] * u_ref[b, :, :SSM_GB_CH])
        y_ref[b, :, SSM_GB_CH:] = (yb_ref[1, pl.ds(b + BATCH, SSM_TCHUNK, stride=SUBLANES), :]
                                   + d[:, SSM_GB_CH:] * u_ref[b, :, SSM_GB_CH:])


def _s5_scan(z3, wb, a8, wc, d_row):
    rows = SSM_TCHUNK * SUBLANES
    width = 2 * SSM_GB_CH
    states = 2 * SSM_GB_ST
    return pl.pallas_call(
        _s5_scan_kernel,
        grid=(SSM_PAIRS, SEQ // SSM_TCHUNK),
        in_specs=[pl.BlockSpec((BATCH, SSM_TCHUNK, width), lambda g, c: (0, c, Z_SSM // width + g)),
                  pl.BlockSpec((None, width, states), lambda g, c: (g, 0, 0)),
                  pl.BlockSpec((None, SUBLANES, states), lambda g, c: (g, 0, 0)),
                  pl.BlockSpec((None, states, width), lambda g, c: (g, 0, 0)),
                  pl.BlockSpec((1, width), lambda g, c: (0, g))],
        out_specs=pl.BlockSpec((BATCH, SSM_TCHUNK, width), lambda g, c: (0, c, g)),
        out_shape=jax.ShapeDtypeStruct((BATCH, SEQ, SSM_WIDTH), F32),
        scratch_shapes=[pltpu.VMEM((2, rows, SSM_GB_CH), F32), pltpu.VMEM((rows, states), F32),
                        pltpu.VMEM((2, rows, SSM_GB_CH), F32), pltpu.VMEM((SUBLANES, states), F32)],
        compiler_params=_cparams(2),
        name="s5_scan",
    )(z3, wb, a8, wc, d_row)


def _gelu_tanh(x):
    return 0.5 * x * (1.0 + jnp.tanh(math.sqrt(2.0 / math.pi) * (x + 0.044715 * (x * x * x))))


def _glu_kernel(y_ref, w_ref, b_ref, g_ref, o_ref):
    g = _gelu_tanh(y_ref[...])
    gate = jax.nn.sigmoid(jnp.dot(g.astype(BF16), w_ref[...], preferred_element_type=F32) + b_ref[...])
    o_ref[...] = _rms(g * gate, g_ref[...]).astype(o_ref.dtype)


def _glu(y, w, b, gain):
    tm = 256
    return pl.pallas_call(
        _glu_kernel,
        grid=(TOKENS // tm,),
        in_specs=[pl.BlockSpec((tm, SSM_WIDTH), lambda i: (i, 0)),
                  pl.BlockSpec((SSM_WIDTH, SSM_WIDTH), lambda i: (0, 0)),
                  pl.BlockSpec((1, SSM_WIDTH), lambda i: (0, 0)),
                  pl.BlockSpec((1, SSM_WIDTH), lambda i: (0, 0))],
        out_specs=pl.BlockSpec((tm, SSM_WIDTH), lambda i: (i, 0)),
        out_shape=jax.ShapeDtypeStruct((TOKENS, SSM_WIDTH), BF16),
        compiler_params=_cparams(1),
        name="s5_glu",
    )(y, w, b, gain)


def _out_proj_kernel(om_ref, os_ref, g_ref, w_ref, x_ref, ga_ref, o_ref, lhs_ref):
    @pl.when(pl.program_id(1) == 0)
    def _():
        lhs_ref[:, :MLA_WIDTH] = _rms(om_ref[...], g_ref[...]).astype(BF16)
        lhs_ref[:, MLA_WIDTH:] = os_ref[...]

    o_ref[...] = x_ref[...] + ga_ref[...] * jnp.dot(lhs_ref[...], w_ref[...], preferred_element_type=F32)


def _out_proj(o_mla, o_ssm, gain, w, x2, mod3, gate_chunk):
    tm, tn = 512, 1024
    per_b = SEQ // tm
    nj = D_MODEL // tn
    return pl.pallas_call(
        _out_proj_kernel,
        grid=(TOKENS // tm, nj),
        in_specs=[pl.BlockSpec((tm, MLA_WIDTH), lambda i, j: (i, 0)),
                  pl.BlockSpec((tm, SSM_WIDTH), lambda i, j: (i, 0)),
                  pl.BlockSpec((1, MLA_WIDTH), lambda i, j: (0, 0)),
                  pl.BlockSpec((D_MODEL, tn), lambda i, j: (0, j)),
                  pl.BlockSpec((tm, tn), lambda i, j: (i, j)),
                  pl.BlockSpec((None, 1, tn), lambda i, j: (i // per_b, 0, gate_chunk * nj + j))],
        out_specs=pl.BlockSpec((tm, tn), lambda i, j: (i, j)),
        out_shape=jax.ShapeDtypeStruct((TOKENS, D_MODEL), F32),
        scratch_shapes=[pltpu.VMEM((tm, D_MODEL), BF16)],
        compiler_params=_cparams(2),
        name="out_proj",
    )(o_mla, o_ssm, gain, w, x2, mod3)


def _router_kernel(x_ref, g_ref, sh_ref, sc_ref, whi_ref, wlo_ref, br_ref, h_ref, eid_ref, wt_ref):
    h = _rms(x_ref[...], g_ref[...]) * (1.0 + sc_ref[...]) + sh_ref[...]
    h_ref[...] = h
    h_hi = h.astype(BF16)
    h_lo = (h - h_hi.astype(F32)).astype(BF16)

    whi = whi_ref[...]
    logits = (jnp.dot(h_hi, whi, preferred_element_type=F32)
              + jnp.dot(h_lo, whi, preferred_element_type=F32)
              + jnp.dot(h_hi, wlo_ref[...], preferred_element_type=F32)
              + br_ref[...])

    col = lax.broadcasted_iota(jnp.int32, logits.shape, 1)
    neg = -jnp.inf
    is_g = col < N_EGROUPS
    gl = jnp.where(is_g, logits, neg)
    gmax = jnp.max(gl, axis=1, keepdims=True)
    grp = jnp.min(jnp.where(gl == gmax, col, LANES), axis=1, keepdims=True)
    p_grp = 1.0 / jnp.sum(jnp.where(is_g, jnp.exp(logits - gmax), 0.0), axis=1, keepdims=True)

    ecol = col - N_EGROUPS
    in_grp = (ecol >= 0) & (ecol < N_EXPERTS) & ((ecol // EXPERTS_PER_GROUP) == grp)
    el = jnp.where(in_grp, logits, neg)
    v1 = jnp.max(el, axis=1, keepdims=True)
    i1 = jnp.min(jnp.where(el == v1, col, LANES), axis=1, keepdims=True)
    el2 = jnp.where(col == i1, neg, el)
    v2 = jnp.max(el2, axis=1, keepdims=True)
    i2 = jnp.min(jnp.where(el2 == v2, col, LANES), axis=1, keepdims=True)

    s2 = jnp.exp(v2 - v1)
    w1 = p_grp / (1.0 + s2)
    w2 = p_grp * s2 / (1.0 + s2)
    eid_ref[...] = jnp.where(col == 0, i1 - N_EGROUPS, jnp.where(col == 1, i2 - N_EGROUPS, 0))
    wt_ref[...] = jnp.where(col == 0, w1, jnp.where(col == 1, w2, 0.0))


def _router(x1, gain, mod3, shift_chunk, scale_chunk, w_hi, w_lo, b_row):
    tm = 256
    per_b = SEQ // tm
    return pl.pallas_call(
        _router_kernel,
        grid=(TOKENS // tm,),
        in_specs=[pl.BlockSpec((tm, D_MODEL), lambda i: (i, 0)),
                  pl.BlockSpec((1, D_MODEL), lambda i: (0, 0)),
                  pl.BlockSpec((None, 1, D_MODEL), lambda i: (i // per_b, 0, shift_chunk)),
                  pl.BlockSpec((None, 1, D_MODEL), lambda i: (i // per_b, 0, scale_chunk)),
                  pl.BlockSpec((D_MODEL, LANES), lambda i: (0, 0)),
                  pl.BlockSpec((D_MODEL, LANES), lambda i: (0, 0)),
                  pl.BlockSpec((1, LANES), lambda i: (0, 0))],
        out_specs=[pl.BlockSpec((tm, D_MODEL), lambda i: (i, 0)),
                   pl.BlockSpec((tm, LANES), lambda i: (i, 0)),
                   pl.BlockSpec((tm, LANES), lambda i: (i, 0))],
        out_shape=[jax.ShapeDtypeStruct((TOKENS, D_MODEL), F32),
                   jax.ShapeDtypeStruct((TOKENS, LANES), jnp.int32),
                   jax.ShapeDtypeStruct((TOKENS, LANES), F32)],
        compiler_params=_cparams(1),
        name="ffn_router",
    )(x1, gain, mod3, mod3, w_hi, w_lo, b_row)


def _moe_plan(eid):
    n_assign = TOKENS * TOP_K
    e_flat = eid.reshape(n_assign)
    onehot = (e_flat[:, None] == jnp.arange(N_EXPERTS, dtype=jnp.int32)[None, :]).astype(jnp.int32)
    csum = jnp.cumsum(onehot, axis=0)
    counts = csum[-1]
    rank = jnp.sum((csum - onehot) * onehot, axis=1)
    nblk_e = (counts + MOE_BLK - 1) // MOE_BLK
    bend = jnp.cumsum(nblk_e)
    bstart = bend - nblk_e
    dest = jnp.sum(onehot * bstart[None, :], axis=1) * MOE_BLK + rank
    a_id = jnp.arange(n_assign, dtype=jnp.int32)
    row_dst = jnp.zeros((MOE_ROWS,), jnp.int32).at[dest].set((a_id % TOP_K) * TOKENS + a_id // TOP_K,
                                                              unique_indices=True)
    blk = jnp.arange(MOE_NBLK, dtype=jnp.int32)
    n_used = bend[-1]
    blk_e = jnp.minimum(jnp.sum((bend[None, :] <= blk[:, None]).astype(jnp.int32), axis=1), N_EXPERTS - 1)
    blk_nvalid = jnp.clip(counts[blk_e] - (blk - bstart[blk_e]) * MOE_BLK, 0, MOE_BLK)
    blk_nvalid = jnp.where(blk < n_used, blk_nvalid, 0).astype(jnp.int32)
    last_e = blk_e[jnp.maximum(n_used - 1, 0)]
    blk_e = jnp.where(blk < n_used, blk_e, last_e).astype(jnp.int32)
    return row_dst, blk_e, blk_nvalid, n_used.reshape(1).astype(jnp.int32)


def _copy_groups(n_rows):
    return lax.shift_right_logical(n_rows + (DMA_UNROLL - 1), DMA_UNROLL.bit_length() - 1)


def _expert_up_kernel(dst_ref, be_ref, nv_ref, nu_ref, h_hbm, w1_ref, w3_ref, o_ref, xbuf, sem):
    i = pl.program_id(0)
    slot = i % 2

    def row_copy(src_row, dst_slot, dst_row):
        return pltpu.make_async_copy(h_hbm.at[pl.ds(src_row, 1)], xbuf.at[dst_slot, pl.ds(dst_row, 1)],
                                     sem.at[dst_slot])

    def issue(block, dst_slot):
        base = block * MOE_BLK

        def body(g, c):
            for u in range(DMA_UNROLL):
                r = g * DMA_UNROLL + u
                row_copy(dst_ref[base + r] & (TOKENS - 1), dst_slot, r).start()
            return c

        lax.fori_loop(0, _copy_groups(nv_ref[block]), body, 0)

    @pl.when(i == 0)
    def _():
        xbuf[...] = jnp.zeros_like(xbuf)
        issue(0, 0)

    @pl.when(i + 1 < pl.num_programs(0))
    def _():
        issue(i + 1, 1 - slot)

    def wait_body(g, c):
        for u in range(DMA_UNROLL):
            row_copy(0, slot, g * DMA_UNROLL + u).wait()
        return c

    lax.fori_loop(0, _copy_groups(nv_ref[i]), wait_body, 0)

    @pl.when(i < nu_ref[0])
    def _():
        x = xbuf[slot].astype(BF16)
        a = jnp.dot(x, w1_ref[...].astype(BF16), preferred_element_type=F32)
        b = jnp.dot(x, w3_ref[...].astype(BF16), preferred_element_type=F32)
        o_ref[...] = (a * jax.nn.sigmoid(a) * b).astype(o_ref.dtype)

    @pl.when(i >= nu_ref[0])
    def _():
        o_ref[...] = jnp.zeros_like(o_ref)


def _expert_up(row_dst, blk_e, blk_nvalid, n_used, h2, w1, w3):
    wspec = pl.BlockSpec((None, D_MODEL, D_EXPERT), lambda i, tok, be, ng, nu: (be[i], 0, 0))
    return pl.pallas_call(
        _expert_up_kernel,
        grid_spec=pltpu.PrefetchScalarGridSpec(
            num_scalar_prefetch=4,
            grid=(MOE_NBLK,),
            in_specs=[pl.BlockSpec(memory_space=pl.ANY), wspec, wspec],
            out_specs=pl.BlockSpec((MOE_BLK, D_EXPERT), lambda i, tok, be, ng, nu: (i, 0)),
            scratch_shapes=[pltpu.VMEM((2, MOE_BLK, D_MODEL), F32),
                            pltpu.SemaphoreType.DMA((2,))]),
        out_shape=jax.ShapeDtypeStruct((MOE_ROWS, D_EXPERT), BF16),
        compiler_params=_cparams(1, 60 * 1024 * 1024),
        name="moe_expert_up",
    )(row_dst, blk_e, blk_nvalid, n_used, h2, w1, w3)


def _expert_down_kernel(dst_ref, be_ref, nv_ref, nu_ref, h_ref, w_ref, y_hbm, ybuf, sem):
    i = pl.program_id(0)
    slot = i % 2

    def row_copy(src_row, dst_row, src_slot):
        return pltpu.make_async_copy(ybuf.at[src_slot, pl.ds(src_row, 1)], y_hbm.at[pl.ds(dst_row, 1)],
                                     sem.at[src_slot])

    def drain(block, src_slot):
        nv = nv_ref[block]
        full = lax.shift_right_logical(nv, DMA_UNROLL.bit_length() - 1)

        def body(g, c):
            for u in range(DMA_UNROLL):
                row_copy(g * DMA_UNROLL + u, 0, src_slot).wait()
            return c

        lax.fori_loop(0, full, body, 0)

        def tail(r, c):
            row_copy(r, 0, src_slot).wait()
            return c

        lax.fori_loop(full * DMA_UNROLL, nv, tail, 0)

    @pl.when(i >= 2)
    def _():
        drain(i - 2, slot)

    @pl.when(i < nu_ref[0])
    def _():
        ybuf[slot] = jnp.dot(h_ref[...], w_ref[...].astype(BF16), preferred_element_type=F32)
        base = i * MOE_BLK
        nv = nv_ref[i]
        full = lax.shift_right_logical(nv, DMA_UNROLL.bit_length() - 1)

        def body(g, c):
            for u in range(DMA_UNROLL):
                r = g * DMA_UNROLL + u
                row_copy(r, dst_ref[base + r], slot).start()
            return c

        lax.fori_loop(0, full, body, 0)

        def tail(r, c):
            row_copy(r, dst_ref[base + r], slot).start()
            return c

        lax.fori_loop(full * DMA_UNROLL, nv, tail, 0)

    @pl.when(i == pl.num_programs(0) - 1)
    def _():
        drain(i - 1, 1 - slot)
        drain(i, slot)


def _expert_down(row_dst, blk_e, blk_nvalid, n_used, hmid, w2):
    return pl.pallas_call(
        _expert_down_kernel,
        grid_spec=pltpu.PrefetchScalarGridSpec(
            num_scalar_prefetch=4,
            grid=(MOE_NBLK,),
            in_specs=[pl.BlockSpec((MOE_BLK, D_EXPERT), lambda i, dst, be, nv, nu: (i, 0)),
                      pl.BlockSpec((None, D_EXPERT, D_MODEL), lambda i, dst, be, nv, nu: (be[i], 0, 0))],
            out_specs=pl.BlockSpec(memory_space=pl.ANY),
            scratch_shapes=[pltpu.VMEM((2, MOE_BLK, D_MODEL), F32),
                            pltpu.SemaphoreType.DMA((2,))]),
        out_shape=jax.ShapeDtypeStruct((TOP_K * TOKENS, D_MODEL), F32),
        compiler_params=_cparams(1),
        name="moe_expert_down",
    )(row_dst, blk_e, blk_nvalid, n_used, hmid, w2)


def _combine_kernel(y0_ref, y1_ref, x_ref, wt_ref, gf_ref, fg_ref, o_ref):
    wt = wt_ref[...]
    moe = wt[:, 0:1] * y0_ref[...] + wt[:, 1:2] * y1_ref[...]
    o_ref[...] = _rms(x_ref[...] + gf_ref[...] * moe, fg_ref[...])


def _combine(ys, x1, wt, mod3, gate_chunk, final_gain):
    per_b = SEQ // CMB_TM
    nblk = TOKENS // CMB_TM
    return pl.pallas_call(
        _combine_kernel,
        grid=(nblk,),
        in_specs=[pl.BlockSpec((CMB_TM, D_MODEL), lambda i: (i, 0)),
                  pl.BlockSpec((CMB_TM, D_MODEL), lambda i: (i + nblk, 0)),
                  pl.BlockSpec((CMB_TM, D_MODEL), lambda i: (i, 0)),
                  pl.BlockSpec((CMB_TM, LANES), lambda i: (i, 0)),
                  pl.BlockSpec((None, 1, D_MODEL), lambda i: (i // per_b, 0, gate_chunk)),
                  pl.BlockSpec((1, D_MODEL), lambda i: (0, 0))],
        out_specs=pl.BlockSpec((CMB_TM, D_MODEL), lambda i: (i, 0)),
        out_shape=jax.ShapeDtypeStruct((TOKENS, D_MODEL), F32),
        compiler_params=_cparams(1),
        name="moe_combine_final_norm",
    )(ys, ys, x1, wt, mod3, final_gain)


def _rotate_half_cols(w):
    half = w.shape[-1] // 2
    return jnp.concatenate([-w[..., half:], w[..., :half]], axis=-1)


def _block_diag_in(bbar):
    bb = bbar.reshape(SSM_GROUPS // SSM_GB, SSM_GB, SSM_STATE, SSM_GROUP_CH)
    w = jnp.einsum('bgpn,gh->bgnhp', bb, jnp.eye(SSM_GB, dtype=bbar.dtype))
    return w.reshape(SSM_GROUPS // SSM_GB, SSM_GB_CH, SSM_GB_ST)


def _block_diag_out(cmat):
    cc = cmat.reshape(SSM_GROUPS // SSM_GB, SSM_GB, SSM_GROUP_CH, SSM_STATE)
    w = jnp.einsum('bgnp,gh->bgphn', cc, jnp.eye(SSM_GB, dtype=cmat.dtype))
    return w.reshape(SSM_GROUPS // SSM_GB, SSM_GB_ST, SSM_GB_CH)


def kernel(x, c, positions, w_ada, b_ada, norm_mix_gain, w_in, q_lat_gain, w_uq, kv_lat_gain, w_ukv,
           ssm_lam_re, ssm_lam_im, ssm_log_dt, ssm_b_re, ssm_b_im, ssm_c_re, ssm_c_im, ssm_d,
           w_glu, b_glu, mla_out_gain, ssm_out_gain, w_out, norm_ffn_gain,
           w_group_router, b_group_router, w_expert_router, b_expert_router,
           w1_experts, w3_experts, w2_experts, final_gain):
    x2 = x.reshape(TOKENS, D_MODEL)

    c8 = jnp.zeros((SUBLANES, D_MODEL), F32).at[:BATCH].set(c)
    mod = _ada(c8, w_ada[0], b_ada[0].reshape(1, 6 * D_MODEL))
    mod3 = mod.reshape(SUBLANES, 1, 6 * D_MODEL)

    z = _in_proj(x2, norm_mix_gain[0].reshape(1, D_MODEL), mod3, 0, 1, _wz_prep(w_in[0].T))

    cs = _rope_table(positions.reshape(TOKENS, 1))

    wq = w_uq[0].reshape(Q_RANK, MLA_HEADS, QK_HEAD)
    wq_heads = jnp.concatenate([wq, _rotate_half_cols(wq[:, :, QK_NOPE:])], axis=-1)
    wq_heads = wq_heads.astype(BF16).reshape(Q_RANK, MLA_HEADS * 2 * LANES)
    q = _q_proj(z, q_lat_gain[0].reshape(1, Q_RANK), wq_heads, cs)
    wkv = w_ukv[0].reshape(KV_RANK, MLA_HEADS, QK_NOPE + V_HEAD)
    wk_heads = wkv[:, :, :QK_NOPE].astype(BF16).reshape(KV_RANK, MLA_HEADS * QK_NOPE)
    wvt_heads = wkv[:, :, QK_NOPE:].transpose(1, 2, 0).astype(BF16)
    k, vt = _kv_proj(z, kv_lat_gain[0].reshape(1, KV_RANK), wk_heads, wvt_heads, cs)
    o_mla = _attention(q, k, vt)

    n_st = SSM_GROUPS * SSM_STATE
    abar_re, abar_im, bbar_re, bbar_im = _s5_disc(
        ssm_lam_re[0].reshape(n_st, 1), ssm_lam_im[0].reshape(n_st, 1),
        jnp.repeat(ssm_log_dt[0], SSM_STATE).reshape(n_st, 1),
        ssm_b_re[0].reshape(n_st, SSM_GROUP_CH), ssm_b_im[0].reshape(n_st, SSM_GROUP_CH))
    n_gb = SSM_GROUPS // SSM_GB
    wb = jnp.concatenate([_block_diag_in(bbar_re), _block_diag_in(bbar_im)], axis=-1)
    wb = wb.reshape(SSM_PAIRS, 2 * SSM_GB_CH, 2 * SSM_GB_ST).astype(BF16)
    wc = jnp.concatenate([_block_diag_out(ssm_c_re[0]), -_block_diag_out(ssm_c_im[0])], axis=1)
    wc = wc.reshape(SSM_PAIRS, 2, 2 * SSM_GB_ST, SSM_GB_CH).transpose(0, 2, 1, 3)
    wc = wc.reshape(SSM_PAIRS, 2 * SSM_GB_ST, 2 * SSM_GB_CH).astype(BF16)
    a16 = jnp.concatenate([abar_re.reshape(n_gb, SSM_GB_ST), abar_im.reshape(n_gb, SSM_GB_ST)], axis=-1)
    a8 = jnp.repeat(a16.reshape(SSM_PAIRS, 2, 2 * SSM_GB_ST), BATCH, axis=1)

    y = _s5_scan(z.reshape(BATCH, SEQ, Z_COLS), wb, a8, wc, ssm_d[0].reshape(1, SSM_WIDTH))
    o_ssm = _glu(y.reshape(TOKENS, SSM_WIDTH), w_glu[0].astype(BF16), b_glu[0].reshape(1, SSM_WIDTH),
                 ssm_out_gain[0].reshape(1, SSM_WIDTH))

    x1 = _out_proj(o_mla, o_ssm, mla_out_gain[0].reshape(1, MLA_WIDTH), w_out[0].astype(BF16), x2, mod3, 2)

    w_r = jnp.concatenate([w_group_router[0], w_expert_router[0],
                           jnp.zeros((D_MODEL, LANES - N_EGROUPS - N_EXPERTS), F32)], axis=1)
    w_r_hi = w_r.astype(BF16)
    w_r_lo = (w_r - w_r_hi.astype(F32)).astype(BF16)
    b_r = jnp.concatenate([b_group_router[0], b_expert_router[0],
                           jnp.zeros((LANES - N_EGROUPS - N_EXPERTS,), F32)]).reshape(1, LANES)
    h2, eid, wt = _router(x1, norm_ffn_gain[0].reshape(1, D_MODEL), mod3, 3, 4, w_r_hi, w_r_lo, b_r)

    row_dst, blk_e, blk_nvalid, n_used = _moe_plan(eid[:, :TOP_K])
    hmid = _expert_up(row_dst, blk_e, blk_nvalid, n_used, h2, w1_experts[0], w3_experts[0])
    ys = _expert_down(row_dst, blk_e, blk_nvalid, n_used, hmid, w2_experts[0])
    out = _combine(ys, x1, wt, mod3, 5, final_gain.reshape(1, D_MODEL))
    return out.reshape(BATCH, SEQ, D_MODEL)
```

```python
import math

import jax
import jax.numpy as jnp
from jax import lax
from jax.experimental import pallas as pl
from jax.experimental.pallas import tpu as pltpu

F32 = jnp.float32
BF16 = jnp.bfloat16

D_MODEL = 4096
BATCH = 4
SEQ = 2048
TOKENS = BATCH * SEQ
CHUNK = 64
EPS = 1e-6

MLA_HEADS = 16
QK_NOPE = 128
QK_ROPE = 64
QK_HEAD = QK_NOPE + QK_ROPE
V_HEAD = 128
Q_RANK = 768
KV_RANK = 512
ROPE_THETA = 10000.0
MLA_WIDTH = MLA_HEADS * V_HEAD

SSM_WIDTH = D_MODEL - MLA_WIDTH
SSM_GROUP_CH = 16
SSM_GROUPS = SSM_WIDTH // SSM_GROUP_CH
SSM_STATE = 64

N_EGROUPS = 8
EXPERTS_PER_GROUP = 8
N_EXPERTS = N_EGROUPS * EXPERTS_PER_GROUP
TOP_K = 2
D_EXPERT = 512

LANES = 128
SUBLANES = 8
VMEM_LIMIT = 56 * 1024 * 1024

Z_Q = 0
Z_SSM = Q_RANK
Z_KV = Z_SSM + SSM_WIDTH
Z_KR = Z_KV + KV_RANK
Z_COLS = Z_KR + 2 * QK_ROPE

ATT_BLK = 256
ATT_NBLK = SEQ // ATT_BLK

SSM_GB = 8
SSM_GB_CH = SSM_GB * SSM_GROUP_CH
SSM_GB_ST = SSM_GB * SSM_STATE
SSM_PAIRS = SSM_GROUPS // (2 * SSM_GB)
SSM_TCHUNK = 256

MOE_BLK = 384
MOE_NBLK = (TOKENS * TOP_K + N_EXPERTS * (MOE_BLK - 1)) // MOE_BLK + 1
MOE_ROWS = MOE_NBLK * MOE_BLK
DMA_UNROLL = 8
CMB_TM = 128


def _cparams(n_axes, vmem=VMEM_LIMIT):
    return pltpu.CompilerParams(dimension_semantics=("arbitrary",) * n_axes, vmem_limit_bytes=vmem)


def _ada_kernel(c_ref, w_ref, b_ref, o_ref):
    c = c_ref[...]
    ca = (c * jax.nn.sigmoid(c)).astype(BF16)
    o_ref[...] = jnp.dot(ca, w_ref[...].astype(BF16), preferred_element_type=F32) + b_ref[...]


def _ada(c8, w, b):
    n = w.shape[1]
    tn = 512
    return pl.pallas_call(
        _ada_kernel,
        grid=(n // tn,),
        in_specs=[pl.BlockSpec((SUBLANES, D_MODEL), lambda j: (0, 0)),
                  pl.BlockSpec((D_MODEL, tn), lambda j: (0, j)),
                  pl.BlockSpec((1, tn), lambda j: (0, j))],
        out_specs=pl.BlockSpec((SUBLANES, tn), lambda j: (0, j)),
        out_shape=jax.ShapeDtypeStruct((SUBLANES, n), F32),
        compiler_params=_cparams(1),
        name="ada_mod",
    )(c8, w, b)


def _rms(x, gain):
    return x * lax.rsqrt(jnp.mean(x * x, axis=-1, keepdims=True) + EPS) * gain


def _in_proj_kernel(x_ref, g_ref, sh_ref, sc_ref, w_ref, o_ref, h_ref):
    @pl.when(pl.program_id(1) == 0)
    def _():
        y = _rms(x_ref[...], g_ref[...])
        h_ref[...] = (y * (1.0 + sc_ref[...]) + sh_ref[...]).astype(h_ref.dtype)

    o_ref[...] = lax.dot_general(h_ref[...], w_ref[...], (((1,), (1,)), ((), ())), preferred_element_type=F32)


def _in_proj(x2, gain, mod3, shift_chunk, scale_chunk, w_t):
    tm = 512
    n = w_t.shape[0]
    tn = n // 3
    per_b = SEQ // tm
    return pl.pallas_call(
        _in_proj_kernel,
        grid=(TOKENS // tm, n // tn),
        in_specs=[pl.BlockSpec((tm, D_MODEL), lambda i, j: (i, 0)),
                  pl.BlockSpec((1, D_MODEL), lambda i, j: (0, 0)),
                  pl.BlockSpec((None, 1, D_MODEL), lambda i, j: (i // per_b, 0, shift_chunk)),
                  pl.BlockSpec((None, 1, D_MODEL), lambda i, j: (i // per_b, 0, scale_chunk)),
                  pl.BlockSpec((tn, D_MODEL), lambda i, j: (j, 0))],
        out_specs=pl.BlockSpec((tm, tn), lambda i, j: (i, j)),
        out_shape=jax.ShapeDtypeStruct((TOKENS, n), F32),
        scratch_shapes=[pltpu.VMEM((tm, D_MODEL), BF16)],
        compiler_params=_cparams(2),
        name="in_proj",
    )(x2, gain, mod3, mod3, w_t)


W_IN_KV = Q_RANK
W_IN_KR = Q_RANK + KV_RANK
W_IN_SSM = W_IN_KR + QK_ROPE
W_IN_COLS = W_IN_SSM + SSM_WIDTH


def _wz_kernel(w_ref, o_ref):
    o_ref[Z_Q:Z_Q + Q_RANK] = w_ref[:W_IN_KV].astype(o_ref.dtype)
    o_ref[Z_SSM:Z_SSM + SSM_WIDTH] = w_ref[W_IN_SSM:].astype(o_ref.dtype)
    o_ref[Z_KV:Z_KV + KV_RANK] = w_ref[W_IN_KV:W_IN_KR].astype(o_ref.dtype)
    half = QK_ROPE // 2
    o_ref[Z_KR:Z_KR + QK_ROPE] = w_ref[W_IN_KR:W_IN_SSM].astype(o_ref.dtype)
    o_ref[Z_KR + QK_ROPE:Z_KR + QK_ROPE + half] = (-w_ref[W_IN_KR + half:W_IN_SSM]).astype(o_ref.dtype)
    o_ref[Z_KR + QK_ROPE + half:] = w_ref[W_IN_KR:W_IN_KR + half].astype(o_ref.dtype)


def _wz_prep(w_in_t):
    tk = 512
    return pl.pallas_call(
        _wz_kernel,
        grid=(D_MODEL // tk,),
        in_specs=[pl.BlockSpec((W_IN_COLS, tk), lambda i: (0, i))],
        out_specs=pl.BlockSpec((Z_COLS, tk), lambda i: (0, i)),
        out_shape=jax.ShapeDtypeStruct((Z_COLS, D_MODEL), BF16),
        compiler_params=_cparams(1),
        name="in_proj_weight_prep",
    )(w_in_t)


def _rope_table_kernel(pos_ref, o_ref):
    lane = lax.broadcasted_iota(jnp.int32, (1, LANES), 1)
    pair = (lane % (QK_ROPE // 2)).astype(F32)
    inv_freq = jnp.exp(-math.log(ROPE_THETA) * (2.0 * pair) / QK_ROPE)
    ang = pos_ref[...].astype(F32) * inv_freq
    o_ref[...] = jnp.where(lane < QK_ROPE, jnp.cos(ang), jnp.sin(ang))


def _rope_table(pos_col):
    tm = 1024
    return pl.pallas_call(
        _rope_table_kernel,
        grid=(TOKENS // tm,),
        in_specs=[pl.BlockSpec((tm, 1), lambda i: (i, 0))],
        out_specs=pl.BlockSpec((tm, LANES), lambda i: (i, 0)),
        out_shape=jax.ShapeDtypeStruct((TOKENS, LANES), F32),
        compiler_params=_cparams(1),
        name="rope_table",
    )(pos_col)


def _rope_pair(t, cs):
    u = t * cs
    return u + pltpu.roll(u, QK_ROPE, axis=1)


def _q_proj_kernel(ql_ref, g_ref, w_ref, cs_ref, o_ref):
    hn = _rms(ql_ref[...], g_ref[...]).astype(BF16)
    cs = cs_ref[...]
    for h in range(MLA_HEADS):
        r = jnp.dot(hn, w_ref[:, h * 2 * LANES:(h + 1) * 2 * LANES], preferred_element_type=F32)
        o_ref[h, :, :QK_NOPE] = r[:, :QK_NOPE].astype(o_ref.dtype)
        o_ref[h, :, QK_NOPE:] = _rope_pair(r[:, QK_NOPE:], cs)[:, :QK_ROPE].astype(o_ref.dtype)


def _q_proj(z, gain, w_heads, cs):
    tm = ATT_BLK
    per_b = SEQ // tm
    return pl.pallas_call(
        _q_proj_kernel,
        grid=(TOKENS // tm,),
        in_specs=[pl.BlockSpec((tm, Q_RANK), lambda i: (i, Z_Q // Q_RANK)),
                  pl.BlockSpec((1, Q_RANK), lambda i: (0, 0)),
                  pl.BlockSpec((Q_RANK, MLA_HEADS * 2 * LANES), lambda i: (0, 0)),
                  pl.BlockSpec((tm, LANES), lambda i: (i, 0))],
        out_specs=pl.BlockSpec((None, MLA_HEADS, tm, QK_HEAD), lambda i: (i // per_b, 0, i % per_b, 0)),
        out_shape=jax.ShapeDtypeStruct((BATCH, MLA_HEADS, SEQ, QK_HEAD), BF16),
        compiler_params=_cparams(1),
        name="mla_q_proj",
    )(z, gain, w_heads, cs)


def _kv_proj_kernel(kva_ref, kvb_ref, kr_ref, g_ref, w_ref, cs_ref, k_ref, vt_ref):
    kvl = jnp.concatenate([kva_ref[...], kvb_ref[...]], axis=1)
    hn = _rms(kvl, g_ref[...]).astype(BF16)
    kr = _rope_pair(kr_ref[...], cs_ref[...])[:, :QK_ROPE].astype(k_ref.dtype)
    width = QK_NOPE + V_HEAD
    for h in range(MLA_HEADS):
        r = jnp.dot(hn, w_ref[:, h * width:(h + 1) * width], preferred_element_type=F32)
        k_ref[h, :, :QK_NOPE] = r[:, :QK_NOPE].astype(k_ref.dtype)
        k_ref[h, :, QK_NOPE:] = kr
        vt_ref[h] = r[:, QK_NOPE:].T.astype(vt_ref.dtype)


def _kv_proj(z, gain, w_heads, cs):
    tm = ATT_BLK
    per_b = SEQ // tm
    half = KV_RANK // 2
    return pl.pallas_call(
        _kv_proj_kernel,
        grid=(TOKENS // tm,),
        in_specs=[pl.BlockSpec((tm, half), lambda i: (i, Z_KV // half)),
                  pl.BlockSpec((tm, half), lambda i: (i, Z_KV // half + 1)),
                  pl.BlockSpec((tm, LANES), lambda i: (i, Z_KR // LANES)),
                  pl.BlockSpec((1, KV_RANK), lambda i: (0, 0)),
                  pl.BlockSpec((KV_RANK, MLA_HEADS * (QK_NOPE + V_HEAD)), lambda i: (0, 0)),
                  pl.BlockSpec((tm, LANES), lambda i: (i, 0))],
        out_specs=[pl.BlockSpec((None, MLA_HEADS, tm, QK_HEAD), lambda i: (i // per_b, 0, i % per_b, 0)),
                   pl.BlockSpec((None, MLA_HEADS, None, V_HEAD, tm), lambda i: (i // per_b, 0, i % per_b, 0, 0))],
        out_shape=[jax.ShapeDtypeStruct((BATCH, MLA_HEADS, SEQ, QK_HEAD), BF16),
                   jax.ShapeDtypeStruct((BATCH, MLA_HEADS, ATT_NBLK, V_HEAD, ATT_BLK), BF16)],
        compiler_params=_cparams(1),
        name="mla_kv_proj",
    )(z, z, z, gain, w_heads, cs)


ATT_HEADS = 8


def _attn_kernel(q_ref, k_ref, vt_ref, o_ref, acc_ref):
    log2_scale = (QK_HEAD ** -0.5) * math.log2(math.e)
    key_chunk = lax.broadcasted_iota(jnp.int32, (ATT_BLK, ATT_BLK), 0) // CHUNK
    qry_chunk = lax.broadcasted_iota(jnp.int32, (ATT_BLK, ATT_BLK), 1) // CHUNK
    diag_mask = key_chunk <= qry_chunk

    qi = pl.program_id(2)
    if True:
        def kv_block(j, stats, n_blk, masked):
            keys = n_blk * ATT_BLK
            k0 = pl.multiple_of(j * ATT_BLK, ATT_BLK)
            sts = [lax.dot_general(k_ref[h, pl.ds(k0, keys), :], q_ref[h],
                                   (((1,), (1,)), ((), ())), preferred_element_type=F32)
                   for h in range(ATT_HEADS)]
            new_stats, ps, alphas = [], [], []
            for h in range(ATT_HEADS):
                m, l = stats[h]
                st = sts[h] * log2_scale
                if masked:
                    mask = diag_mask if n_blk == 1 else jnp.concatenate(
                        [jnp.ones(((n_blk - 1) * ATT_BLK, ATT_BLK), jnp.bool_), diag_mask], axis=0)
                    st = jnp.where(mask, st, -jnp.inf)
                m_new = jnp.maximum(m, jnp.max(st, axis=0, keepdims=True))
                alpha = jnp.exp2(m - m_new)
                p = jnp.exp2(st - m_new)
                new_stats.append((m_new, alpha * l + jnp.sum(p, axis=0, keepdims=True)))
                ps.append(p.astype(BF16))
                alphas.append(alpha)
            for h in range(ATT_HEADS):
                pv = jnp.dot(vt_ref[h, j], ps[h][:ATT_BLK], preferred_element_type=F32)
                for b in range(1, n_blk):
                    pv = pv + jnp.dot(vt_ref[h, j + b], ps[h][b * ATT_BLK:(b + 1) * ATT_BLK],
                                      preferred_element_type=F32)
                acc_ref[h] = alphas[h] * acc_ref[h] + pv
            return tuple(new_stats)

        acc_ref[...] = jnp.zeros_like(acc_ref)
        stats = tuple((jnp.full((1, ATT_BLK), -jnp.inf, F32), jnp.zeros((1, ATT_BLK), F32))
                      for _ in range(ATT_HEADS))
        n_pairs = lax.shift_right_logical(qi, 1)
        stats = lax.fori_loop(0, n_pairs, lambda jj, st: kv_block(2 * jj, st, 2, False), stats)
        stats = lax.cond(qi % 2 == 1, lambda st: kv_block(qi - 1, st, 2, True),
                         lambda st: kv_block(qi, st, 1, True), stats)
        for h in range(ATT_HEADS):
            o_ref[:, h * V_HEAD:(h + 1) * V_HEAD] = (acc_ref[h] / stats[h][1]).T


def _attention(q, k, vt):
    return pl.pallas_call(
        _attn_kernel,
        grid=(BATCH, MLA_HEADS // ATT_HEADS, ATT_NBLK),
        in_specs=[pl.BlockSpec((None, ATT_HEADS, ATT_BLK, QK_HEAD), lambda b, h, i: (b, h, i, 0)),
                  pl.BlockSpec((None, ATT_HEADS, SEQ, QK_HEAD), lambda b, h, i: (b, h, 0, 0)),
                  pl.BlockSpec((None, ATT_HEADS, ATT_NBLK, V_HEAD, ATT_BLK), lambda b, h, i: (b, h, 0, 0, 0))],
        out_specs=pl.BlockSpec((ATT_BLK, ATT_HEADS * V_HEAD), lambda b, h, i: (b * ATT_NBLK + i, h)),
        out_shape=jax.ShapeDtypeStruct((TOKENS, MLA_WIDTH), F32),
        scratch_shapes=[pltpu.VMEM((ATT_HEADS, V_HEAD, ATT_BLK), F32)],
        compiler_params=_cparams(3),
        name="mla_attention",
    )(q, k, vt)


def _s5_disc_kernel(lre_ref, lim_ref, ldt_ref, bre_ref, bim_ref, are_ref, aim_ref, bbre_ref, bbim_ref):
    lam_re = lre_ref[...]
    lam_im = lim_ref[...]
    dt = jnp.exp(ldt_ref[...])
    mag = jnp.exp(lam_re * dt)
    abar_re = mag * jnp.cos(lam_im * dt)
    abar_im = mag * jnp.sin(lam_im * dt)
    nr = abar_re - 1.0
    ni = abar_im
    den = lam_re * lam_re + lam_im * lam_im
    f_re = (nr * lam_re + ni * lam_im) / den
    f_im = (ni * lam_re - nr * lam_im) / den
    are_ref[...] = abar_re
    aim_ref[...] = abar_im
    b_re = bre_ref[...]
    b_im = bim_ref[...]
    bbre_ref[...] = f_re * b_re - f_im * b_im
    bbim_ref[...] = f_re * b_im + f_im * b_re


def _s5_disc(lam_re, lam_im, log_dt, b_re, b_im):
    n = SSM_GROUPS * SSM_STATE
    tm = 1024
    col = pl.BlockSpec((tm, 1), lambda i: (i, 0))
    mat = pl.BlockSpec((tm, SSM_GROUP_CH), lambda i: (i, 0))
    return pl.pallas_call(
        _s5_disc_kernel,
        grid=(n // tm,),
        in_specs=[col, col, col, mat, mat],
        out_specs=[col, col, mat, mat],
        out_shape=[jax.ShapeDtypeStruct((n, 1), F32), jax.ShapeDtypeStruct((n, 1), F32),
                   jax.ShapeDtypeStruct((n, SSM_GROUP_CH), F32), jax.ShapeDtypeStruct((n, SSM_GROUP_CH), F32)],
        compiler_params=_cparams(1),
        name="s5_discretise",
    )(lam_re, lam_im, log_dt, b_re, b_im)


def _s5_scan_kernel(u_ref, wb_ref, a_ref, wc_ref, d_ref, y_ref, lhs_ref, s_ref, yb_ref, h_ref):
    @pl.when(pl.program_id(1) == 0)
    def _():
        h_ref[...] = jnp.zeros_like(h_ref)
        lhs_ref[...] = jnp.zeros_like(lhs_ref)

    for b in range(BATCH):
        lhs_ref[0, pl.ds(b, SSM_TCHUNK, stride=SUBLANES), :] = u_ref[b, :, :SSM_GB_CH]
        lhs_ref[1, pl.ds(b + BATCH, SSM_TCHUNK, stride=SUBLANES), :] = u_ref[b, :, SSM_GB_CH:]
    half = SSM_TCHUNK * SUBLANES // 2
    wb = wb_ref[...]
    for r0 in (0, half):
        um = jnp.concatenate([lhs_ref[0, r0:r0 + half], lhs_ref[1, r0:r0 + half]], axis=1).astype(BF16)
        s_ref[r0:r0 + half] = jnp.dot(um, wb, preferred_element_type=F32)

    a_re = a_ref[:, :SSM_GB_ST]
    a_im = a_ref[:, SSM_GB_ST:]

    def step(t, carry):
        h_re, h_im = carry
        r0 = pl.multiple_of(t * SUBLANES, SUBLANES)
        n_re = a_re * h_re - a_im * h_im + s_ref[pl.ds(r0, SUBLANES), :SSM_GB_ST]
        n_im = a_re * h_im + a_im * h_re + s_ref[pl.ds(r0, SUBLANES), SSM_GB_ST:]
        s_ref[pl.ds(r0, SUBLANES), :SSM_GB_ST] = n_re
        s_ref[pl.ds(r0, SUBLANES), SSM_GB_ST:] = n_im
        return n_re, n_im

    h_re, h_im = lax.fori_loop(0, SSM_TCHUNK, step, (h_ref[:, :SSM_GB_ST], h_ref[:, SSM_GB_ST:]), unroll=8)
    h_ref[:, :SSM_GB_ST] = h_re
    h_ref[:, SSM_GB_ST:] = h_im

    wc = wc_ref[...]
    for r0 in (0, half):
        y8 = jnp.dot(s_ref[r0:r0 + half].astype(BF16), wc, preferred_element_type=F32)
        yb_ref[0, r0:r0 + half] = y8[:, :SSM_GB_CH]
        yb_ref[1, r0:r0 + half] = y8[:, SSM_GB_CH:]
    d = d_ref[...]
    for b in range(BATCH):
        y_ref[b, :, :SSM_GB_CH] = (yb_ref[0, pl.ds(b, SSM_TCHUNK, stride=SUBLANES), :]
                                   + d[:, :SSM_GB_CH] * u_ref[b, :, :SSM_GB_CH])
        y_ref[b, :, SSM_GB_CH:] = (yb_ref[1, pl.ds(b + BATCH, SSM_TCHUNK, stride=SUBLANES), :]
                                   + d[:, SSM_GB_CH:] * u_ref[b, :, SSM_GB_CH:])


def _s5_scan(z3, wb, a8, wc, d_row):
    rows = SSM_TCHUNK * SUBLANES
    width = 2 * SSM_GB_CH
    states = 2 * SSM_GB_ST
    return pl.pallas_call(
        _s5_scan_kernel,
        grid=(SSM_PAIRS, SEQ // SSM_TCHUNK),
        in_specs=[pl.BlockSpec((BATCH, SSM_TCHUNK, width), lambda g, c: (0, c, Z_SSM // width + g)),
                  pl.BlockSpec((None, width, states), lambda g, c: (g, 0, 0)),
                  pl.BlockSpec((None, SUBLANES, states), lambda g, c: (g, 0, 0)),
                  pl.BlockSpec((None, states, width), lambda g, c: (g, 0, 0)),
                  pl.BlockSpec((1, width), lambda g, c: (0, g))],
        out_specs=pl.BlockSpec((BATCH, SSM_TCHUNK, width), lambda g, c: (0, c, g)),
        out_shape=jax.ShapeDtypeStruct((BATCH, SEQ, SSM_WIDTH), F32),
        scratch_shapes=[pltpu.VMEM((2, rows, SSM_GB_CH), F32), pltpu.VMEM((rows, states), F32),
                        pltpu.VMEM((2, rows, SSM_GB_CH), F32), pltpu.VMEM((SUBLANES, states), F32)],
        compiler_params=_cparams(2),
        name="s5_scan",
    )(z3, wb, a8, wc, d_row)


def _gelu_tanh(x):
    return 0.5 * x * (1.0 + jnp.tanh(math.sqrt(2.0 / math.pi) * (x + 0.044715 * (x * x * x))))


def _glu_kernel(y_ref, w_ref, b_ref, g_ref, o_ref):
    g = _gelu_tanh(y_ref[...])
    gate = jax.nn.sigmoid(jnp.dot(g.astype(BF16), w_ref[...], preferred_element_type=F32) + b_ref[...])
    o_ref[...] = _rms(g * gate, g_ref[...]).astype(o_ref.dtype)


def _glu(y, w, b, gain):
    tm = 256
    return pl.pallas_call(
        _glu_kernel,
        grid=(TOKENS // tm,),
        in_specs=[pl.BlockSpec((tm, SSM_WIDTH), lambda i: (i, 0)),
                  pl.BlockSpec((SSM_WIDTH, SSM_WIDTH), lambda i: (0, 0)),
                  pl.BlockSpec((1, SSM_WIDTH), lambda i: (0, 0)),
                  pl.BlockSpec((1, SSM_WIDTH), lambda i: (0, 0))],
        out_specs=pl.BlockSpec((tm, SSM_WIDTH), lambda i: (i, 0)),
        out_shape=jax.ShapeDtypeStruct((TOKENS, SSM_WIDTH), BF16),
        compiler_params=_cparams(1),
        name="s5_glu",
    )(y, w, b, gain)


def _out_proj_kernel(om_ref, os_ref, g_ref, w_ref, x_ref, ga_ref, o_ref, lhs_ref):
    @pl.when(pl.program_id(1) == 0)
    def _():
        lhs_ref[:, :MLA_WIDTH] = _rms(om_ref[...], g_ref[...]).astype(BF16)
        lhs_ref[:, MLA_WIDTH:] = os_ref[...]

    o_ref[...] = x_ref[...] + ga_ref[...] * jnp.dot(lhs_ref[...], w_ref[...], preferred_element_type=F32)


def _out_proj(o_mla, o_ssm, gain, w, x2, mod3, gate_chunk):
    tm, tn = 512, 1024
    per_b = SEQ // tm
    nj = D_MODEL // tn
    return pl.pallas_call(
        _out_proj_kernel,
        grid=(TOKENS // tm, nj),
        in_specs=[pl.BlockSpec((tm, MLA_WIDTH), lambda i, j: (i, 0)),
                  pl.BlockSpec((tm, SSM_WIDTH), lambda i, j: (i, 0)),
                  pl.BlockSpec((1, MLA_WIDTH), lambda i, j: (0, 0)),
                  pl.BlockSpec((D_MODEL, tn), lambda i, j: (0, j)),
                  pl.BlockSpec((tm, tn), lambda i, j: (i, j)),
                  pl.BlockSpec((None, 1, tn), lambda i, j: (i // per_b, 0, gate_chunk * nj + j))],
        out_specs=pl.BlockSpec((tm, tn), lambda i, j: (i, j)),
        out_shape=jax.ShapeDtypeStruct((TOKENS, D_MODEL), F32),
        scratch_shapes=[pltpu.VMEM((tm, D_MODEL), BF16)],
        compiler_params=_cparams(2),
        name="out_proj",
    )(o_mla, o_ssm, gain, w, x2, mod3)


def _router_kernel(x_ref, g_ref, sh_ref, sc_ref, w_ref, br_ref, h_ref, eid_ref, wt_ref):
    h = _rms(x_ref[...], g_ref[...]) * (1.0 + sc_ref[...]) + sh_ref[...]
    h_ref[...] = h
    h_hi = h.astype(BF16)
    h_lo = (h - h_hi.astype(F32)).astype(BF16)

    tm = h.shape[0]
    r = jnp.dot(jnp.concatenate([h_hi, h_lo], axis=0), w_ref[...], preferred_element_type=F32)
    logits = r[:tm, :LANES] + r[:tm, LANES:] + r[tm:, :LANES] + br_ref[...]

    col = lax.broadcasted_iota(jnp.int32, logits.shape, 1)
    neg = -jnp.inf
    is_g = col < N_EGROUPS
    gl = jnp.where(is_g, logits, neg)
    gmax = jnp.max(gl, axis=1, keepdims=True)
    grp = jnp.min(jnp.where(gl == gmax, col, LANES), axis=1, keepdims=True)
    p_grp = 1.0 / jnp.sum(jnp.where(is_g, jnp.exp(logits - gmax), 0.0), axis=1, keepdims=True)

    ecol = col - N_EGROUPS
    in_grp = (ecol >= 0) & (ecol < N_EXPERTS) & ((ecol // EXPERTS_PER_GROUP) == grp)
    el = jnp.where(in_grp, logits, neg)
    v1 = jnp.max(el, axis=1, keepdims=True)
    i1 = jnp.min(jnp.where(el == v1, col, LANES), axis=1, keepdims=True)
    el2 = jnp.where(col == i1, neg, el)
    v2 = jnp.max(el2, axis=1, keepdims=True)
    i2 = jnp.min(jnp.where(el2 == v2, col, LANES), axis=1, keepdims=True)

    s2 = jnp.exp(v2 - v1)
    w1 = p_grp / (1.0 + s2)
    w2 = p_grp * s2 / (1.0 + s2)
    eid_ref[...] = jnp.where(col == 0, i1 - N_EGROUPS, jnp.where(col == 1, i2 - N_EGROUPS, 0))
    wt_ref[...] = jnp.where(col == 0, w1, jnp.where(col == 1, w2, 0.0))


def _router(x1, gain, mod3, shift_chunk, scale_chunk, w_hilo, b_row):
    tm = 256
    per_b = SEQ // tm
    return pl.pallas_call(
        _router_kernel,
        grid=(TOKENS // tm,),
        in_specs=[pl.BlockSpec((tm, D_MODEL), lambda i: (i, 0)),
                  pl.BlockSpec((1, D_MODEL), lambda i: (0, 0)),
                  pl.BlockSpec((None, 1, D_MODEL), lambda i: (i // per_b, 0, shift_chunk)),
                  pl.BlockSpec((None, 1, D_MODEL), lambda i: (i // per_b, 0, scale_chunk)),
                  pl.BlockSpec((D_MODEL, 2 * LANES), lambda i: (0, 0)),
                  pl.BlockSpec((1, LANES), lambda i: (0, 0))],
        out_specs=[pl.BlockSpec((tm, D_MODEL), lambda i: (i, 0)),
                   pl.BlockSpec((tm, LANES), lambda i: (i, 0)),
                   pl.BlockSpec((tm, LANES), lambda i: (i, 0))],
        out_shape=[jax.ShapeDtypeStruct((TOKENS, D_MODEL), F32),
                   jax.ShapeDtypeStruct((TOKENS, LANES), jnp.int32),
                   jax.ShapeDtypeStruct((TOKENS, LANES), F32)],
        compiler_params=_cparams(1),
        name="ffn_router",
    )(x1, gain, mod3, mod3, w_hilo, b_row)


def _moe_plan(eid):
    n_assign = TOKENS * TOP_K
    e_flat = eid.reshape(n_assign)
    onehot = (e_flat[:, None] == jnp.arange(N_EXPERTS, dtype=jnp.int32)[None, :]).astype(jnp.int32)
    csum = jnp.cumsum(onehot, axis=0)
    counts = csum[-1]
    rank = jnp.sum((csum - onehot) * onehot, axis=1)
    nblk_e = (counts + MOE_BLK - 1) // MOE_BLK
    bend = jnp.cumsum(nblk_e)
    bstart = bend - nblk_e
    dest = jnp.sum(onehot * bstart[None, :], axis=1) * MOE_BLK + rank
    a_id = jnp.arange(n_assign, dtype=jnp.int32)
    row_dst = jnp.zeros((MOE_ROWS,), jnp.int32).at[dest].set((a_id % TOP_K) * TOKENS + a_id // TOP_K,
                                                              unique_indices=True)
    blk = jnp.arange(MOE_NBLK, dtype=jnp.int32)
    n_used = bend[-1]
    blk_e = jnp.minimum(jnp.sum((bend[None, :] <= blk[:, None]).astype(jnp.int32), axis=1), N_EXPERTS - 1)
    blk_nvalid = jnp.clip(counts[blk_e] - (blk - bstart[blk_e]) * MOE_BLK, 0, MOE_BLK)
    blk_nvalid = jnp.where(blk < n_used, blk_nvalid, 0).astype(jnp.int32)
    last_e = blk_e[jnp.maximum(n_used - 1, 0)]
    blk_e = jnp.where(blk < n_used, blk_e, last_e).astype(jnp.int32)
    return row_dst, blk_e, blk_nvalid, n_used.reshape(1).astype(jnp.int32)


def _copy_groups(n_rows):
    return lax.shift_right_logical(n_rows + (DMA_UNROLL - 1), DMA_UNROLL.bit_length() - 1)


def _expert_up_kernel(dst_ref, be_ref, nv_ref, nu_ref, h_hbm, w1_ref, w3_ref, o_ref, xbuf, sem):
    i = pl.program_id(0)
    slot = i % 2

    def row_copy(src_row, dst_slot, dst_row):
        return pltpu.make_async_copy(h_hbm.at[pl.ds(src_row, 1)], xbuf.at[dst_slot, pl.ds(dst_row, 1)],
                                     sem.at[dst_slot])

    def issue(block, dst_slot):
        base = block * MOE_BLK

        def body(g, c):
            for u in range(DMA_UNROLL):
                r = g * DMA_UNROLL + u
                row_copy(dst_ref[base + r] & (TOKENS - 1), dst_slot, r).start()
            return c

        lax.fori_loop(0, _copy_groups(nv_ref[block]), body, 0)

    @pl.when(i == 0)
    def _():
        xbuf[...] = jnp.zeros_like(xbuf)
        issue(0, 0)

    @pl.when(i + 1 < pl.num_programs(0))
    def _():
        issue(i + 1, 1 - slot)

    def wait_body(g, c):
        for u in range(DMA_UNROLL):
            row_copy(0, slot, g * DMA_UNROLL + u).wait()
        return c

    lax.fori_loop(0, _copy_groups(nv_ref[i]), wait_body, 0)

    @pl.when(i < nu_ref[0])
    def _():
        x = xbuf[slot].astype(BF16)
        a = jnp.dot(x, w1_ref[...].astype(BF16), preferred_element_type=F32)
        b = jnp.dot(x, w3_ref[...].astype(BF16), preferred_element_type=F32)
        o_ref[...] = (a * jax.nn.sigmoid(a) * b).astype(o_ref.dtype)

    @pl.when(i >= nu_ref[0])
    def _():
        o_ref[...] = jnp.zeros_like(o_ref)


def _expert_up(row_dst, blk_e, blk_nvalid, n_used, h2, w1, w3):
    wspec = pl.BlockSpec((None, D_MODEL, D_EXPERT), lambda i, tok, be, ng, nu: (be[i], 0, 0))
    return pl.pallas_call(
        _expert_up_kernel,
        grid_spec=pltpu.PrefetchScalarGridSpec(
            num_scalar_prefetch=4,
            grid=(MOE_NBLK,),
            in_specs=[pl.BlockSpec(memory_space=pl.ANY), wspec, wspec],
            out_specs=pl.BlockSpec((MOE_BLK, D_EXPERT), lambda i, tok, be, ng, nu: (i, 0)),
            scratch_shapes=[pltpu.VMEM((2, MOE_BLK, D_MODEL), F32),
                            pltpu.SemaphoreType.DMA((2,))]),
        out_shape=jax.ShapeDtypeStruct((MOE_ROWS, D_EXPERT), BF16),
        compiler_params=_cparams(1, 60 * 1024 * 1024),
        name="moe_expert_up",
    )(row_dst, blk_e, blk_nvalid, n_used, h2, w1, w3)


def _expert_down_kernel(dst_ref, be_ref, nv_ref, nu_ref, h_ref, w_ref, y_hbm, ybuf, sem):
    i = pl.program_id(0)
    slot = i % 2

    def row_copy(src_row, dst_row, src_slot):
        return pltpu.make_async_copy(ybuf.at[src_slot, pl.ds(src_row, 1)], y_hbm.at[pl.ds(dst_row, 1)],
                                     sem.at[src_slot])

    def drain(block, src_slot):
        nv = nv_ref[block]
        full = lax.shift_right_logical(nv, DMA_UNROLL.bit_length() - 1)

        def body(g, c):
            for u in range(DMA_UNROLL):
                row_copy(g * DMA_UNROLL + u, 0, src_slot).wait()
            return c

        lax.fori_loop(0, full, body, 0)

        def tail(r, c):
            row_copy(r, 0, src_slot).wait()
            return c

        lax.fori_loop(full * DMA_UNROLL, nv, tail, 0)

    @pl.when(i >= 2)
    def _():
        drain(i - 2, slot)

    @pl.when(i < nu_ref[0])
    def _():
        ybuf[slot] = jnp.dot(h_ref[...], w_ref[...].astype(BF16), preferred_element_type=F32)
        base = i * MOE_BLK
        nv = nv_ref[i]
        full = lax.shift_right_logical(nv, DMA_UNROLL.bit_length() - 1)

        def body(g, c):
            for u in range(DMA_UNROLL):
                r = g * DMA_UNROLL + u
                row_copy(r, dst_ref[base + r], slot).start()
            return c

        lax.fori_loop(0, full, body, 0)

        def tail(r, c):
            row_copy(r, dst_ref[base + r], slot).start()
            return c

        lax.fori_loop(full * DMA_UNROLL, nv, tail, 0)

    @pl.when(i == pl.num_programs(0) - 1)
    def _():
        drain(i - 1, 1 - slot)
        drain(i, slot)


def _expert_down(row_dst, blk_e, blk_nvalid, n_used, hmid, w2):
    return pl.pallas_call(
        _expert_down_kernel,
        grid_spec=pltpu.PrefetchScalarGridSpec(
            num_scalar_prefetch=4,
            grid=(MOE_NBLK,),
            in_specs=[pl.BlockSpec((MOE_BLK, D_EXPERT), lambda i, dst, be, nv, nu: (i, 0)),
                      pl.BlockSpec((None, D_EXPERT, D_MODEL), lambda i, dst, be, nv, nu: (be[i], 0, 0))],
            out_specs=pl.BlockSpec(memory_space=pl.ANY),
            scratch_shapes=[pltpu.VMEM((2, MOE_BLK, D_MODEL), F32),
                            pltpu.SemaphoreType.DMA((2,))]),
        out_shape=jax.ShapeDtypeStruct((TOP_K * TOKENS, D_MODEL), F32),
        compiler_params=_cparams(1),
        name="moe_expert_down",
    )(row_dst, blk_e, blk_nvalid, n_used, hmid, w2)


def _combine_kernel(y0_ref, y1_ref, x_ref, wt_ref, gf_ref, fg_ref, o_ref):
    wt = wt_ref[...]
    moe = wt[:, 0:1] * y0_ref[...] + wt[:, 1:2] * y1_ref[...]
    o_ref[...] = _rms(x_ref[...] + gf_ref[...] * moe, fg_ref[...])


def _combine(ys, x1, wt, mod3, gate_chunk, final_gain):
    per_b = SEQ // CMB_TM
    nblk = TOKENS // CMB_TM
    return pl.pallas_call(
        _combine_kernel,
        grid=(nblk,),
        in_specs=[pl.BlockSpec((CMB_TM, D_MODEL), lambda i: (i, 0)),
                  pl.BlockSpec((CMB_TM, D_MODEL), lambda i: (i + nblk, 0)),
                  pl.BlockSpec((CMB_TM, D_MODEL), lambda i: (i, 0)),
                  pl.BlockSpec((CMB_TM, LANES), lambda i: (i, 0)),
                  pl.BlockSpec((None, 1, D_MODEL), lambda i: (i // per_b, 0, gate_chunk)),
                  pl.BlockSpec((1, D_MODEL), lambda i: (0, 0))],
        out_specs=pl.BlockSpec((CMB_TM, D_MODEL), lambda i: (i, 0)),
        out_shape=jax.ShapeDtypeStruct((TOKENS, D_MODEL), F32),
        compiler_params=_cparams(1),
        name="moe_combine_final_norm",
    )(ys, ys, x1, wt, mod3, final_gain)


def _rotate_half_cols(w):
    half = w.shape[-1] // 2
    return jnp.concatenate([-w[..., half:], w[..., :half]], axis=-1)


def _block_diag_in(bbar):
    bb = bbar.reshape(SSM_GROUPS // SSM_GB, SSM_GB, SSM_STATE, SSM_GROUP_CH)
    w = jnp.einsum('bgpn,gh->bgnhp', bb, jnp.eye(SSM_GB, dtype=bbar.dtype))
    return w.reshape(SSM_GROUPS // SSM_GB, SSM_GB_CH, SSM_GB_ST)


def _block_diag_out(cmat):
    cc = cmat.reshape(SSM_GROUPS // SSM_GB, SSM_GB, SSM_GROUP_CH, SSM_STATE)
    w = jnp.einsum('bgnp,gh->bgphn', cc, jnp.eye(SSM_GB, dtype=cmat.dtype))
    return w.reshape(SSM_GROUPS // SSM_GB, SSM_GB_ST, SSM_GB_CH)


def kernel(x, c, positions, w_ada, b_ada, norm_mix_gain, w_in, q_lat_gain, w_uq, kv_lat_gain, w_ukv,
           ssm_lam_re, ssm_lam_im, ssm_log_dt, ssm_b_re, ssm_b_im, ssm_c_re, ssm_c_im, ssm_d,
           w_glu, b_glu, mla_out_gain, ssm_out_gain, w_out, norm_ffn_gain,
           w_group_router, b_group_router, w_expert_router, b_expert_router,
           w1_experts, w3_experts, w2_experts, final_gain):
    x2 = x.reshape(TOKENS, D_MODEL)

    c8 = jnp.zeros((SUBLANES, D_MODEL), F32).at[:BATCH].set(c)
    mod = _ada(c8, w_ada[0], b_ada[0].reshape(1, 6 * D_MODEL))
    mod3 = mod.reshape(SUBLANES, 1, 6 * D_MODEL)

    z = _in_proj(x2, norm_mix_gain[0].reshape(1, D_MODEL), mod3, 0, 1, _wz_prep(w_in[0].T))

    cs = _rope_table(positions.reshape(TOKENS, 1))

    wq = w_uq[0].reshape(Q_RANK, MLA_HEADS, QK_HEAD)
    wq_heads = jnp.concatenate([wq, _rotate_half_cols(wq[:, :, QK_NOPE:])], axis=-1)
    wq_heads = wq_heads.astype(BF16).reshape(Q_RANK, MLA_HEADS * 2 * LANES)
    q = _q_proj(z, q_lat_gain[0].reshape(1, Q_RANK), wq_heads, cs)
    k, vt = _kv_proj(z, kv_lat_gain[0].reshape(1, KV_RANK), w_ukv[0].astype(BF16), cs)
    o_mla = _attention(q, k, vt)

    n_st = SSM_GROUPS * SSM_STATE
    abar_re, abar_im, bbar_re, bbar_im = _s5_disc(
        ssm_lam_re[0].reshape(n_st, 1), ssm_lam_im[0].reshape(n_st, 1),
        jnp.repeat(ssm_log_dt[0], SSM_STATE).reshape(n_st, 1),
        ssm_b_re[0].reshape(n_st, SSM_GROUP_CH), ssm_b_im[0].reshape(n_st, SSM_GROUP_CH))
    n_gb = SSM_GROUPS // SSM_GB
    wb = jnp.concatenate([_block_diag_in(bbar_re), _block_diag_in(bbar_im)], axis=-1)
    wb = wb.reshape(SSM_PAIRS, 2 * SSM_GB_CH, 2 * SSM_GB_ST).astype(BF16)
    wc = jnp.concatenate([_block_diag_out(ssm_c_re[0]), -_block_diag_out(ssm_c_im[0])], axis=1)
    wc = wc.reshape(SSM_PAIRS, 2, 2 * SSM_GB_ST, SSM_GB_CH).transpose(0, 2, 1, 3)
    wc = wc.reshape(SSM_PAIRS, 2 * SSM_GB_ST, 2 * SSM_GB_CH).astype(BF16)
    a16 = jnp.concatenate([abar_re.reshape(n_gb, SSM_GB_ST), abar_im.reshape(n_gb, SSM_GB_ST)], axis=-1)
    a8 = jnp.repeat(a16.reshape(SSM_PAIRS, 2, 2 * SSM_GB_ST), BATCH, axis=1)

    y = _s5_scan(z.reshape(BATCH, SEQ, Z_COLS), wb, a8, wc, ssm_d[0].reshape(1, SSM_WIDTH))
    o_ssm = _glu(y.reshape(TOKENS, SSM_WIDTH), w_glu[0].astype(BF16), b_glu[0].reshape(1, SSM_WIDTH),
                 ssm_out_gain[0].reshape(1, SSM_WIDTH))

    x1 = _out_proj(o_mla, o_ssm, mla_out_gain[0].reshape(1, MLA_WIDTH), w_out[0].astype(BF16), x2, mod3, 2)

    w_r = jnp.concatenate([w_group_router[0], w_expert_router[0],
                           jnp.zeros((D_MODEL, LANES - N_EGROUPS - N_EXPERTS), F32)], axis=1)
    w_r_hi = w_r.astype(BF16)
    w_r_lo = (w_r - w_r_hi.astype(F32)).astype(BF16)
    b_r = jnp.concatenate([b_group_router[0], b_expert_router[0],
                           jnp.zeros((LANES - N_EGROUPS - N_EXPERTS,), F32)]).reshape(1, LANES)
    h2, eid, wt = _router(x1, norm_ffn_gain[0].reshape(1, D_MODEL), mod3, 3, 4,
                          jnp.concatenate([w_r_hi, w_r_lo], axis=1), b_r)

    row_dst, blk_e, blk_nvalid, n_used = _moe_plan(eid[:, :TOP_K])
    hmid = _expert_up(row_dst, blk_e, blk_nvalid, n_used, h2, w1_experts[0], w3_experts[0])
    ys = _expert_down(row_dst, blk_e, blk_nvalid, n_used, hmid, w2_experts[0])
    out = _combine(ys, x1, wt, mod3, 5, final_gain.reshape(1, D_MODEL))
    return out.reshape(BATCH, SEQ, D_MODEL)
```

```python
import math

import jax
import jax.numpy as jnp
from jax import lax
from jax.experimental import pallas as pl
from jax.experimental.pallas import tpu as pltpu

F32 = jnp.float32
BF16 = jnp.bfloat16

D_MODEL = 4096
BATCH = 4
SEQ = 2048
TOKENS = BATCH * SEQ
CHUNK = 64
EPS = 1e-6

MLA_HEADS = 16
QK_NOPE = 128
QK_ROPE = 64
QK_HEAD = QK_NOPE + QK_ROPE
V_HEAD = 128
Q_RANK = 768
KV_RANK = 512
ROPE_THETA = 10000.0
MLA_WIDTH = MLA_HEADS * V_HEAD

SSM_WIDTH = D_MODEL - MLA_WIDTH
SSM_GROUP_CH = 16
SSM_GROUPS = SSM_WIDTH // SSM_GROUP_CH
SSM_STATE = 64

N_EGROUPS = 8
EXPERTS_PER_GROUP = 8
N_EXPERTS = N_EGROUPS * EXPERTS_PER_GROUP
TOP_K = 2
D_EXPERT = 512

LANES = 128
SUBLANES = 8
VMEM_LIMIT = 56 * 1024 * 1024

Z_Q = 0
Z_SSM = Q_RANK
Z_KV = Z_SSM + SSM_WIDTH
Z_KR = Z_KV + KV_RANK
Z_COLS = Z_KR + 2 * QK_ROPE

ATT_BLK = 256
ATT_NBLK = SEQ // ATT_BLK

SSM_GB = 8
SSM_GB_CH = SSM_GB * SSM_GROUP_CH
SSM_GB_ST = SSM_GB * SSM_STATE
SSM_PAIRS = SSM_GROUPS // (2 * SSM_GB)
SSM_TCHUNK = 256

MOE_BLK = 384
MOE_NBLK = (TOKENS * TOP_K + N_EXPERTS * (MOE_BLK - 1)) // MOE_BLK + 1
MOE_ROWS = MOE_NBLK * MOE_BLK
DMA_UNROLL = 8
CMB_TM = 128


def _cparams(n_axes, vmem=VMEM_LIMIT):
    return pltpu.CompilerParams(dimension_semantics=("arbitrary",) * n_axes, vmem_limit_bytes=vmem)


def _ada_kernel(c_ref, w_ref, b_ref, o_ref):
    c = c_ref[...]
    ca = (c * jax.nn.sigmoid(c)).astype(BF16)
    o_ref[...] = jnp.dot(ca, w_ref[...].astype(BF16), preferred_element_type=F32) + b_ref[...]


def _ada(c8, w, b):
    n = w.shape[1]
    tn = 512
    return pl.pallas_call(
        _ada_kernel,
        grid=(n // tn,),
        in_specs=[pl.BlockSpec((SUBLANES, D_MODEL), lambda j: (0, 0)),
                  pl.BlockSpec((D_MODEL, tn), lambda j: (0, j)),
                  pl.BlockSpec((1, tn), lambda j: (0, j))],
        out_specs=pl.BlockSpec((SUBLANES, tn), lambda j: (0, j)),
        out_shape=jax.ShapeDtypeStruct((SUBLANES, n), F32),
        compiler_params=_cparams(1),
        name="ada_mod",
    )(c8, w, b)


def _rms(x, gain):
    return x * lax.rsqrt(jnp.mean(x * x, axis=-1, keepdims=True) + EPS) * gain


def _in_proj_kernel(x_ref, g_ref, sh_ref, sc_ref, w_ref, o_ref, h_ref):
    @pl.when(pl.program_id(1) == 0)
    def _():
        y = _rms(x_ref[...], g_ref[...])
        h_ref[...] = (y * (1.0 + sc_ref[...]) + sh_ref[...]).astype(h_ref.dtype)

    o_ref[...] = lax.dot_general(h_ref[...], w_ref[...], (((1,), (1,)), ((), ())), preferred_element_type=F32)


def _in_proj(x2, gain, mod3, shift_chunk, scale_chunk, w_t):
    tm = 512
    n = w_t.shape[0]
    tn = n // 3
    per_b = SEQ // tm
    return pl.pallas_call(
        _in_proj_kernel,
        grid=(TOKENS // tm, n // tn),
        in_specs=[pl.BlockSpec((tm, D_MODEL), lambda i, j: (i, 0)),
                  pl.BlockSpec((1, D_MODEL), lambda i, j: (0, 0)),
                  pl.BlockSpec((None, 1, D_MODEL), lambda i, j: (i // per_b, 0, shift_chunk)),
                  pl.BlockSpec((None, 1, D_MODEL), lambda i, j: (i // per_b, 0, scale_chunk)),
                  pl.BlockSpec((tn, D_MODEL), lambda i, j: (j, 0))],
        out_specs=pl.BlockSpec((tm, tn), lambda i, j: (i, j)),
        out_shape=jax.ShapeDtypeStruct((TOKENS, n), F32),
        scratch_shapes=[pltpu.VMEM((tm, D_MODEL), BF16)],
        compiler_params=_cparams(2),
        name="in_proj",
    )(x2, gain, mod3, mod3, w_t)


W_IN_KV = Q_RANK
W_IN_KR = Q_RANK + KV_RANK
W_IN_SSM = W_IN_KR + QK_ROPE
W_IN_COLS = W_IN_SSM + SSM_WIDTH


def _wz_kernel(w_ref, o_ref):
    o_ref[Z_Q:Z_Q + Q_RANK] = w_ref[:W_IN_KV].astype(o_ref.dtype)
    o_ref[Z_SSM:Z_SSM + SSM_WIDTH] = w_ref[W_IN_SSM:].astype(o_ref.dtype)
    o_ref[Z_KV:Z_KV + KV_RANK] = w_ref[W_IN_KV:W_IN_KR].astype(o_ref.dtype)
    half = QK_ROPE // 2
    o_ref[Z_KR:Z_KR + QK_ROPE] = w_ref[W_IN_KR:W_IN_SSM].astype(o_ref.dtype)
    o_ref[Z_KR + QK_ROPE:Z_KR + QK_ROPE + half] = (-w_ref[W_IN_KR + half:W_IN_SSM]).astype(o_ref.dtype)
    o_ref[Z_KR + QK_ROPE + half:] = w_ref[W_IN_KR:W_IN_KR + half].astype(o_ref.dtype)


def _wz_prep(w_in_t):
    tk = 512
    return pl.pallas_call(
        _wz_kernel,
        grid=(D_MODEL // tk,),
        in_specs=[pl.BlockSpec((W_IN_COLS, tk), lambda i: (0, i))],
        out_specs=pl.BlockSpec((Z_COLS, tk), lambda i: (0, i)),
        out_shape=jax.ShapeDtypeStruct((Z_COLS, D_MODEL), BF16),
        compiler_params=_cparams(1),
        name="in_proj_weight_prep",
    )(w_in_t)


def _rope_table_kernel(pos_ref, o_ref):
    lane = lax.broadcasted_iota(jnp.int32, (1, LANES), 1)
    pair = (lane % (QK_ROPE // 2)).astype(F32)
    inv_freq = jnp.exp(-math.log(ROPE_THETA) * (2.0 * pair) / QK_ROPE)
    ang = pos_ref[...].astype(F32) * inv_freq
    o_ref[...] = jnp.where(lane < QK_ROPE, jnp.cos(ang), jnp.sin(ang))


def _rope_table(pos_col):
    tm = 1024
    return pl.pallas_call(
        _rope_table_kernel,
        grid=(TOKENS // tm,),
        in_specs=[pl.BlockSpec((tm, 1), lambda i: (i, 0))],
        out_specs=pl.BlockSpec((tm, LANES), lambda i: (i, 0)),
        out_shape=jax.ShapeDtypeStruct((TOKENS, LANES), F32),
        compiler_params=_cparams(1),
        name="rope_table",
    )(pos_col)


def _rope_pair(t, cs):
    u = t * cs
    return u + pltpu.roll(u, QK_ROPE, axis=1)


def _q_proj_kernel(ql_ref, g_ref, w_ref, cs_ref, o_ref):
    hn = _rms(ql_ref[...], g_ref[...]).astype(BF16)
    cs = cs_ref[...]
    for h in range(MLA_HEADS):
        r = jnp.dot(hn, w_ref[:, h * 2 * LANES:(h + 1) * 2 * LANES], preferred_element_type=F32)
        o_ref[h, :, :QK_NOPE] = r[:, :QK_NOPE].astype(o_ref.dtype)
        o_ref[h, :, QK_NOPE:] = _rope_pair(r[:, QK_NOPE:], cs)[:, :QK_ROPE].astype(o_ref.dtype)


def _q_proj(z, gain, w_heads, cs):
    tm = ATT_BLK
    per_b = SEQ // tm
    return pl.pallas_call(
        _q_proj_kernel,
        grid=(TOKENS // tm,),
        in_specs=[pl.BlockSpec((tm, Q_RANK), lambda i: (i, Z_Q // Q_RANK)),
                  pl.BlockSpec((1, Q_RANK), lambda i: (0, 0)),
                  pl.BlockSpec((Q_RANK, MLA_HEADS * 2 * LANES), lambda i: (0, 0)),
                  pl.BlockSpec((tm, LANES), lambda i: (i, 0))],
        out_specs=pl.BlockSpec((None, MLA_HEADS, tm, QK_HEAD), lambda i: (i // per_b, 0, i % per_b, 0)),
        out_shape=jax.ShapeDtypeStruct((BATCH, MLA_HEADS, SEQ, QK_HEAD), BF16),
        compiler_params=_cparams(1),
        name="mla_q_proj",
    )(z, gain, w_heads, cs)


def _kv_proj_kernel(kva_ref, kvb_ref, kr_ref, g_ref, w_ref, cs_ref, k_ref, vt_ref):
    kvl = jnp.concatenate([kva_ref[...], kvb_ref[...]], axis=1)
    hn = _rms(kvl, g_ref[...]).astype(BF16)
    kr = _rope_pair(kr_ref[...], cs_ref[...])[:, :QK_ROPE].astype(k_ref.dtype)
    width = QK_NOPE + V_HEAD
    for h in range(MLA_HEADS):
        r = jnp.dot(hn, w_ref[:, h * width:(h + 1) * width], preferred_element_type=F32)
        k_ref[h, :, :QK_NOPE] = r[:, :QK_NOPE].astype(k_ref.dtype)
        k_ref[h, :, QK_NOPE:] = kr
        vt_ref[h] = r[:, QK_NOPE:].T.astype(vt_ref.dtype)


def _kv_proj(z, gain, w_heads, cs):
    tm = ATT_BLK
    per_b = SEQ // tm
    half = KV_RANK // 2
    return pl.pallas_call(
        _kv_proj_kernel,
        grid=(TOKENS // tm,),
        in_specs=[pl.BlockSpec((tm, half), lambda i: (i, Z_KV // half)),
                  pl.BlockSpec((tm, half), lambda i: (i, Z_KV // half + 1)),
                  pl.BlockSpec((tm, LANES), lambda i: (i, Z_KR // LANES)),
                  pl.BlockSpec((1, KV_RANK), lambda i: (0, 0)),
                  pl.BlockSpec((KV_RANK, MLA_HEADS * (QK_NOPE + V_HEAD)), lambda i: (0, 0)),
                  pl.BlockSpec((tm, LANES), lambda i: (i, 0))],
        out_specs=[pl.BlockSpec((None, MLA_HEADS, tm, QK_HEAD), lambda i: (i // per_b, 0, i % per_b, 0)),
                   pl.BlockSpec((None, MLA_HEADS, None, V_HEAD, tm), lambda i: (i // per_b, 0, i % per_b, 0, 0))],
        out_shape=[jax.ShapeDtypeStruct((BATCH, MLA_HEADS, SEQ, QK_HEAD), BF16),
                   jax.ShapeDtypeStruct((BATCH, MLA_HEADS, ATT_NBLK, V_HEAD, ATT_BLK), BF16)],
        compiler_params=_cparams(1),
        name="mla_kv_proj",
    )(z, z, z, gain, w_heads, cs)


ATT_HEADS = 8


def _attn_kernel(q_ref, k_ref, vt_ref, o_ref, acc_ref):
    log2_scale = (QK_HEAD ** -0.5) * math.log2(math.e)
    key_chunk = lax.broadcasted_iota(jnp.int32, (ATT_BLK, ATT_BLK), 0) // CHUNK
    qry_chunk = lax.broadcasted_iota(jnp.int32, (ATT_BLK, ATT_BLK), 1) // CHUNK
    diag_mask = key_chunk <= qry_chunk

    qi = pl.program_id(2)
    if True:
        def kv_block(j, stats, n_blk, masked):
            keys = n_blk * ATT_BLK
            k0 = pl.multiple_of(j * ATT_BLK, ATT_BLK)
            sts = [lax.dot_general(k_ref[h, pl.ds(k0, keys), :], q_ref[h],
                                   (((1,), (1,)), ((), ())), preferred_element_type=F32)
                   for h in range(ATT_HEADS)]
            new_stats, ps, alphas = [], [], []
            for h in range(ATT_HEADS):
                m, l = stats[h]
                st = sts[h] * log2_scale
                if masked:
                    mask = diag_mask if n_blk == 1 else jnp.concatenate(
                        [jnp.ones(((n_blk - 1) * ATT_BLK, ATT_BLK), jnp.bool_), diag_mask], axis=0)
                    st = jnp.where(mask, st, -jnp.inf)
                m_new = jnp.maximum(m, jnp.max(st, axis=0, keepdims=True))
                alpha = jnp.exp2(m - m_new)
                p = jnp.exp2(st - m_new)
                new_stats.append((m_new, alpha * l + jnp.sum(p, axis=0, keepdims=True)))
                ps.append(p.astype(BF16))
                alphas.append(alpha)
            for h in range(ATT_HEADS):
                pv = jnp.dot(vt_ref[h, j], ps[h][:ATT_BLK], preferred_element_type=F32)
                for b in range(1, n_blk):
                    pv = pv + jnp.dot(vt_ref[h, j + b], ps[h][b * ATT_BLK:(b + 1) * ATT_BLK],
                                      preferred_element_type=F32)
                acc_ref[h] = alphas[h] * acc_ref[h] + pv
            return tuple(new_stats)

        acc_ref[...] = jnp.zeros_like(acc_ref)
        stats = tuple((jnp.full((1, ATT_BLK), -jnp.inf, F32), jnp.zeros((1, ATT_BLK), F32))
                      for _ in range(ATT_HEADS))
        n_pairs = lax.shift_right_logical(qi, 1)
        stats = lax.fori_loop(0, n_pairs, lambda jj, st: kv_block(2 * jj, st, 2, False), stats)
        stats = lax.cond(qi % 2 == 1, lambda st: kv_block(qi - 1, st, 2, True),
                         lambda st: kv_block(qi, st, 1, True), stats)
        for h in range(ATT_HEADS):
            o_ref[:, h * V_HEAD:(h + 1) * V_HEAD] = (acc_ref[h] / stats[h][1]).T


def _attention(q, k, vt):
    return pl.pallas_call(
        _attn_kernel,
        grid=(BATCH, MLA_HEADS // ATT_HEADS, ATT_NBLK),
        in_specs=[pl.BlockSpec((None, ATT_HEADS, ATT_BLK, QK_HEAD), lambda b, h, i: (b, h, i, 0)),
                  pl.BlockSpec((None, ATT_HEADS, SEQ, QK_HEAD), lambda b, h, i: (b, h, 0, 0)),
                  pl.BlockSpec((None, ATT_HEADS, ATT_NBLK, V_HEAD, ATT_BLK), lambda b, h, i: (b, h, 0, 0, 0))],
        out_specs=pl.BlockSpec((ATT_BLK, ATT_HEADS * V_HEAD), lambda b, h, i: (b * ATT_NBLK + i, h)),
        out_shape=jax.ShapeDtypeStruct((TOKENS, MLA_WIDTH), F32),
        scratch_shapes=[pltpu.VMEM((ATT_HEADS, V_HEAD, ATT_BLK), F32)],
        compiler_params=_cparams(3),
        name="mla_attention",
    )(q, k, vt)


def _s5_disc_kernel(lre_ref, lim_ref, ldt_ref, bre_ref, bim_ref, are_ref, aim_ref, bbre_ref, bbim_ref):
    lam_re = lre_ref[...]
    lam_im = lim_ref[...]
    dt = jnp.exp(ldt_ref[...])
    mag = jnp.exp(lam_re * dt)
    abar_re = mag * jnp.cos(lam_im * dt)
    abar_im = mag * jnp.sin(lam_im * dt)
    nr = abar_re - 1.0
    ni = abar_im
    den = lam_re * lam_re + lam_im * lam_im
    f_re = (nr * lam_re + ni * lam_im) / den
    f_im = (ni * lam_re - nr * lam_im) / den
    are_ref[...] = abar_re
    aim_ref[...] = abar_im
    b_re = bre_ref[...]
    b_im = bim_ref[...]
    bbre_ref[...] = f_re * b_re - f_im * b_im
    bbim_ref[...] = f_re * b_im + f_im * b_re


def _s5_disc(lam_re, lam_im, log_dt, b_re, b_im):
    n = SSM_GROUPS * SSM_STATE
    tm = 1024
    col = pl.BlockSpec((tm, 1), lambda i: (i, 0))
    mat = pl.BlockSpec((tm, SSM_GROUP_CH), lambda i: (i, 0))
    return pl.pallas_call(
        _s5_disc_kernel,
        grid=(n // tm,),
        in_specs=[col, col, col, mat, mat],
        out_specs=[col, col, mat, mat],
        out_shape=[jax.ShapeDtypeStruct((n, 1), F32), jax.ShapeDtypeStruct((n, 1), F32),
                   jax.ShapeDtypeStruct((n, SSM_GROUP_CH), F32), jax.ShapeDtypeStruct((n, SSM_GROUP_CH), F32)],
        compiler_params=_cparams(1),
        name="s5_discretise",
    )(lam_re, lam_im, log_dt, b_re, b_im)


def _s5_scan_kernel(u_ref, wb_ref, a_ref, wc_ref, d_ref, y_ref, lhs_ref, s_ref, yb_ref, h_ref):
    @pl.when(pl.program_id(1) == 0)
    def _():
        h_ref[...] = jnp.zeros_like(h_ref)
        lhs_ref[...] = jnp.zeros_like(lhs_ref)

    for b in range(BATCH):
        lhs_ref[0, pl.ds(b, SSM_TCHUNK, stride=SUBLANES), :] = u_ref[b, :, :SSM_GB_CH]
        lhs_ref[1, pl.ds(b + BATCH, SSM_TCHUNK, stride=SUBLANES), :] = u_ref[b, :, SSM_GB_CH:]
    half = SSM_TCHUNK * SUBLANES // 2
    wb = wb_ref[...]
    for r0 in (0, half):
        um = jnp.concatenate([lhs_ref[0, r0:r0 + half], lhs_ref[1, r0:r0 + half]], axis=1).astype(BF16)
        s_ref[r0:r0 + half] = jnp.dot(um, wb, preferred_element_type=F32)

    a_re = a_ref[:, :SSM_GB_ST]
    a_im = a_ref[:, SSM_GB_ST:]

    def step(t, carry):
        h_re, h_im = carry
        r0 = pl.multiple_of(t * SUBLANES, SUBLANES)
        n_re = a_re * h_re - a_im * h_im + s_ref[pl.ds(r0, SUBLANES), :SSM_GB_ST]
        n_im = a_re * h_im + a_im * h_re + s_ref[pl.ds(r0, SUBLANES), SSM_GB_ST:]
        s_ref[pl.ds(r0, SUBLANES), :SSM_GB_ST] = n_re
        s_ref[pl.ds(r0, SUBLANES), SSM_GB_ST:] = n_im
        return n_re, n_im

    h_re, h_im = lax.fori_loop(0, SSM_TCHUNK, step, (h_ref[:, :SSM_GB_ST], h_ref[:, SSM_GB_ST:]), unroll=8)
    h_ref[:, :SSM_GB_ST] = h_re
    h_ref[:, SSM_GB_ST:] = h_im

    wc = wc_ref[...]
    for r0 in (0, half):
        y8 = jnp.dot(s_ref[r0:r0 + half].astype(BF16), wc, preferred_element_type=F32)
        yb_ref[0, r0:r0 + half] = y8[:, :SSM_GB_CH]
        yb_ref[1, r0:r0 + half] = y8[:, SSM_GB_CH:]
    d = d_ref[...]
    for b in range(BATCH):
        y_ref[b, :, :SSM_GB_CH] = (yb_ref[0, pl.ds(b, SSM_TCHUNK, stride=SUBLANES), :]
                                   + d[:, :SSM_GB_CH] * u_ref[b, :, :SSM_GB_CH])
        y_ref[b, :, SSM_GB_CH:] = (yb_ref[1, pl.ds(b + BATCH, SSM_TCHUNK, stride=SUBLANES), :]
                                   + d[:, SSM_GB_CH:] * u_ref[b, :, SSM_GB_CH:])


def _s5_scan(z3, wb, a8, wc, d_row):
    rows = SSM_TCHUNK * SUBLANES
    width = 2 * SSM_GB_CH
    states = 2 * SSM_GB_ST
    return pl.pallas_call(
        _s5_scan_kernel,
        grid=(SSM_PAIRS, SEQ // SSM_TCHUNK),
        in_specs=[pl.BlockSpec((BATCH, SSM_TCHUNK, width), lambda g, c: (0, c, Z_SSM // width + g)),
                  pl.BlockSpec((None, width, states), lambda g, c: (g, 0, 0)),
                  pl.BlockSpec((None, SUBLANES, states), lambda g, c: (g, 0, 0)),
                  pl.BlockSpec((None, states, width), lambda g, c: (g, 0, 0)),
                  pl.BlockSpec((1, width), lambda g, c: (0, g))],
        out_specs=pl.BlockSpec((BATCH, SSM_TCHUNK, width), lambda g, c: (0, c, g)),
        out_shape=jax.ShapeDtypeStruct((BATCH, SEQ, SSM_WIDTH), F32),
        scratch_shapes=[pltpu.VMEM((2, rows, SSM_GB_CH), F32), pltpu.VMEM((rows, states), F32),
                        pltpu.VMEM((2, rows, SSM_GB_CH), F32), pltpu.VMEM((SUBLANES, states), F32)],
        compiler_params=_cparams(2),
        name="s5_scan",
    )(z3, wb, a8, wc, d_row)


def _gelu_tanh(x):
    return 0.5 * x * (1.0 + jnp.tanh(math.sqrt(2.0 / math.pi) * (x + 0.044715 * (x * x * x))))


def _glu_kernel(y_ref, w_ref, b_ref, g_ref, o_ref):
    g = _gelu_tanh(y_ref[...])
    gate = jax.nn.sigmoid(jnp.dot(g.astype(BF16), w_ref[...], preferred_element_type=F32) + b_ref[...])
    o_ref[...] = _rms(g * gate, g_ref[...]).astype(o_ref.dtype)


def _glu(y, w, b, gain):
    tm = 256
    return pl.pallas_call(
        _glu_kernel,
        grid=(TOKENS // tm,),
        in_specs=[pl.BlockSpec((tm, SSM_WIDTH), lambda i: (i, 0)),
                  pl.BlockSpec((SSM_WIDTH, SSM_WIDTH), lambda i: (0, 0)),
                  pl.BlockSpec((1, SSM_WIDTH), lambda i: (0, 0)),
                  pl.BlockSpec((1, SSM_WIDTH), lambda i: (0, 0))],
        out_specs=pl.BlockSpec((tm, SSM_WIDTH), lambda i: (i, 0)),
        out_shape=jax.ShapeDtypeStruct((TOKENS, SSM_WIDTH), BF16),
        compiler_params=_cparams(1),
        name="s5_glu",
    )(y, w, b, gain)


def _out_proj_kernel(om_ref, os_ref, g_ref, w_ref, x_ref, ga_ref, o_ref, lhs_ref):
    @pl.when(pl.program_id(1) == 0)
    def _():
        lhs_ref[:, :MLA_WIDTH] = _rms(om_ref[...], g_ref[...]).astype(BF16)
        lhs_ref[:, MLA_WIDTH:] = os_ref[...]

    o_ref[...] = x_ref[...] + ga_ref[...] * jnp.dot(lhs_ref[...], w_ref[...], preferred_element_type=F32)


def _out_proj(o_mla, o_ssm, gain, w, x2, mod3, gate_chunk):
    tm, tn = 512, 1024
    per_b = SEQ // tm
    nj = D_MODEL // tn
    return pl.pallas_call(
        _out_proj_kernel,
        grid=(TOKENS // tm, nj),
        in_specs=[pl.BlockSpec((tm, MLA_WIDTH), lambda i, j: (i, 0)),
                  pl.BlockSpec((tm, SSM_WIDTH), lambda i, j: (i, 0)),
                  pl.BlockSpec((1, MLA_WIDTH), lambda i, j: (0, 0)),
                  pl.BlockSpec((D_MODEL, tn), lambda i, j: (0, j)),
                  pl.BlockSpec((tm, tn), lambda i, j: (i, j)),
                  pl.BlockSpec((None, 1, tn), lambda i, j: (i // per_b, 0, gate_chunk * nj + j))],
        out_specs=pl.BlockSpec((tm, tn), lambda i, j: (i, j)),
        out_shape=jax.ShapeDtypeStruct((TOKENS, D_MODEL), F32),
        scratch_shapes=[pltpu.VMEM((tm, D_MODEL), BF16)],
        compiler_params=_cparams(2),
        name="out_proj",
    )(o_mla, o_ssm, gain, w, x2, mod3)


def _router_kernel(x_ref, g_ref, sh_ref, sc_ref, w_ref, br_ref, h_ref, eid_ref, wt_ref):
    h = _rms(x_ref[...], g_ref[...]) * (1.0 + sc_ref[...]) + sh_ref[...]
    h_ref[...] = h
    h_hi = h.astype(BF16)
    h_lo = (h - h_hi.astype(F32)).astype(BF16)

    tm = h.shape[0]
    r = jnp.dot(jnp.concatenate([h_hi, h_lo], axis=0), w_ref[...], preferred_element_type=F32)
    logits = r[:tm, :LANES] + r[:tm, LANES:] + r[tm:, :LANES] + br_ref[...]

    col = lax.broadcasted_iota(jnp.int32, logits.shape, 1)
    neg = -jnp.inf
    is_g = col < N_EGROUPS
    gl = jnp.where(is_g, logits, neg)
    gmax = jnp.max(gl, axis=1, keepdims=True)
    grp = jnp.min(jnp.where(gl == gmax, col, LANES), axis=1, keepdims=True)
    p_grp = 1.0 / jnp.sum(jnp.where(is_g, jnp.exp(logits - gmax), 0.0), axis=1, keepdims=True)

    ecol = col - N_EGROUPS
    in_grp = (ecol >= 0) & (ecol < N_EXPERTS) & ((ecol // EXPERTS_PER_GROUP) == grp)
    el = jnp.where(in_grp, logits, neg)
    v1 = jnp.max(el, axis=1, keepdims=True)
    i1 = jnp.min(jnp.where(el == v1, col, LANES), axis=1, keepdims=True)
    el2 = jnp.where(col == i1, neg, el)
    v2 = jnp.max(el2, axis=1, keepdims=True)
    i2 = jnp.min(jnp.where(el2 == v2, col, LANES), axis=1, keepdims=True)

    s2 = jnp.exp(v2 - v1)
    w1 = p_grp / (1.0 + s2)
    w2 = p_grp * s2 / (1.0 + s2)
    eid_ref[...] = jnp.where(col == 0, i1 - N_EGROUPS, jnp.where(col == 1, i2 - N_EGROUPS, 0))
    wt_ref[...] = jnp.where(col == 0, w1, jnp.where(col == 1, w2, 0.0))


def _router(x1, gain, mod3, shift_chunk, scale_chunk, w_hilo, b_row):
    tm = 256
    per_b = SEQ // tm
    return pl.pallas_call(
        _router_kernel,
        grid=(TOKENS // tm,),
        in_specs=[pl.BlockSpec((tm, D_MODEL), lambda i: (i, 0)),
                  pl.BlockSpec((1, D_MODEL), lambda i: (0, 0)),
                  pl.BlockSpec((None, 1, D_MODEL), lambda i: (i // per_b, 0, shift_chunk)),
                  pl.BlockSpec((None, 1, D_MODEL), lambda i: (i // per_b, 0, scale_chunk)),
                  pl.BlockSpec((D_MODEL, 2 * LANES), lambda i: (0, 0)),
                  pl.BlockSpec((1, LANES), lambda i: (0, 0))],
        out_specs=[pl.BlockSpec((tm, D_MODEL), lambda i: (i, 0)),
                   pl.BlockSpec((tm, LANES), lambda i: (i, 0)),
                   pl.BlockSpec((tm, LANES), lambda i: (i, 0))],
        out_shape=[jax.ShapeDtypeStruct((TOKENS, D_MODEL), F32),
                   jax.ShapeDtypeStruct((TOKENS, LANES), jnp.int32),
                   jax.ShapeDtypeStruct((TOKENS, LANES), F32)],
        compiler_params=_cparams(1),
        name="ffn_router",
    )(x1, gain, mod3, mod3, w_hilo, b_row)


def _moe_plan(eid):
    n_assign = TOKENS * TOP_K
    e_flat = eid.reshape(n_assign)
    onehot = (e_flat[:, None] == jnp.arange(N_EXPERTS, dtype=jnp.int32)[None, :]).astype(jnp.int32)
    csum = jnp.cumsum(onehot, axis=0)
    counts = csum[-1]
    rank = jnp.sum((csum - onehot) * onehot, axis=1)
    nblk_e = (counts + MOE_BLK - 1) // MOE_BLK
    bend = jnp.cumsum(nblk_e)
    bstart = bend - nblk_e
    dest = jnp.sum(onehot * bstart[None, :], axis=1) * MOE_BLK + rank
    a_id = jnp.arange(n_assign, dtype=jnp.int32)
    row_dst = jnp.zeros((MOE_ROWS,), jnp.int32).at[dest].set((a_id % TOP_K) * TOKENS + a_id // TOP_K,
                                                              unique_indices=True)
    blk = jnp.arange(MOE_NBLK, dtype=jnp.int32)
    n_used = bend[-1]
    blk_e = jnp.minimum(jnp.sum((bend[None, :] <= blk[:, None]).astype(jnp.int32), axis=1), N_EXPERTS - 1)
    blk_nvalid = jnp.clip(counts[blk_e] - (blk - bstart[blk_e]) * MOE_BLK, 0, MOE_BLK)
    blk_nvalid = jnp.where(blk < n_used, blk_nvalid, 0).astype(jnp.int32)
    last_e = blk_e[jnp.maximum(n_used - 1, 0)]
    blk_e = jnp.where(blk < n_used, blk_e, last_e).astype(jnp.int32)
    return row_dst, blk_e, blk_nvalid, n_used.reshape(1).astype(jnp.int32)


def _copy_groups(n_rows):
    return lax.shift_right_logical(n_rows + (DMA_UNROLL - 1), DMA_UNROLL.bit_length() - 1)


def _expert_up_kernel(dst_ref, be_ref, nv_ref, nu_ref, h_hbm, w1_ref, w3_ref, o_ref, xbuf, sem):
    i = pl.program_id(0)
    slot = i % 2

    def row_copy(src_row, dst_slot, dst_row):
        return pltpu.make_async_copy(h_hbm.at[pl.ds(src_row, 1)], xbuf.at[dst_slot, pl.ds(dst_row, 1)],
                                     sem.at[dst_slot])

    def issue(block, dst_slot):
        base = block * MOE_BLK

        def body(g, c):
            for u in range(DMA_UNROLL):
                r = g * DMA_UNROLL + u
                row_copy(dst_ref[base + r] & (TOKENS - 1), dst_slot, r).start()
            return c

        lax.fori_loop(0, _copy_groups(nv_ref[block]), body, 0)

    @pl.when(i == 0)
    def _():
        xbuf[...] = jnp.zeros_like(xbuf)
        issue(0, 0)

    @pl.when(i + 1 < pl.num_programs(0))
    def _():
        issue(i + 1, 1 - slot)

    def wait_body(g, c):
        for u in range(DMA_UNROLL):
            row_copy(0, slot, g * DMA_UNROLL + u).wait()
        return c

    lax.fori_loop(0, _copy_groups(nv_ref[i]), wait_body, 0)

    @pl.when(i < nu_ref[0])
    def _():
        x = xbuf[slot].astype(BF16)
        a = jnp.dot(x, w1_ref[...].astype(BF16), preferred_element_type=F32)
        b = jnp.dot(x, w3_ref[...].astype(BF16), preferred_element_type=F32)
        o_ref[...] = (a * jax.nn.sigmoid(a) * b).astype(o_ref.dtype)

    @pl.when(i >= nu_ref[0])
    def _():
        o_ref[...] = jnp.zeros_like(o_ref)


def _expert_up(row_dst, blk_e, blk_nvalid, n_used, h2, w1, w3):
    wspec = pl.BlockSpec((None, D_MODEL, D_EXPERT), lambda i, tok, be, ng, nu: (be[i], 0, 0))
    return pl.pallas_call(
        _expert_up_kernel,
        grid_spec=pltpu.PrefetchScalarGridSpec(
            num_scalar_prefetch=4,
            grid=(MOE_NBLK,),
            in_specs=[pl.BlockSpec(memory_space=pl.ANY), wspec, wspec],
            out_specs=pl.BlockSpec((MOE_BLK, D_EXPERT), lambda i, tok, be, ng, nu: (i, 0)),
            scratch_shapes=[pltpu.VMEM((2, MOE_BLK, D_MODEL), F32),
                            pltpu.SemaphoreType.DMA((2,))]),
        out_shape=jax.ShapeDtypeStruct((MOE_ROWS, D_EXPERT), BF16),
        compiler_params=_cparams(1, 60 * 1024 * 1024),
        name="moe_expert_up",
    )(row_dst, blk_e, blk_nvalid, n_used, h2, w1, w3)


def _expert_down_kernel(dst_ref, be_ref, nv_ref, nu_ref, h_ref, w_ref, y_hbm, ybuf, sem):
    i = pl.program_id(0)
    slot = i % 2

    def row_copy(src_row, dst_row, src_slot):
        return pltpu.make_async_copy(ybuf.at[src_slot, pl.ds(src_row, 1)], y_hbm.at[pl.ds(dst_row, 1)],
                                     sem.at[src_slot])

    def drain(block, src_slot):
        nv = nv_ref[block]
        full = lax.shift_right_logical(nv, DMA_UNROLL.bit_length() - 1)

        def body(g, c):
            for u in range(DMA_UNROLL):
                row_copy(g * DMA_UNROLL + u, 0, src_slot).wait()
            return c

        lax.fori_loop(0, full, body, 0)

        def tail(r, c):
            row_copy(r, 0, src_slot).wait()
            return c

        lax.fori_loop(full * DMA_UNROLL, nv, tail, 0)

    @pl.when(i >= 2)
    def _():
        drain(i - 2, slot)

    @pl.when(i < nu_ref[0])
    def _():
        ybuf[slot] = jnp.dot(h_ref[...], w_ref[...].astype(BF16), preferred_element_type=F32)
        base = i * MOE_BLK
        nv = nv_ref[i]
        full = lax.shift_right_logical(nv, DMA_UNROLL.bit_length() - 1)

        def body(g, c):
            for u in range(DMA_UNROLL):
                r = g * DMA_UNROLL + u
                row_copy(r, dst_ref[base + r], slot).start(priority=u % 2)
            return c

        lax.fori_loop(0, full, body, 0)

        def tail(r, c):
            row_copy(r, dst_ref[base + r], slot).start()
            return c

        lax.fori_loop(full * DMA_UNROLL, nv, tail, 0)

    @pl.when(i == pl.num_programs(0) - 1)
    def _():
        drain(i - 1, 1 - slot)
        drain(i, slot)


def _expert_down(row_dst, blk_e, blk_nvalid, n_used, hmid, w2):
    return pl.pallas_call(
        _expert_down_kernel,
        grid_spec=pltpu.PrefetchScalarGridSpec(
            num_scalar_prefetch=4,
            grid=(MOE_NBLK,),
            in_specs=[pl.BlockSpec((MOE_BLK, D_EXPERT), lambda i, dst, be, nv, nu: (i, 0)),
                      pl.BlockSpec((None, D_EXPERT, D_MODEL), lambda i, dst, be, nv, nu: (be[i], 0, 0))],
            out_specs=pl.BlockSpec(memory_space=pl.ANY),
            scratch_shapes=[pltpu.VMEM((2, MOE_BLK, D_MODEL), F32),
                            pltpu.SemaphoreType.DMA((2,))]),
        out_shape=jax.ShapeDtypeStruct((TOP_K * TOKENS, D_MODEL), F32),
        compiler_params=_cparams(1),
        name="moe_expert_down",
    )(row_dst, blk_e, blk_nvalid, n_used, hmid, w2)


def _combine_kernel(y0_ref, y1_ref, x_ref, wt_ref, gf_ref, fg_ref, o_ref):
    wt = wt_ref[...]
    moe = wt[:, 0:1] * y0_ref[...] + wt[:, 1:2] * y1_ref[...]
    o_ref[...] = _rms(x_ref[...] + gf_ref[...] * moe, fg_ref[...])


def _combine(ys, x1, wt, mod3, gate_chunk, final_gain):
    per_b = SEQ // CMB_TM
    nblk = TOKENS // CMB_TM
    return pl.pallas_call(
        _combine_kernel,
        grid=(nblk,),
        in_specs=[pl.BlockSpec((CMB_TM, D_MODEL), lambda i: (i, 0)),
                  pl.BlockSpec((CMB_TM, D_MODEL), lambda i: (i + nblk, 0)),
                  pl.BlockSpec((CMB_TM, D_MODEL), lambda i: (i, 0)),
                  pl.BlockSpec((CMB_TM, LANES), lambda i: (i, 0)),
                  pl.BlockSpec((None, 1, D_MODEL), lambda i: (i // per_b, 0, gate_chunk)),
                  pl.BlockSpec((1, D_MODEL), lambda i: (0, 0))],
        out_specs=pl.BlockSpec((CMB_TM, D_MODEL), lambda i: (i, 0)),
        out_shape=jax.ShapeDtypeStruct((TOKENS, D_MODEL), F32),
        compiler_params=_cparams(1),
        name="moe_combine_final_norm",
    )(ys, ys, x1, wt, mod3, final_gain)


def _rotate_half_cols(w):
    half = w.shape[-1] // 2
    return jnp.concatenate([-w[..., half:], w[..., :half]], axis=-1)


def _block_diag_in(bbar):
    bb = bbar.reshape(SSM_GROUPS // SSM_GB, SSM_GB, SSM_STATE, SSM_GROUP_CH)
    w = jnp.einsum('bgpn,gh->bgnhp', bb, jnp.eye(SSM_GB, dtype=bbar.dtype))
    return w.reshape(SSM_GROUPS // SSM_GB, SSM_GB_CH, SSM_GB_ST)


def _block_diag_out(cmat):
    cc = cmat.reshape(SSM_GROUPS // SSM_GB, SSM_GB, SSM_GROUP_CH, SSM_STATE)
    w = jnp.einsum('bgnp,gh->bgphn', cc, jnp.eye(SSM_GB, dtype=cmat.dtype))
    return w.reshape(SSM_GROUPS // SSM_GB, SSM_GB_ST, SSM_GB_CH)


def kernel(x, c, positions, w_ada, b_ada, norm_mix_gain, w_in, q_lat_gain, w_uq, kv_lat_gain, w_ukv,
           ssm_lam_re, ssm_lam_im, ssm_log_dt, ssm_b_re, ssm_b_im, ssm_c_re, ssm_c_im, ssm_d,
           w_glu, b_glu, mla_out_gain, ssm_out_gain, w_out, norm_ffn_gain,
           w_group_router, b_group_router, w_expert_router, b_expert_router,
           w1_experts, w3_experts, w2_experts, final_gain):
    x2 = x.reshape(TOKENS, D_MODEL)

    c8 = jnp.zeros((SUBLANES, D_MODEL), F32).at[:BATCH].set(c)
    mod = _ada(c8, w_ada[0], b_ada[0].reshape(1, 6 * D_MODEL))
    mod3 = mod.reshape(SUBLANES, 1, 6 * D_MODEL)

    z = _in_proj(x2, norm_mix_gain[0].reshape(1, D_MODEL), mod3, 0, 1, _wz_prep(w_in[0].T))

    cs = _rope_table(positions.reshape(TOKENS, 1))

    wq = w_uq[0].reshape(Q_RANK, MLA_HEADS, QK_HEAD)
    wq_heads = jnp.concatenate([wq, _rotate_half_cols(wq[:, :, QK_NOPE:])], axis=-1)
    wq_heads = wq_heads.astype(BF16).reshape(Q_RANK, MLA_HEADS * 2 * LANES)
    q = _q_proj(z, q_lat_gain[0].reshape(1, Q_RANK), wq_heads, cs)
    k, vt = _kv_proj(z, kv_lat_gain[0].reshape(1, KV_RANK), w_ukv[0].astype(BF16), cs)
    o_mla = _attention(q, k, vt)

    n_st = SSM_GROUPS * SSM_STATE
    abar_re, abar_im, bbar_re, bbar_im = _s5_disc(
        ssm_lam_re[0].reshape(n_st, 1), ssm_lam_im[0].reshape(n_st, 1),
        jnp.repeat(ssm_log_dt[0], SSM_STATE).reshape(n_st, 1),
        ssm_b_re[0].reshape(n_st, SSM_GROUP_CH), ssm_b_im[0].reshape(n_st, SSM_GROUP_CH))
    n_gb = SSM_GROUPS // SSM_GB
    wb = jnp.concatenate([_block_diag_in(bbar_re), _block_diag_in(bbar_im)], axis=-1)
    wb = wb.reshape(SSM_PAIRS, 2 * SSM_GB_CH, 2 * SSM_GB_ST).astype(BF16)
    wc = jnp.concatenate([_block_diag_out(ssm_c_re[0]), -_block_diag_out(ssm_c_im[0])], axis=1)
    wc = wc.reshape(SSM_PAIRS, 2, 2 * SSM_GB_ST, SSM_GB_CH).transpose(0, 2, 1, 3)
    wc = wc.reshape(SSM_PAIRS, 2 * SSM_GB_ST, 2 * SSM_GB_CH).astype(BF16)
    a16 = jnp.concatenate([abar_re.reshape(n_gb, SSM_GB_ST), abar_im.reshape(n_gb, SSM_GB_ST)], axis=-1)
    a8 = jnp.repeat(a16.reshape(SSM_PAIRS, 2, 2 * SSM_GB_ST), BATCH, axis=1)

    y = _s5_scan(z.reshape(BATCH, SEQ, Z_COLS), wb, a8, wc, ssm_d[0].reshape(1, SSM_WIDTH))
    o_ssm = _glu(y.reshape(TOKENS, SSM_WIDTH), w_glu[0].astype(BF16), b_glu[0].reshape(1, SSM_WIDTH),
                 ssm_out_gain[0].reshape(1, SSM_WIDTH))

    x1 = _out_proj(o_mla, o_ssm, mla_out_gain[0].reshape(1, MLA_WIDTH), w_out[0].astype(BF16), x2, mod3, 2)

    w_r = jnp.concatenate([w_group_router[0], w_expert_router[0],
                           jnp.zeros((D_MODEL, LANES - N_EGROUPS - N_EXPERTS), F32)], axis=1)
    w_r_hi = w_r.astype(BF16)
    w_r_lo = (w_r - w_r_hi.astype(F32)).astype(BF16)
    b_r = jnp.concatenate([b_group_router[0], b_expert_router[0],
                           jnp.zeros((LANES - N_EGROUPS - N_EXPERTS,), F32)]).reshape(1, LANES)
    h2, eid, wt = _router(x1, norm_ffn_gain[0].reshape(1, D_MODEL), mod3, 3, 4,
                          jnp.concatenate([w_r_hi, w_r_lo], axis=1), b_r)

    row_dst, blk_e, blk_nvalid, n_used = _moe_plan(eid[:, :TOP_K])
    hmid = _expert_up(row_dst, blk_e, blk_nvalid, n_used, h2, w1_experts[0], w3_experts[0])
    ys = _expert_down(row_dst, blk_e, blk_nvalid, n_used, hmid, w2_experts[0])
    out = _combine(ys, x1, wt, mod3, 5, final_gain.reshape(1, D_MODEL))
    return out.reshape(BATCH, SEQ, D_MODEL)
```
